```python
import jax, jax.numpy as jnp
from jax import lax
import numpy as np

D_MODEL = 1024
BATCH = 8
SEQ = 2048
DEPTH = 4
DEC_BATCH = 32
DEC_SEQ = 1
PAST_LEN = 8192
PAGE_SIZE = 128

N_ATT_LAYERS = (DEPTH + 1) // 2
N_SSM_LAYERS = DEPTH // 2
H_A = 8
DH_A = 64
H_B = 8
DK_B = 64
DV_B = 128
RET_DECAY_BASE = 5.0
ROPE_THETA = 10000.0
Q_BLOCK = 128
CHUNK = 128
D_INNER = 2 * D_MODEL
HEAD_DIM_C = 64
H_C = D_INNER // HEAD_DIM_C
N_GROUPS = 4
HPG = H_C // N_GROUPS
D_STATE = 128
CONV_W = 4
CONV_DIM = D_INNER + 2 * N_GROUPS * D_STATE
N_EXPERTS = 32
TOP_K = 4
D_FF = D_MODEL
SWIGLU_LIMIT = 7.0
SWIGLU_ALPHA = 1.702
N_ADA = 6
ATT_IN = 3 * H_A * DH_A + H_A + 2 * H_B * DK_B + 2 * H_B * DV_B
ATT_OUT = H_A * DH_A + H_B * DV_B
SSM_IN = D_INNER + CONV_DIM + H_C

kernel_name = 'fox_retnet_ssd_moe_adaln_step'

F32 = jnp.float32


def rmsnorm(x, g, eps=1e-6):
    xf = x.astype(F32)
    return (xf * lax.rsqrt(jnp.mean(xf * xf, axis=-1, keepdims=True) + eps)).astype(x.dtype) * g


def modulation(c, w, b):
    m = jax.nn.silu(c) @ w + b
    return jnp.split(m[:, None, :], N_ADA, axis=-1)


def modulate(x, g, shift, scale):
    return rmsnorm(x, g) * (1 + scale) + shift


def rope(x, pos):
    half = x.shape[-1] // 2
    freq = ROPE_THETA ** (-jnp.arange(half, dtype=F32) / half)
    ang = pos.astype(F32)[:, None] * freq[None, :]
    cos = jnp.cos(ang)[None, :, None, :]
    sin = jnp.sin(ang)[None, :, None, :]
    x1 = x[..., :half].astype(F32)
    x2 = x[..., half:].astype(F32)
    return jnp.concatenate([x1 * cos - x2 * sin, x1 * sin + x2 * cos], axis=-1).astype(x.dtype)


def decay_chunk(state, q, k, v, log_a):
    L = q.shape[1]
    cum = jnp.cumsum(log_a.astype(F32), axis=1)
    causal = jnp.tril(jnp.ones((L, L), dtype=bool))[None, :, :, None, None]
    seg = cum[:, :, None] - cum[:, None, :]
    decay = jnp.exp(jnp.where(causal, seg, -jnp.inf))
    scores = jnp.einsum('btgn,bsgn->btsg', q, k)[..., None] * decay
    y = jnp.einsum('btsgh,bsghv->btghv', scores, v)
    y = y + jnp.exp(cum)[..., None] * jnp.einsum('btgn,bghnv->btghv', q, state)
    tail = jnp.exp(cum[:, -1:] - cum)
    new_state = (jnp.exp(cum[:, -1])[..., None, None] * state
                 + jnp.einsum('bsgn,bsghv->bghnv', k, v * tail[..., None]))
    return y.astype(v.dtype), new_state.astype(state.dtype)


def chunked_scan(state0, q, k, v, log_a):
    bsz, s = q.shape[:2]
    n = s // CHUNK

    def split(a):
        return jnp.moveaxis(a.reshape((bsz, n, CHUNK) + a.shape[2:]), 1, 0)

    def step(st, inp):
        y, st = decay_chunk(st, *inp)
        return st, y

    st, ys = lax.scan(step, state0, (split(q), split(k), split(v), split(log_a)))
    return jnp.moveaxis(ys, 0, 1).reshape((bsz, s) + ys.shape[3:]), st


def fox_attend(q, k, v, fq, fk, q_pos, k_pos):
    s = jnp.einsum('bqhd,bkhd->bhqk', q, k).astype(F32) * (DH_A ** -0.5)
    bias = jnp.transpose(fq, (0, 2, 1))[..., :, None] - jnp.transpose(fk, (0, 2, 1))[..., None, :]
    s = jnp.where((k_pos[None, :] <= q_pos[:, None])[None, None], s + bias, -jnp.inf)
    p = jax.nn.softmax(s, axis=-1)
    return jnp.einsum('bhqk,bkhd->bqhd', p.astype(v.dtype), v)


def fox_prompt(q, k, v, logf):
    bsz, s = q.shape[:2]
    nb = s // Q_BLOCK
    fcum = jnp.cumsum(logf, axis=1)
    qb = jnp.moveaxis(q.reshape(bsz, nb, Q_BLOCK, H_A, DH_A), 1, 0)
    fb = jnp.moveaxis(fcum.reshape(bsz, nb, Q_BLOCK, H_A), 1, 0)
    k_pos = jnp.arange(s)

    def block(args):
        qi, fi, i = args
        return fox_attend(qi, k, v, fi, fcum, i * Q_BLOCK + jnp.arange(Q_BLOCK), k_pos)

    out = lax.map(block, (qb, fb, jnp.arange(nb)))
    return jnp.moveaxis(out, 0, 1).reshape(bsz, s, H_A, DH_A)


def att_ret_project(h, pos, w_in, b_f):
    bsz, L, _ = h.shape
    p = h @ w_in
    sizes = (H_A * DH_A, H_A * DH_A, H_A * DH_A, H_A, H_B * DK_B, H_B * DK_B, H_B * DV_B, H_B * DV_B)
    idx = [int(i) for i in np.cumsum(sizes)[:-1]]
    qa, ka, va, fa, qr, kr, vr, gr = jnp.split(p, idx, axis=-1)
    qa = qa.reshape(bsz, L, H_A, DH_A)
    ka = ka.reshape(bsz, L, H_A, DH_A)
    va = va.reshape(bsz, L, H_A, DH_A)
    logf = jax.nn.log_sigmoid((fa + b_f).astype(F32))
    qr = rope(qr.reshape(bsz, L, H_B, DK_B), pos)
    kr = rope(kr.reshape(bsz, L, H_B, DK_B), pos) * (DK_B ** -0.5)
    vr = vr.reshape(bsz, L, H_B, 1, DV_B)
    log_gamma = jnp.log1p(-jnp.exp2(-RET_DECAY_BASE - jnp.arange(H_B, dtype=F32)))
    log_a = jnp.broadcast_to(log_gamma[:, None], (bsz, L, H_B, 1))
    return qa, ka, va, logf, qr, kr, vr, log_a, gr


def att_ret_output(o_fox, o_ret, gate, gn_g, gn_b, w_out):
    bsz, L = o_fox.shape[:2]
    r = o_ret.reshape(bsz, L, H_B, DV_B).astype(F32)
    mu = jnp.mean(r, axis=-1, keepdims=True)
    var = jnp.mean(jnp.square(r - mu), axis=-1, keepdims=True)
    r = ((r - mu) * lax.rsqrt(var + 1e-5)).reshape(bsz, L, H_B * DV_B).astype(gate.dtype) * gn_g + gn_b
    r = r * jax.nn.silu(gate)
    o = jnp.concatenate([o_fox.reshape(bsz, L, H_A * DH_A), r], axis=-1)
    return o @ w_out


def ssd_project(h, conv_buf, w_in, conv_w, conv_b, dt_bias, a_log):
    bsz, L, _ = h.shape
    p = h @ w_in
    z = p[..., :D_INNER]
    xbc = p[..., D_INNER:D_INNER + CONV_DIM]
    dt_raw = p[..., D_INNER + CONV_DIM:]
    xpad = jnp.concatenate([conv_buf.astype(xbc.dtype), xbc], axis=1)
    conv = conv_b + conv_w[0] * xpad[:, 0:L]
    for i in range(1, CONV_W):
        conv = conv + conv_w[i] * xpad[:, i:i + L]
    xbc = jax.nn.silu(conv)
    x = xbc[..., :D_INNER].reshape(bsz, L, N_GROUPS, HPG, HEAD_DIM_C)
    bm = xbc[..., D_INNER:D_INNER + N_GROUPS * D_STATE].reshape(bsz, L, N_GROUPS, D_STATE)
    cm = xbc[..., D_INNER + N_GROUPS * D_STATE:].reshape(bsz, L, N_GROUPS, D_STATE)
    dt = jax.nn.softplus((dt_raw + dt_bias).astype(F32)).reshape(bsz, L, N_GROUPS, HPG)
    log_a = -dt * jnp.exp(a_log.astype(F32)).reshape(N_GROUPS, HPG)
    v = x * dt[..., None].astype(x.dtype)
    return cm, bm, v, log_a, x, z, xpad[:, -(CONV_W - 1):]


def ssd_output(y, x, z, d_skip, norm_g, w_out):
    bsz, L = y.shape[:2]
    y = y + d_skip.reshape(N_GROUPS, HPG)[:, :, None] * x
    y = y.reshape(bsz, L, D_INNER) * jax.nn.silu(z)
    yg = y.reshape(bsz, L, N_GROUPS, D_INNER // N_GROUPS).astype(F32)
    yg = yg * lax.rsqrt(jnp.mean(yg * yg, axis=-1, keepdims=True) + 1e-5)
    y = yg.reshape(bsz, L, D_INNER).astype(z.dtype) * norm_g
    return y @ w_out


def moe(x, router_w, router_b, w_up, b_up, w_down, b_down):
    logits = (x @ router_w + router_b).astype(F32)
    top_val, top_idx = lax.top_k(logits, TOP_K)
    probs = jax.nn.softmax(top_val, axis=-1)
    combine = jnp.sum(jax.nn.one_hot(top_idx, N_EXPERTS, dtype=F32) * probs[..., None], axis=1)

    def expert(acc, prm):
        wu, bu, wd, bd, ce = prm
        h = x @ wu + bu
        gate = jnp.minimum(h[:, 0::2], SWIGLU_LIMIT)
        up = jnp.clip(h[:, 1::2], -SWIGLU_LIMIT, SWIGLU_LIMIT)
        act = (up + 1) * gate * jax.nn.sigmoid(SWIGLU_ALPHA * gate)
        return acc + ce[:, None].astype(x.dtype) * (act @ wd + bd), None

    acc, _ = lax.scan(expert, jnp.zeros_like(x), (w_up, b_up, w_down, b_down, combine.T))
    return acc


def setup_inputs(seed: int = 0) -> dict:
    key = jax.random.key(seed)
    ks = iter(jax.random.split(key, 48))

    def nrm(shape, scale=1.0):
        return jax.random.normal(next(ks), shape, F32) * scale

    def unif(shape, lo, hi):
        return jax.random.uniform(next(ks), shape, F32, lo, hi)

    n_pages = PAST_LEN // PAGE_SIZE
    n_used = DEC_BATCH * n_pages
    n_pool = (n_used * 5) // 4
    page_table = jax.random.permutation(next(ks), n_pool)[:n_used].reshape(DEC_BATCH, n_pages).astype(jnp.int32)
    dt0 = jnp.exp(unif((N_SSM_LAYERS, H_C), float(np.log(1e-3)), float(np.log(1e-1))))
    return {
        'x_prompt': nrm((BATCH, SEQ, D_MODEL)),
        'x_sample': nrm((DEC_BATCH, DEC_SEQ, D_MODEL)),
        'c_prompt': nrm((BATCH, D_MODEL)),
        'c_sample': nrm((DEC_BATCH, D_MODEL)),
        'cache_k': nrm((N_ATT_LAYERS, n_pool, PAGE_SIZE, H_A, DH_A)),
        'cache_v': nrm((N_ATT_LAYERS, n_pool, PAGE_SIZE, H_A, DH_A)),
        'cache_logf': jax.nn.log_sigmoid(3.0 + nrm((N_ATT_LAYERS, n_pool, PAGE_SIZE, H_A))),
        'page_table': page_table,
        'state_ret': nrm((N_ATT_LAYERS, DEC_BATCH, H_B, DK_B, DV_B), 0.5),
        'state_ssm': nrm((N_SSM_LAYERS, DEC_BATCH, H_C, D_STATE, HEAD_DIM_C), 0.5),
        'state_conv': nrm((N_SSM_LAYERS, DEC_BATCH, CONV_W - 1, CONV_DIM)),
        'ada_w': nrm((DEPTH, D_MODEL, N_ADA * D_MODEL), 0.5 * D_MODEL ** -0.5),
        'ada_b': nrm((DEPTH, N_ADA * D_MODEL), 0.02),
        'norm_mix_g': 1.0 + nrm((DEPTH, D_MODEL), 0.02),
        'norm_ffn_g': 1.0 + nrm((DEPTH, D_MODEL), 0.02),
        'norm_final_g': 1.0 + nrm((D_MODEL,), 0.02),
        'att_w_in': nrm((N_ATT_LAYERS, D_MODEL, ATT_IN), D_MODEL ** -0.5),
        'att_b_f': unif((N_ATT_LAYERS, H_A), 1.0, 5.0),
        'ret_gn_g': 1.0 + nrm((N_ATT_LAYERS, H_B * DV_B), 0.02),
        'ret_gn_b': nrm((N_ATT_LAYERS, H_B * DV_B), 0.02),
        'att_w_out': nrm((N_ATT_LAYERS, ATT_OUT, D_MODEL), ATT_OUT ** -0.5),
        'ssm_w_in': nrm((N_SSM_LAYERS, D_MODEL, SSM_IN), D_MODEL ** -0.5),
        'ssm_conv_w': unif((N_SSM_LAYERS, CONV_W, CONV_DIM), -CONV_W ** -0.5, CONV_W ** -0.5),
        'ssm_conv_b': nrm((N_SSM_LAYERS, CONV_DIM), 0.02),
        'ssm_dt_bias': dt0 + jnp.log(-jnp.expm1(-dt0)),
        'ssm_a_log': jnp.log(unif((N_SSM_LAYERS, H_C), 1.0, 16.0)),
        'ssm_d': 1.0 + nrm((N_SSM_LAYERS, H_C), 0.1),
        'ssm_norm_g': 1.0 + nrm((N_SSM_LAYERS, D_INNER), 0.02),
        'ssm_w_out': nrm((N_SSM_LAYERS, D_INNER, D_MODEL), D_INNER ** -0.5),
        'router_w': nrm((DEPTH, D_MODEL, N_EXPERTS), D_MODEL ** -0.5),
        'router_b': nrm((DEPTH, N_EXPERTS), 0.01),
        'exp_w_up': nrm((DEPTH, N_EXPERTS, D_MODEL, 2 * D_FF), D_MODEL ** -0.5),
        'exp_b_up': nrm((DEPTH, N_EXPERTS, 2 * D_FF), 0.02),
        'exp_w_down': nrm((DEPTH, N_EXPERTS, D_FF, D_MODEL), D_FF ** -0.5),
        'exp_b_down': nrm((DEPTH, N_EXPERTS, D_MODEL), 0.02),
    }


def reference(x_prompt, x_sample, c_prompt, c_sample, cache_k, cache_v, cache_logf, page_table,
              state_ret, state_ssm, state_conv, ada_w, ada_b, norm_mix_g, norm_ffn_g, norm_final_g,
              att_w_in, att_b_f, ret_gn_g, ret_gn_b, att_w_out, ssm_w_in, ssm_conv_w, ssm_conv_b,
              ssm_dt_bias, ssm_a_log, ssm_d, ssm_norm_g, ssm_w_out, router_w, router_b,
              exp_w_up, exp_b_up, exp_w_down, exp_b_down):
    bp, sp = x_prompt.shape[:2]
    bs, ss = x_sample.shape[:2]
    n_pages = page_table.shape[1]
    past = n_pages * PAGE_SIZE
    pos_p = jnp.arange(sp)
    pos_s = PAST_LEN + jnp.arange(ss)
    yp, ys = x_prompt, x_sample
    k_p, v_p, f_p, k_s, v_s, f_s, ret_p, ret_s = [], [], [], [], [], [], [], []
    ssm_p, ssm_s, conv_p, conv_s = [], [], [], []
    for l in range(DEPTH):
        j = l // 2
        mp = modulation(c_prompt, ada_w[l], ada_b[l])
        ms = modulation(c_sample, ada_w[l], ada_b[l])
        hp = modulate(yp, norm_mix_g[l], mp[0], mp[1])
        hs = modulate(ys, norm_mix_g[l], ms[0], ms[1])
        if l % 2 == 0:
            qa, ka, va, logf, qr, kr, vr, log_a, gate = att_ret_project(hp, pos_p, att_w_in[j], att_b_f[j])
            o_fox = fox_prompt(qa, ka, va, logf)
            o_ret, st = chunked_scan(jnp.zeros((bp, H_B, 1, DK_B, DV_B), hp.dtype), qr, kr, vr, log_a)
            op = att_ret_output(o_fox, o_ret, gate, ret_gn_g[j], ret_gn_b[j], att_w_out[j])
            k_p.append(ka)
            v_p.append(va)
            f_p.append(logf)
            ret_p.append(st[:, :, 0])
            qa, ka, va, logf, qr, kr, vr, log_a, gate = att_ret_project(hs, pos_s, att_w_in[j], att_b_f[j])
            k_all = jnp.concatenate([cache_k[j][page_table].reshape(bs, past, H_A, DH_A).astype(ka.dtype), ka], axis=1)
            v_all = jnp.concatenate([cache_v[j][page_table].reshape(bs, past, H_A, DH_A).astype(va.dtype), va], axis=1)
            f_all = jnp.concatenate([cache_logf[j][page_table].reshape(bs, past, H_A).astype(F32), logf], axis=1)
            f_all = jnp.cumsum(f_all, axis=1)
            o_fox = fox_attend(qa, k_all, v_all, f_all[:, past:], f_all, pos_s, jnp.arange(past + ss))
            o_ret, st = decay_chunk(state_ret[j][:, :, None], qr, kr, vr, log_a)
            os_ = att_ret_output(o_fox, o_ret, gate, ret_gn_g[j], ret_gn_b[j], att_w_out[j])
            k_s.append(ka)
            v_s.append(va)
            f_s.append(logf)
            ret_s.append(st[:, :, 0])
        else:
            prm = (ssm_w_in[j], ssm_conv_w[j], ssm_conv_b[j], ssm_dt_bias[j], ssm_a_log[j])
            cm, bm, v, log_a, xh, z, buf = ssd_project(hp, jnp.zeros((bp, CONV_W - 1, CONV_DIM), hp.dtype), *prm)
            y, st = chunked_scan(jnp.zeros((bp, N_GROUPS, HPG, D_STATE, HEAD_DIM_C), hp.dtype), cm, bm, v, log_a)
            op = ssd_output(y, xh, z, ssm_d[j], ssm_norm_g[j], ssm_w_out[j])
            ssm_p.append(st.reshape(bp, H_C, D_STATE, HEAD_DIM_C))
            conv_p.append(buf)
            cm, bm, v, log_a, xh, z, buf = ssd_project(hs, state_conv[j], *prm)
            y, st = decay_chunk(state_ssm[j].reshape(bs, N_GROUPS, HPG, D_STATE, HEAD_DIM_C), cm, bm, v, log_a)
            os_ = ssd_output(y, xh, z, ssm_d[j], ssm_norm_g[j], ssm_w_out[j])
            ssm_s.append(st.reshape(bs, H_C, D_STATE, HEAD_DIM_C))
            conv_s.append(buf)
        yp = yp + mp[2] * op
        ys = ys + ms[2] * os_
        hp = modulate(yp, norm_ffn_g[l], mp[3], mp[4])
        hs = modulate(ys, norm_ffn_g[l], ms[3], ms[4])
        tok = jnp.concatenate([hp.reshape(-1, D_MODEL), hs.reshape(-1, D_MODEL)], axis=0)
        out = moe(tok, router_w[l], router_b[l], exp_w_up[l], exp_b_up[l], exp_w_down[l], exp_b_down[l])
        yp = yp + mp[5] * out[:bp * sp].reshape(yp.shape)
        ys = ys + ms[5] * out[bp * sp:].reshape(ys.shape)
    y_prompt = rmsnorm(yp, norm_final_g)
    y_sample = rmsnorm(ys, norm_final_g)
    return (y_prompt, y_sample,
            jnp.stack(k_p), jnp.stack(v_p), jnp.stack(f_p),
            jnp.stack(k_s), jnp.stack(v_s), jnp.stack(f_s),
            jnp.stack(ret_p), jnp.stack(ret_s),
            jnp.stack(ssm_p), jnp.stack(ssm_s),
            jnp.stack(conv_p), jnp.stack(conv_s))
```

```python
import functools
import math

import jax
import jax.numpy as jnp
import numpy as np
from jax import lax
from jax.experimental import pallas as pl
from jax.experimental.pallas import tpu as pltpu

F32 = jnp.float32
BF16 = jnp.bfloat16
I32 = jnp.int32

LANES = 128
VMEM_LIMIT = 56 * 1024 * 1024
CHUNK = 128
TOP_K = 4
RET_DECAY_BASE = 5.0
ROPE_THETA = 10000.0
SWIGLU_LIMIT = 7.0
SWIGLU_ALPHA = 1.702
NEG_INF = float("-inf")


def _params(*sem):
    return pltpu.CompilerParams(dimension_semantics=sem, vmem_limit_bytes=VMEM_LIMIT)


def _dot(a, b):
    return jnp.dot(a, b, preferred_element_type=F32)


def _dot_nt(a, b):
    return lax.dot_general(a, b, (((1,), (1,)), ((), ())), preferred_element_type=F32)


def _dot_tn(a, b):
    return lax.dot_general(a, b, (((0,), (0,)), ((), ())), preferred_element_type=F32)


def _split3(x):
    hi = x.astype(BF16)
    r = x - hi.astype(F32)
    mid = r.astype(BF16)
    lo = (r - mid.astype(F32)).astype(BF16)
    return hi, mid, lo


def _dot_sel_rhs(x, m):
    hi, mid, lo = _split3(x)
    return _dot(hi, m) + _dot(mid, m) + _dot(lo, m)


def _dot_sel_lhs(m, x):
    hi, mid, lo = _split3(x)
    return _dot(m, hi) + _dot(m, mid) + _dot(m, lo)


def _dot3(a, b, dot=_dot):
    a_hi = a.astype(BF16)
    a_lo = (a - a_hi.astype(F32)).astype(BF16)
    b_hi = b.astype(BF16)
    b_lo = (b - b_hi.astype(F32)).astype(BF16)
    return dot(a_hi, b_hi) + dot(a_hi, b_lo) + dot(a_lo, b_hi)


def _sigmoid(x):
    return 1.0 / (1.0 + jnp.exp(-x))


def _silu(x):
    return x * _sigmoid(x)


def _softplus(x):
    return jnp.maximum(x, 0.0) + jnp.log(1.0 + jnp.exp(-jnp.abs(x)))


def _log_sigmoid(x):
    return -_softplus(-x)


def _modulate(x, g, shift, scale):
    ms = jnp.mean(x * x, axis=-1, keepdims=True)
    return (x * lax.rsqrt(ms + 1e-6)) * g * (1.0 + scale) + shift


def _iota2(shape, dim):
    return lax.broadcasted_iota(I32, shape, dim)


class _Rows:
    def __init__(self, n_batch, rows_per_batch, tile):
        self.n_batch = n_batch
        self.rows_per_batch = rows_per_batch
        self.rows = n_batch * rows_per_batch
        self.per_row_mod = rows_per_batch == 1
        self.tile = self.rows if self.per_row_mod else min(tile, rows_per_batch)
        assert self.rows % self.tile == 0 and (self.per_row_mod or rows_per_batch % self.tile == 0)
        self.n_tiles = self.rows // self.tile
        self.tiles_per_batch = 1 if self.per_row_mod else rows_per_batch // self.tile

    def spec(self, width, col=0):
        return pl.BlockSpec((self.tile, width), lambda i, *_: (i, col))

    def mod(self, mod_arr, layer, chunk, d):
        if self.per_row_mod:
            return mod_arr, pl.BlockSpec((None, self.rows, d), lambda i, *_: (layer, 0, chunk))
        tpb = self.tiles_per_batch
        arr = mod_arr.reshape(mod_arr.shape[0], mod_arr.shape[1], 1, mod_arr.shape[2])
        return arr, pl.BlockSpec((None, None, 1, d), lambda i, *_: (layer, i // tpb, 0, chunk))


def _full(arr):
    nd = arr.ndim
    return pl.BlockSpec(arr.shape, lambda *_: (0,) * nd)


def _mod_kernel(c_ref, w_ref, b_ref, o_ref):
    c = c_ref[...]
    o_ref[...] = _dot3(_silu(c), w_ref[...]) + b_ref[...]


def _modulation_all(c_all, ada_w, ada_b):
    n_layers, d, n6 = ada_w.shape
    rows = c_all.shape[0]
    tn = n6 // 8 if n6 % (8 * LANES) == 0 else n6
    return pl.pallas_call(
        _mod_kernel,
        out_shape=jax.ShapeDtypeStruct((n_layers, rows, n6), F32),
        grid=(n_layers, n6 // tn),
        in_specs=[pl.BlockSpec((rows, d), lambda l, j: (0, 0)),
                  pl.BlockSpec((None, d, tn), lambda l, j: (l, 0, j)),
                  pl.BlockSpec((None, 1, tn), lambda l, j: (l, 0, j))],
        out_specs=pl.BlockSpec((None, rows, tn), lambda l, j: (l, 0, j)),
        compiler_params=_params("arbitrary", "arbitrary"),
        name="adaln_modulation",
    )(c_all, ada_w, ada_b.reshape(n_layers, 1, n6))


def _proj_kernel(x_ref, g_ref, sh_ref, sc_ref, w_ref, *o_refs, segs):
    h = _modulate(x_ref[...], g_ref[...], sh_ref[...], sc_ref[...]).astype(BF16)
    for o_ref, (start, width) in zip(o_refs, segs):
        o_ref[...] = _dot(h, w_ref[:, start:start + width])


def _project(rows, x, g, mod, layer, w_bf16, segs):
    d = x.shape[1]
    sh_arr, sh_spec = rows.mod(mod, layer, 0, d)
    sc_arr, sc_spec = rows.mod(mod, layer, 1, d)
    return pl.pallas_call(
        functools.partial(_proj_kernel, segs=segs),
        out_shape=[jax.ShapeDtypeStruct((rows.rows, wd), F32) for _, wd in segs],
        grid=(rows.n_tiles,),
        in_specs=[rows.spec(d), _full(g), sh_spec, sc_spec, _full(w_bf16)],
        out_specs=[rows.spec(wd) for _, wd in segs],
        compiler_params=_params("arbitrary"),
        name="norm_mod_project",
    )(x, g, sh_arr, sc_arr, w_bf16)


def _logf_kernel(fa_ref, bf_ref, lf_ref, fc_ref, carry_ref, *, tiles_per_batch):
    i = pl.program_id(0)
    lf = _log_sigmoid(fa_ref[...] + bf_ref[...])
    lf_ref[...] = lf

    @pl.when(i % tiles_per_batch == 0)
    def _():
        carry_ref[...] = jnp.zeros_like(carry_ref)

    tm = lf.shape[0]
    tri = (_iota2((tm, tm), 1) <= _iota2((tm, tm), 0)).astype(BF16)
    cs = _dot_sel_lhs(tri, lf) + carry_ref[...]
    fc_ref[...] = cs
    carry_ref[...] = cs[tm - 1:tm, :]


def _forget_gates(rows, fa_raw, b_f_pad):
    return pl.pallas_call(
        functools.partial(_logf_kernel, tiles_per_batch=rows.tiles_per_batch),
        out_shape=[jax.ShapeDtypeStruct((rows.rows, LANES), F32)] * 2,
        grid=(rows.n_tiles,),
        in_specs=[rows.spec(LANES), _full(b_f_pad)],
        out_specs=[rows.spec(LANES)] * 2,
        scratch_shapes=[pltpu.VMEM((1, LANES), F32)],
        compiler_params=_params("arbitrary"),
        name="forget_gates",
    )(fa_raw, b_f_pad)


def _fox_kernel(q_ref, k_ref, v_ref, fq_ref, fk_ref, o_ref, *, tq, tk, dh, n_blocks):
    qi = pl.program_id(1)
    scale = dh ** -0.5
    hpb = LANES // dh
    lane = _iota2((1, LANES), 1)
    row_ids = qi * tq + _iota2((tq, 1), 0)
    n_kv = (qi * tq + tq + tk - 1) // tk
    for p in range(n_blocks):
        cols = slice(p * LANES, (p + 1) * LANES)
        q2 = q_ref[:, cols] * scale
        qs = [jnp.where(lane // dh == j, q2, 0.0).astype(BF16) for j in range(hpb)]
        fqs = [fq_ref[:, p * hpb + j:p * hpb + j + 1] for j in range(hpb)]

        def body(kv, carry, p=p, cols=cols, qs=qs, fqs=fqs):
            ms, ls, acc = carry
            k0 = pl.multiple_of(kv * tk, tk)
            k2 = k_ref[pl.ds(k0, tk), cols].astype(BF16)
            v2 = v_ref[pl.ds(k0, tk), cols].astype(BF16)
            causal = (k0 + _iota2((1, tk), 1)) <= row_ids
            new_ms, new_ls = [], []
            alpha2 = None
            pv2 = None
            for j in range(hpb):
                h = p * hpb + j
                s = _dot_nt(qs[j], k2) + (fqs[j] - fk_ref[h:h + 1, pl.ds(k0, tk)])
                s = jnp.where(causal, s, NEG_INF)
                m_new = jnp.maximum(ms[j], jnp.max(s, axis=-1, keepdims=True))
                alpha = jnp.exp(ms[j] - m_new)
                pe = jnp.exp(s - m_new)
                new_ms.append(m_new)
                new_ls.append(alpha * ls[j] + jnp.sum(pe, axis=-1, keepdims=True))
                pv = _dot(pe.astype(BF16), v2)
                sel = lane // dh == j
                alpha2 = jnp.where(sel, alpha, 0.0) if alpha2 is None else jnp.where(sel, alpha, alpha2)
                pv2 = jnp.where(sel, pv, 0.0) if pv2 is None else jnp.where(sel, pv, pv2)
            return tuple(new_ms), tuple(new_ls), alpha2 * acc + pv2

        init = (tuple(jnp.full((tq, 1), NEG_INF, F32) for _ in range(hpb)),
                tuple(jnp.zeros((tq, 1), F32) for _ in range(hpb)),
                jnp.zeros((tq, LANES), F32))
        ms, ls, acc = lax.fori_loop(0, n_kv, body, init)
        l2 = None
        for j in range(hpb):
            sel = lane // dh == j
            l2 = jnp.where(sel, ls[j], 1.0) if l2 is None else jnp.where(sel, ls[j], l2)
        o_ref[:, cols] = acc / l2


def _fox_prompt(q, k, v, fq, fk_t, n_batch, seq, dh):
    width = q.shape[1]
    assert width % LANES == 0 and LANES % dh == 0
    tq = tk = min(256, seq)
    nq = seq // tq
    hp = fk_t.shape[1]
    return pl.pallas_call(
        functools.partial(_fox_kernel, tq=tq, tk=tk, dh=dh, n_blocks=width // LANES),
        out_shape=jax.ShapeDtypeStruct(q.shape, F32),
        grid=(n_batch, nq),
        in_specs=[pl.BlockSpec((tq, width), lambda b, i: (b * nq + i, 0)),
                  pl.BlockSpec((seq, width), lambda b, i: (b, 0)),
                  pl.BlockSpec((seq, width), lambda b, i: (b, 0)),
                  pl.BlockSpec((tq, LANES), lambda b, i: (b * nq + i, 0)),
                  pl.BlockSpec((None, hp, seq), lambda b, i: (b, 0, 0))],
        out_specs=pl.BlockSpec((tq, width), lambda b, i: (b * nq + i, 0)),
        compiler_params=_params("arbitrary", "arbitrary"),
        name="fox_prompt_attention",
    )(q, k, v, fq, fk_t)


def _rope_tables(pos, n_heads, dk):
    half = dk // 2
    freq = ROPE_THETA ** (-jnp.arange(half, dtype=F32) / half)
    ang = pos.astype(F32)[:, None] * freq[None, :]
    cos = jnp.cos(ang)
    sin = jnp.sin(ang)
    cos_h = jnp.concatenate([cos, cos], axis=-1)
    sin_h = jnp.concatenate([-sin, sin], axis=-1)
    return jnp.tile(cos_h, (1, n_heads)), jnp.tile(sin_h, (1, n_heads))


def _rope(x, cos, sin, dk):
    half = dk // 2
    lane = _iota2((1, LANES), 1)
    up = pltpu.roll(x, LANES - half, 1)
    down = pltpu.roll(x, half, 1)
    partner = jnp.where((lane % dk) < half, up, down)
    return x * cos + partner * sin


def _ret_kernel(q_ref, k_ref, v_ref, cos_ref, sin_ref, o_ref, st_ref, state, *, n_heads, dk, dv, log_gammas):
    c = pl.program_id(1)
    hpb = LANES // dk

    @pl.when(c == 0)
    def _():
        state[...] = jnp.zeros_like(state)

    lane = _iota2((1, LANES), 1)
    t_col = _iota2((CHUNK, 1), 0).astype(F32)
    seg = (_iota2((CHUNK, CHUNK), 0) - _iota2((CHUNK, CHUNK), 1)).astype(F32)
    causal = seg >= 0.0
    row_head = _iota2((LANES, 1), 0) // dk
    for p in range(n_heads // hpb):
        cols = slice(p * LANES, (p + 1) * LANES)
        cos = cos_ref[:, cols]
        sin = sin_ref[:, cols]
        q2 = _rope(q_ref[:, cols], cos, sin, dk)
        k2 = _rope(k_ref[:, cols], cos, sin, dk) * (dk ** -0.5)
        k2b = k2.astype(BF16)
        st2 = state[p]
        st2b = st2.astype(BF16)
        upd = None
        row_decay = None
        for j in range(hpb):
            h = p * hpb + j
            lg = log_gammas[h]
            sel = lane // dk == j
            qh = jnp.where(sel, q2, 0.0).astype(BF16)
            s = _dot_nt(qh, k2b)
            decay = jnp.exp(jnp.where(causal, seg * lg, NEG_INF))
            vh = v_ref[:, h * dv:(h + 1) * dv]
            y = _dot((s * decay).astype(BF16), vh.astype(BF16))
            y = y + jnp.exp((t_col + 1.0) * lg) * _dot(qh, st2b)
            o_ref[:, h * dv:(h + 1) * dv] = y
            tail = jnp.exp((CHUNK - 1.0 - t_col) * lg)
            kh = jnp.where(sel, k2, 0.0).astype(BF16)
            u = _dot_tn(kh, (vh * tail).astype(BF16))
            upd = u if upd is None else upd + u
            rd = jnp.where(row_head == j, math.exp(CHUNK * lg), 0.0)
            row_decay = rd if row_decay is None else row_decay + rd
        state[p] = row_decay * st2 + upd

    @pl.when(c == pl.num_programs(1) - 1)
    def _():
        for h in range(n_heads):
            p, j = divmod(h, hpb)
            st_ref[h] = state[p, j * dk:(j + 1) * dk, :]


def _retention_prompt(qr, kr, vr, cos, sin, n_batch, seq, n_heads, dk, dv):
    nc = seq // CHUNK
    log_gammas = tuple(math.log1p(-2.0 ** (-RET_DECAY_BASE - h)) for h in range(n_heads))
    wq = n_heads * dk
    wv = n_heads * dv
    assert wq % LANES == 0 and LANES % dk == 0 and dv % LANES == 0
    return pl.pallas_call(
        functools.partial(_ret_kernel, n_heads=n_heads, dk=dk, dv=dv, log_gammas=log_gammas),
        out_shape=[jax.ShapeDtypeStruct((n_batch * seq, wv), F32),
                   jax.ShapeDtypeStruct((n_batch, n_heads, dk, dv), F32)],
        grid=(n_batch, nc),
        in_specs=[pl.BlockSpec((CHUNK, wq), lambda b, c: (b * nc + c, 0)),
                  pl.BlockSpec((CHUNK, wq), lambda b, c: (b * nc + c, 0)),
                  pl.BlockSpec((CHUNK, wv), lambda b, c: (b * nc + c, 0)),
                  pl.BlockSpec((CHUNK, wq), lambda b, c: (c, 0)),
                  pl.BlockSpec((CHUNK, wq), lambda b, c: (c, 0))],
        out_specs=[pl.BlockSpec((CHUNK, wv), lambda b, c: (b * nc + c, 0)),
                   pl.BlockSpec((None, n_heads, dk, dv), lambda b, c: (b, 0, 0, 0))],
        scratch_shapes=[pltpu.VMEM((wq // LANES, LANES, dv), F32)],
        compiler_params=_params("arbitrary", "arbitrary"),
        name="retention_prompt_scan",
    )(qr, kr, vr, cos, sin)


def _att_out_kernel(of_ref, or_ref, gt_ref, y_ref, gm_ref, gg_ref, gb_ref, w1_ref, w2_ref, o_ref, *, n_heads, dv):
    parts = []
    for h in range(n_heads):
        r = or_ref[:, h * dv:(h + 1) * dv]
        mu = jnp.mean(r, axis=-1, keepdims=True)
        d = r - mu
        var = jnp.mean(d * d, axis=-1, keepdims=True)
        parts.append(d * lax.rsqrt(var + 1e-5))
    r = jnp.concatenate(parts, axis=1) * gg_ref[...] + gb_ref[...]
    r = r * _silu(gt_ref[...])
    o = _dot(of_ref[...].astype(BF16), w1_ref[...]) + _dot(r.astype(BF16), w2_ref[...])
    o_ref[...] = y_ref[...] + gm_ref[...] * o


def _att_output(rows, o_fox, o_ret, gate, y, mod, layer, gn_g, gn_b, w1, w2, n_heads, dv):
    d = y.shape[1]
    gm_arr, gm_spec = rows.mod(mod, layer, 2, d)
    return pl.pallas_call(
        functools.partial(_att_out_kernel, n_heads=n_heads, dv=dv),
        out_shape=jax.ShapeDtypeStruct(y.shape, F32),
        grid=(rows.n_tiles,),
        in_specs=[rows.spec(o_fox.shape[1]), rows.spec(o_ret.shape[1]), rows.spec(gate.shape[1]), rows.spec(d),
                  gm_spec, _full(gn_g), _full(gn_b), _full(w1), _full(w2)],
        out_specs=rows.spec(d),
        compiler_params=_params("arbitrary"),
        name="attention_output",
    )(o_fox, o_ret, gate, y, gm_arr, gn_g, gn_b, w1, w2)


def _mm_res_kernel(a_ref, y_ref, gm_ref, w_ref, o_ref):
    o_ref[...] = y_ref[...] + gm_ref[...] * _dot(a_ref[...].astype(BF16), w_ref[...])


def _matmul_residual(rows, a, y, mod, layer, w):
    d = y.shape[1]
    gm_arr, gm_spec = rows.mod(mod, layer, 2, d)
    return pl.pallas_call(
        _mm_res_kernel,
        out_shape=jax.ShapeDtypeStruct(y.shape, F32),
        grid=(rows.n_tiles,),
        in_specs=[rows.spec(a.shape[1]), rows.spec(d), gm_spec, _full(w)],
        out_specs=rows.spec(d),
        compiler_params=_params("arbitrary"),
        name="matmul_gated_residual",
    )(a, y, gm_arr, w)


def _ssd_conv_act(taps, cw_ref, cb_ref):
    conv = cb_ref[...] + cw_ref[0:1, :] * taps[0]
    for i in range(1, len(taps)):
        conv = conv + cw_ref[i:i + 1, :] * taps[i]
    return _silu(conv)


def _ssd_gate_norm(y, x, z, dsk, ng, n_groups):
    y = (y + dsk * x) * _silu(z)
    gw = y.shape[1] // n_groups
    parts = []
    for g in range(n_groups):
        blk = y[:, g * gw:(g + 1) * gw]
        ms = jnp.mean(blk * blk, axis=-1, keepdims=True)
        parts.append(blk * lax.rsqrt(ms + 1e-5))
    return jnp.concatenate(parts, axis=1) * ng


def _ssd_kernel(xr_ref, z_ref, dt_ref, cw_ref, cb_ref, dtb_ref, alog_ref, dsk_ref, ng_ref, e_ref,
                y_ref, st_ref, cv_ref, prev, state, *, d_inner, n_groups, d_state, hd):
    c = pl.program_id(1)
    hpg = d_inner // hd // n_groups
    ppg = hpg * hd // LANES

    @pl.when(c == 0)
    def _():
        prev[...] = jnp.zeros_like(prev)
        state[...] = jnp.zeros_like(state)

    cur = xr_ref[...]
    xw = jnp.concatenate([prev[...], cur], axis=0)
    taps = [xw[5 + i:5 + i + CHUNK] for i in range(3)] + [cur]
    tail_rows = cur[CHUNK - 8:CHUNK]
    prev[...] = tail_rows
    cv_ref[...] = tail_rows
    xbc = _ssd_conv_act(taps, cw_ref, cb_ref)
    gn = n_groups * d_state
    x = xbc[:, :d_inner]
    bm = xbc[:, d_inner:d_inner + gn].astype(BF16)
    cm = xbc[:, d_inner + gn:].astype(BF16)

    dt = _softplus(dt_ref[...] + dtb_ref[...])
    la = -dt * jnp.exp(alog_ref[...])
    seg = _iota2((CHUNK, CHUNK), 0) - _iota2((CHUNK, CHUNK), 1)
    causal = seg >= 0
    cum = _dot_sel_lhs(causal.astype(BF16), la)
    cum_t = cum.T
    e = e_ref[...]
    dt_e = _dot_sel_rhs(dt, e)
    cum_e = _dot_sel_rhs(cum, e)
    ecum_e = jnp.exp(cum_e)
    tail_e = jnp.exp(cum_e[CHUNK - 1:CHUNK, :] - cum_e)
    v = x * dt_e
    vb = v.astype(BF16)
    vtb = (v * tail_e).astype(BF16)
    lane = _iota2((1, LANES), 1)
    hpb = LANES // hd
    ys = []
    for g in range(n_groups):
        cmg = cm[:, g * d_state:(g + 1) * d_state]
        bmg = bm[:, g * d_state:(g + 1) * d_state]
        s = _dot_nt(cmg, bmg)
        for pp in range(ppg):
            p = g * ppg + pp
            cols = slice(p * LANES, (p + 1) * LANES)
            v2 = vb[:, cols]
            yp = None
            for j in range(hpb):
                h = p * hpb + j
                d = jnp.exp(jnp.where(causal, cum[:, h:h + 1] - cum_t[h:h + 1, :], NEG_INF))
                yj = _dot((s * d).astype(BF16), v2)
                sel = lane // hd == j
                yp = jnp.where(sel, yj, 0.0) if yp is None else jnp.where(sel, yj, yp)
            st2 = state[p]
            yp = yp + ecum_e[:, cols] * _dot(cmg, st2.astype(BF16))
            ys.append(yp)
            state[p] = ecum_e[CHUNK - 1:CHUNK, cols] * st2 + _dot_tn(bmg, vtb[:, cols])
    y = jnp.concatenate(ys, axis=1)
    y_ref[...] = _ssd_gate_norm(y, x, z_ref[...], dsk_ref[...], ng_ref[...], n_groups)

    @pl.when(c == pl.num_programs(1) - 1)
    def _():
        st_ref[...] = state[...]


def _ssd_prompt(xr, z, dtr, cw, cb, dtb, alog, dsk, ng, e_mat, n_batch, seq, d_inner, n_groups, d_state, hd):
    nc = seq // CHUNK
    cd = xr.shape[1]
    n_blk = d_inner // LANES
    row = lambda b, c: (b * nc + c, 0)
    return pl.pallas_call(
        functools.partial(_ssd_kernel, d_inner=d_inner, n_groups=n_groups, d_state=d_state, hd=hd),
        out_shape=[jax.ShapeDtypeStruct((n_batch * seq, d_inner), F32),
                   jax.ShapeDtypeStruct((n_batch, n_blk, d_state, LANES), F32),
                   jax.ShapeDtypeStruct((n_batch, 8, cd), F32)],
        grid=(n_batch, nc),
        in_specs=[pl.BlockSpec((CHUNK, cd), row), pl.BlockSpec((CHUNK, d_inner), row),
                  pl.BlockSpec((CHUNK, LANES), row),
                  _full(cw), _full(cb), _full(dtb), _full(alog), _full(dsk), _full(ng), _full(e_mat)],
        out_specs=[pl.BlockSpec((CHUNK, d_inner), row),
                   pl.BlockSpec((None, n_blk, d_state, LANES), lambda b, c: (b, 0, 0, 0)),
                   pl.BlockSpec((None, 8, cd), lambda b, c: (b, 0, 0))],
        scratch_shapes=[pltpu.VMEM((8, cd), F32), pltpu.VMEM((n_blk, d_state, LANES), F32)],
        compiler_params=_params("arbitrary", "arbitrary"),
        name="ssd_prompt_scan",
    )(xr, z, dtr, cw, cb, dtb, alog, dsk, ng, e_mat)


def _ssd_step_prep_kernel(xr_ref, c0_ref, c1_ref, c2_ref, dt_ref, cw_ref, cb_ref, dtb_ref, alog_ref, e_ref,
                          x_ref, bm_ref, cm_ref, v_ref, a_ref, *, d_inner, gn):
    xbc = _ssd_conv_act([c0_ref[...], c1_ref[...], c2_ref[...], xr_ref[...]], cw_ref, cb_ref)
    x = xbc[:, :d_inner]
    x_ref[...] = x
    bm_ref[...] = xbc[:, d_inner:d_inner + gn]
    cm_ref[...] = xbc[:, d_inner + gn:]
    dt = _softplus(dt_ref[...] + dtb_ref[...])
    a_ref[...] = jnp.exp(-dt * jnp.exp(alog_ref[...]))
    v_ref[...] = x * _dot_sel_rhs(dt, e_ref[...])


def _ssd_step_prep(xr, taps, dtr, cw, cb, dtb, alog, e_mat, d_inner, gn):
    n = xr.shape[0]
    args = (xr, *taps, dtr, cw, cb, dtb, alog, e_mat)
    return pl.pallas_call(
        functools.partial(_ssd_step_prep_kernel, d_inner=d_inner, gn=gn),
        out_shape=[jax.ShapeDtypeStruct((n, d_inner), F32), jax.ShapeDtypeStruct((n, gn), F32),
                   jax.ShapeDtypeStruct((n, gn), F32), jax.ShapeDtypeStruct((n, d_inner), F32),
                   jax.ShapeDtypeStruct((n, LANES), F32)],
        grid=(1,),
        in_specs=[_full(a) for a in args],
        out_specs=[pl.BlockSpec((n, d_inner), lambda i: (0, 0)), pl.BlockSpec((n, gn), lambda i: (0, 0)),
                   pl.BlockSpec((n, gn), lambda i: (0, 0)), pl.BlockSpec((n, d_inner), lambda i: (0, 0)),
                   pl.BlockSpec((n, LANES), lambda i: (0, 0))],
        compiler_params=_params("arbitrary"),
        name="ssd_step_prep",
    )(*args)


def _ssd_step_post_kernel(y_ref, x_ref, z_ref, dsk_ref, ng_ref, o_ref, *, n_groups):
    o_ref[...] = _ssd_gate_norm(y_ref[...], x_ref[...], z_ref[...], dsk_ref[...], ng_ref[...], n_groups)


def _ssd_step_post(y, x, z, dsk, ng, n_groups):
    args = (y, x, z, dsk, ng)
    return pl.pallas_call(
        functools.partial(_ssd_step_post_kernel, n_groups=n_groups),
        out_shape=jax.ShapeDtypeStruct(y.shape, F32),
        grid=(1,),
        in_specs=[_full(a) for a in args],
        out_specs=pl.BlockSpec(y.shape, lambda i: (0, 0)),
        compiler_params=_params("arbitrary"),
        name="ssd_step_post",
    )(*args)


def _state_step_kernel(s_ref, a_ref, k_ref, q_ref, v_ref, so_ref, y_ref, *, n_heads, heads_per_key):
    for h in range(n_heads):
        g = h // heads_per_key
        new = a_ref[h] * s_ref[h] + k_ref[g] * v_ref[h]
        so_ref[h] = new
        y_ref[h] = jnp.sum(q_ref[g] * new, axis=0, keepdims=True)


def _state_step(state, a, k_col, q_col, v_row):
    n_b, n_h, n_n, n_v = state.shape
    n_g = k_col.shape[1]
    blk = lambda arr: pl.BlockSpec((None,) + arr.shape[1:], lambda b: (b, 0, 0, 0))
    return pl.pallas_call(
        functools.partial(_state_step_kernel, n_heads=n_h, heads_per_key=n_h // n_g),
        out_shape=[jax.ShapeDtypeStruct(state.shape, F32), jax.ShapeDtypeStruct((n_b, n_h, 1, n_v), F32)],
        grid=(n_b,),
        in_specs=[blk(state), blk(a), blk(k_col), blk(q_col), blk(v_row)],
        out_specs=[blk(state), pl.BlockSpec((None, n_h, 1, n_v), lambda b: (b, 0, 0, 0))],
        compiler_params=_params("arbitrary"),
        name="state_step",
    )(state, a, k_col, q_col, v_row)


def _rope_rows_kernel(q_ref, k_ref, cos_ref, sin_ref, qo_ref, ko_ref, *, dk):
    for p in range(q_ref.shape[1] // LANES):
        cols = slice(p * LANES, (p + 1) * LANES)
        qo_ref[:, cols] = _rope(q_ref[:, cols], cos_ref[:, cols], sin_ref[:, cols], dk)
        ko_ref[:, cols] = _rope(k_ref[:, cols], cos_ref[:, cols], sin_ref[:, cols], dk) * (dk ** -0.5)


def _rope_rows(q, k, cos, sin, dk):
    args = (q, k, cos, sin)
    return pl.pallas_call(
        functools.partial(_rope_rows_kernel, dk=dk),
        out_shape=[jax.ShapeDtypeStruct(q.shape, F32)] * 2,
        grid=(1,),
        in_specs=[_full(a) for a in args],
        out_specs=[pl.BlockSpec(q.shape, lambda i: (0, 0))] * 2,
        compiler_params=_params("arbitrary"),
        name="rope_rows",
    )(*args)


def _fox_decode_kernel(pt_ref, q_ref, kn_ref, vn_ref, lfn_ref, kp_ref, vp_ref, lfp_ref, o_ref,
                       m_ref, l_ref, acc_ref, fc_ref, *, page, n_heads, dh):
    del pt_ref
    pi = pl.program_id(1)
    scale = dh ** -0.5
    n_keys = page * n_heads

    @pl.when(pi == 0)
    def _():
        m_ref[...] = jnp.full_like(m_ref, NEG_INF)
        l_ref[...] = jnp.zeros_like(l_ref)
        acc_ref[...] = jnp.zeros_like(acc_ref)
        fc_ref[...] = jnp.zeros_like(fc_ref)

    q = q_ref[...] * scale
    k2 = kp_ref[...].reshape(n_keys, dh).astype(BF16)
    v2 = vp_ref[...].reshape(n_keys, dh).astype(BF16)
    s = _dot_nt(q.astype(BF16), k2)
    lf = lfp_ref[...]
    tri = (_iota2((page, page), 0) <= _iota2((page, page), 1)).astype(BF16)
    fcum = _dot_sel_rhs(lf, tri) + fc_ref[:, 0:1]
    spread = (_iota2((page, n_keys), 1) // n_heads == _iota2((page, n_keys), 0)).astype(BF16)
    bias = _dot_sel_rhs(fcum, spread)
    own = (_iota2((n_heads, n_keys), 1) % n_heads) == _iota2((n_heads, n_keys), 0)
    s = jnp.where(own, s - bias, NEG_INF)
    m_old = m_ref[:, 0:1]
    m_new = jnp.maximum(m_old, jnp.max(s, axis=-1, keepdims=True))
    alpha = jnp.exp(m_old - m_new)
    pe = jnp.exp(s - m_new)
    l_new = alpha * l_ref[:, 0:1] + jnp.sum(pe, axis=-1, keepdims=True)
    acc = alpha * acc_ref[...] + _dot(pe.astype(BF16), v2)
    f_tot = fcum[:, page - 1:page]
    m_ref[...] = jnp.broadcast_to(m_new, m_ref.shape)
    l_ref[...] = jnp.broadcast_to(l_new, l_ref.shape)
    acc_ref[...] = acc
    fc_ref[...] = jnp.broadcast_to(f_tot, fc_ref.shape)

    @pl.when(pi == pl.num_programs(1) - 1)
    def _():
        kn = kn_ref[...]
        s_new = jnp.sum(q * kn, axis=-1, keepdims=True) - (f_tot + lfn_ref[...])
        m_fin = jnp.maximum(m_new, s_new)
        a2 = jnp.exp(m_new - m_fin)
        p_new = jnp.exp(s_new - m_fin)
        l_fin = a2 * l_new + p_new
        acc_fin = a2 * acc + p_new * vn_ref[...]
        o_ref[...] = acc_fin / l_fin


def _fox_decode(page_table, q, k_new, v_new, lf_new, cache_k, cache_v, lf_pages_t, layer):
    n_b, n_h, dh = q.shape
    n_pages = page_table.shape[1]
    page = cache_k.shape[2]
    tok = lambda arr: pl.BlockSpec((None,) + arr.shape[1:], lambda b, p, pt: (b,) + (0,) * (arr.ndim - 1))
    grid_spec = pltpu.PrefetchScalarGridSpec(
        num_scalar_prefetch=1,
        grid=(n_b, n_pages),
        in_specs=[tok(q), tok(k_new), tok(v_new), tok(lf_new),
                  pl.BlockSpec((None, None, page, n_h, dh), lambda b, p, pt: (layer, pt[b, p], 0, 0, 0)),
                  pl.BlockSpec((None, None, page, n_h, dh), lambda b, p, pt: (layer, pt[b, p], 0, 0, 0)),
                  pl.BlockSpec((None, None, n_h, page), lambda b, p, pt: (layer, pt[b, p], 0, 0))],
        out_specs=pl.BlockSpec((None, n_h, dh), lambda b, p, pt: (b, 0, 0)),
        scratch_shapes=[pltpu.VMEM((n_h, LANES), F32), pltpu.VMEM((n_h, LANES), F32),
                        pltpu.VMEM((n_h, dh), F32), pltpu.VMEM((n_h, LANES), F32)],
    )
    return pl.pallas_call(
        functools.partial(_fox_decode_kernel, page=page, n_heads=n_h, dh=dh),
        out_shape=jax.ShapeDtypeStruct((n_b, n_h, dh), F32),
        grid_spec=grid_spec,
        compiler_params=_params("arbitrary", "arbitrary"),
        name="fox_paged_decode",
    )(page_table, q, k_new, v_new, lf_new, cache_k, cache_v, lf_pages_t)


def _router_kernel(y_ref, g_ref, sh_ref, sc_ref, rwt_ref, rb_ref, cin_ref,
                   h_ref, idx_ref, w_ref, rk_ref, cnt_ref, carry, *, n_exp):
    i = pl.program_id(0)

    @pl.when(i == 0)
    def _():
        carry[...] = cin_ref[...]

    h = _modulate(y_ref[...], g_ref[...], sh_ref[...], sc_ref[...])
    h_ref[...] = h
    tm = h.shape[0]
    logits = _dot3(rwt_ref[...], h, dot=_dot_nt) + rb_ref[...]
    eio = _iota2((n_exp, tm), 0)
    vals, idxs = [], []
    rest = logits
    for _ in range(TOP_K):
        m = jnp.max(rest, axis=0, keepdims=True)
        ik = jnp.min(jnp.where(rest == m, eio, n_exp), axis=0, keepdims=True)
        vals.append(m)
        idxs.append(ik)
        rest = jnp.where(eio == ik, NEG_INF, rest)
    ex = [jnp.exp(v - vals[0]) for v in vals]
    den = ex[0]
    for e in ex[1:]:
        den = den + e
    sel = jnp.zeros((n_exp, tm), F32)
    for ik in idxs:
        sel = sel + jnp.where(eio == ik, 1.0, 0.0)
    before = (_iota2((tm, tm), 0) < _iota2((tm, tm), 1)).astype(BF16)
    rank_all = _dot(sel.astype(BF16), before) + carry[:, 0:1]
    ranks = [jnp.sum(jnp.where(eio == ik, rank_all, 0.0), axis=0, keepdims=True) for ik in idxs]
    carry[...] = carry[...] + jnp.sum(sel, axis=1, keepdims=True)
    idx_ref[...] = jnp.concatenate(idxs, axis=0)
    w_ref[...] = jnp.concatenate([e / den for e in ex], axis=0)
    rk_ref[...] = jnp.concatenate(ranks, axis=0).astype(I32)
    cnt_ref[...] = carry[...]


def _route(rows, y, g, mod, layer, rw_t, rb_col, cnt_in):
    d = y.shape[1]
    n_exp = rw_t.shape[0]
    sh_arr, sh_spec = rows.mod(mod, layer, 3, d)
    sc_arr, sc_spec = rows.mod(mod, layer, 4, d)
    kt = pl.BlockSpec((TOP_K, rows.tile), lambda i: (0, i))
    return pl.pallas_call(
        functools.partial(_router_kernel, n_exp=n_exp),
        out_shape=[jax.ShapeDtypeStruct((rows.rows, d), F32),
                   jax.ShapeDtypeStruct((TOP_K, rows.rows), I32),
                   jax.ShapeDtypeStruct((TOP_K, rows.rows), F32),
                   jax.ShapeDtypeStruct((TOP_K, rows.rows), I32),
                   jax.ShapeDtypeStruct((n_exp, LANES), F32)],
        grid=(rows.n_tiles,),
        in_specs=[rows.spec(d), _full(g), sh_spec, sc_spec, _full(rw_t), _full(rb_col), _full(cnt_in)],
        out_specs=[rows.spec(d), kt, kt, kt, pl.BlockSpec((n_exp, LANES), lambda i: (0, 0))],
        scratch_shapes=[pltpu.VMEM((n_exp, LANES), F32)],
        compiler_params=_params("arbitrary"),
        name="moe_router",
    )(y, g, sh_arr, sc_arr, rw_t, rb_col, cnt_in)


def _dispatch_kernel(slot_ref, h_ref, xs_in_ref, xs_ref, sem, *, tm, n_rows):
    del xs_in_ref
    i = pl.program_id(0)

    def issue(t, carry):
        for k in range(TOP_K):
            s = slot_ref[k * n_rows + i * tm + t]
            pltpu.make_async_copy(h_ref.at[pl.ds(t, 1), :], xs_ref.at[pl.ds(s, 1), :], sem).start()
        return carry

    lax.fori_loop(0, tm, issue, 0)
    for k in range(TOP_K):
        pltpu.make_async_copy(h_ref, xs_ref.at[pl.ds(0, tm), :], sem).wait()


def _dispatch(rows, slots_flat, h, xs):
    d = h.shape[1]
    grid_spec = pltpu.PrefetchScalarGridSpec(
        num_scalar_prefetch=1,
        grid=(rows.n_tiles,),
        in_specs=[pl.BlockSpec((rows.tile, d), lambda i, s: (i, 0)), pl.BlockSpec(memory_space=pl.ANY)],
        out_specs=pl.BlockSpec(memory_space=pl.ANY),
        scratch_shapes=[pltpu.SemaphoreType.DMA(())],
    )
    return pl.pallas_call(
        functools.partial(_dispatch_kernel, tm=rows.tile, n_rows=rows.rows),
        out_shape=jax.ShapeDtypeStruct(xs.shape, F32),
        grid_spec=grid_spec,
        input_output_aliases={2: 0},
        compiler_params=_params("arbitrary"),
        name="moe_dispatch",
    )(slots_flat, h, xs)


def _ffn_kernel(te_ref, tv_ref, xs_ref, wg_ref, wu_ref, bg_ref, bu_ref, wd_ref, bd_ref, ys_ref):
    del te_ref
    i = pl.program_id(0)

    @pl.when(tv_ref[i] == 1)
    def _():
        x = xs_ref[...].astype(BF16)
        gate = jnp.minimum(_dot(x, wg_ref[...]) + bg_ref[...], SWIGLU_LIMIT)
        up = jnp.clip(_dot(x, wu_ref[...]) + bu_ref[...], -SWIGLU_LIMIT, SWIGLU_LIMIT)
        act = (up + 1.0) * gate * _sigmoid(SWIGLU_ALPHA * gate)
        ys_ref[...] = _dot(act.astype(BF16), wd_ref[...]) + bd_ref[...]

    @pl.when(tv_ref[i] == 0)
    def _():
        ys_ref[...] = jnp.zeros_like(ys_ref)


def _expert_ffn(tile_expert, tile_valid, xs, wg, wu, bg, bu, wd, bd, tmf):
    n_slots, d = xs.shape
    f = wg.shape[2]
    ex = lambda i, te, tv: (te[i], 0, 0)
    grid_spec = pltpu.PrefetchScalarGridSpec(
        num_scalar_prefetch=2,
        grid=(n_slots // tmf,),
        in_specs=[pl.BlockSpec((tmf, d), lambda i, te, tv: (i, 0)),
                  pl.BlockSpec((None, d, f), ex), pl.BlockSpec((None, d, f), ex),
                  pl.BlockSpec((None, 1, f), ex), pl.BlockSpec((None, 1, f), ex),
                  pl.BlockSpec((None, f, d), ex), pl.BlockSpec((None, 1, d), ex)],
        out_specs=pl.BlockSpec((tmf, d), lambda i, te, tv: (i, 0)),
    )
    return pl.pallas_call(
        _ffn_kernel,
        out_shape=jax.ShapeDtypeStruct((n_slots, d), F32),
        grid_spec=grid_spec,
        compiler_params=_params("arbitrary"),
        name="moe_expert_ffn",
    )(tile_expert, tile_valid, xs, wg, wu, bg, bu, wd, bd)


def _combine_kernel(slot_ref, ys_ref, w_ref, y_ref, gm_ref, o_ref, buf, sem, *, tm, n_rows):
    i = pl.program_id(0)

    def issue(t, carry):
        for k in range(TOP_K):
            s = slot_ref[k * n_rows + i * tm + t]
            pltpu.make_async_copy(ys_ref.at[pl.ds(s, 1), :], buf.at[k, pl.ds(t, 1), :], sem).start()
        return carry

    lax.fori_loop(0, tm, issue, 0)
    for k in range(TOP_K):
        pltpu.make_async_copy(ys_ref.at[pl.ds(0, tm), :], buf.at[k], sem).wait()
    acc = w_ref[:, 0:1] * buf[0]
    for k in range(1, TOP_K):
        acc = acc + w_ref[:, k:k + 1] * buf[k]
    o_ref[...] = y_ref[...] + gm_ref[...] * acc


def _combine(rows, slots_flat, ys, w_tok, y, mod, layer):
    d = y.shape[1]
    gm_arr, gm_spec = rows.mod(mod, layer, 5, d)
    grid_spec = pltpu.PrefetchScalarGridSpec(
        num_scalar_prefetch=1,
        grid=(rows.n_tiles,),
        in_specs=[pl.BlockSpec(memory_space=pl.ANY),
                  pl.BlockSpec((rows.tile, TOP_K), lambda i, s: (i, 0)),
                  pl.BlockSpec((rows.tile, d), lambda i, s: (i, 0)), gm_spec],
        out_specs=pl.BlockSpec((rows.tile, d), lambda i, s: (i, 0)),
        scratch_shapes=[pltpu.VMEM((TOP_K, rows.tile, d), F32), pltpu.SemaphoreType.DMA(())],
    )
    return pl.pallas_call(
        functools.partial(_combine_kernel, tm=rows.tile, n_rows=rows.rows),
        out_shape=jax.ShapeDtypeStruct(y.shape, F32),
        grid_spec=grid_spec,
        compiler_params=_params("arbitrary"),
        name="moe_combine",
    )(slots_flat, ys, w_tok, y, gm_arr)


def _rmsnorm_kernel(x_ref, g_ref, o_ref):
    x = x_ref[...]
    o_ref[...] = x * lax.rsqrt(jnp.mean(x * x, axis=-1, keepdims=True) + 1e-6) * g_ref[...]


def _final_norm(rows, x, g):
    d = x.shape[1]
    return pl.pallas_call(
        _rmsnorm_kernel,
        out_shape=jax.ShapeDtypeStruct(x.shape, F32),
        grid=(rows.n_tiles,),
        in_specs=[rows.spec(d), _full(g)],
        out_specs=rows.spec(d),
        compiler_params=_params("arbitrary"),
        name="final_rmsnorm",
    )(x, g)


def _moe_layer(groups, ys, layer, g_ffn, router_w, router_b, exp_w_up, exp_b_up, exp_w_down, exp_b_down, tmf):
    n_exp = router_w.shape[1]
    d = router_w.shape[0]
    rw_t = router_w.T
    rb_col = router_b.reshape(n_exp, 1)
    cnt = jnp.zeros((n_exp, LANES), F32)
    routed = []
    for (rows, m), y in zip(groups, ys):
        h, idx, w, rank, cnt = _route(rows, y, g_ffn, m, layer, rw_t, rb_col, cnt)
        routed.append((h, idx, w, rank))
    total = sum(rows.rows for rows, _ in groups) * TOP_K
    n_tiles = -(-total // tmf) + n_exp
    counts = cnt[:, 0].astype(I32)
    tiles_e = (counts + tmf - 1) // tmf
    tile_end = jnp.cumsum(tiles_e)
    starts = (tile_end - tiles_e) * tmf
    tile_ids = jnp.arange(n_tiles, dtype=I32)
    tile_expert = jnp.minimum(jnp.searchsorted(tile_end, tile_ids, side="right"), n_exp - 1).astype(I32)
    tile_valid = (tile_ids < tile_end[-1]).astype(I32)
    xs = jnp.zeros((n_tiles * tmf, d), F32)
    slots = []
    for (rows, _), (h, idx, w, rank) in zip(groups, routed):
        sl = (jnp.take(starts, idx) + rank).reshape(-1)
        slots.append(sl)
        xs = _dispatch(rows, sl, h, xs)
    wg = exp_w_up[:, :, 0::2].astype(BF16)
    wu = exp_w_up[:, :, 1::2].astype(BF16)
    bg = exp_b_up[:, None, 0::2]
    bu = exp_b_up[:, None, 1::2]
    y_sorted = _expert_ffn(tile_expert, tile_valid, xs, wg, wu, bg, bu,
                           exp_w_down.astype(BF16), exp_b_down[:, None, :], tmf)
    outs = []
    for (rows, m), y, sl, (h, idx, w, rank) in zip(groups, ys, slots, routed):
        outs.append(_combine(rows, sl, y_sorted, w.T, y, m, layer))
    return outs


def kernel(x_prompt, x_sample, c_prompt, c_sample, cache_k, cache_v, cache_logf, page_table, state_ret, state_ssm, state_conv, ada_w, ada_b, norm_mix_g, norm_ffn_g, norm_final_g, att_w_in, att_b_f, ret_gn_g, ret_gn_b, att_w_out, ssm_w_in, ssm_conv_w, ssm_conv_b, ssm_dt_bias, ssm_a_log, ssm_d, ssm_norm_g, ssm_w_out, router_w, router_b, exp_w_up, exp_b_up, exp_w_down, exp_b_down):
    bp, sp, d = x_prompt.shape
    bs = x_sample.shape[0]
    depth = ada_w.shape[0]
    n_pages, page = page_table.shape[1], cache_k.shape[2]
    past = n_pages * page
    h_a, dh_a = cache_k.shape[3], cache_k.shape[4]
    h_b, dk_b, dv_b = state_ret.shape[2:]
    h_c, d_state, hd_c = state_ssm.shape[2:]
    d_inner = h_c * hd_c
    conv_w_len, conv_dim = ssm_conv_w.shape[1:]
    gn = (conv_dim - d_inner) // 2
    n_groups = gn // d_state
    assert conv_w_len == 4 and hd_c * 2 == LANES and h_a <= LANES and h_c <= LANES

    rows_p = _Rows(bp, sp, 256)
    rows_s = _Rows(bs, 1, bs)
    yp = x_prompt.reshape(bp * sp, d)
    ys = x_sample.reshape(bs, d)

    mod = _modulation_all(jnp.concatenate([c_prompt, c_sample], axis=0), ada_w, ada_b)
    mod_p, mod_s = mod[:, :bp], mod[:, bp:]

    wa, wr, wvr = h_a * dh_a, h_b * dk_b, h_b * dv_b
    att_segs = []
    off = 0
    for wdt in (wa, wa, wa, wr, wr, wvr, wvr, LANES):
        att_segs.append((off, wdt))
        off += wdt
    ssm_segs = ((0, d_inner), (d_inner, conv_dim), (d_inner + conv_dim, LANES))

    cos_p, sin_p = _rope_tables(jnp.arange(sp), h_b, dk_b)
    cos_s, sin_s = _rope_tables(jnp.full((1,), past), h_b, dk_b)
    log_gammas = [math.log1p(-2.0 ** (-RET_DECAY_BASE - h)) for h in range(h_b)]
    gamma_col = jnp.broadcast_to(jnp.asarray(np.exp(log_gammas), F32).reshape(1, h_b, 1, 1), (bs, h_b, 1, 1))
    lf_pages_t = jnp.swapaxes(cache_logf, 2, 3)
    n_exp = router_w.shape[2]
    tmf = 512 if (bp * sp + bs) * TOP_K >= 512 * n_exp else 64
    e_mat = (jnp.arange(LANES)[:, None] == (jnp.arange(d_inner) // hd_c)[None, :]).astype(BF16)

    k_p, v_p, f_p, k_s, v_s, f_s, ret_p, ret_s = [], [], [], [], [], [], [], []
    ssm_p, ssm_s, conv_p, conv_s = [], [], [], []
    for l in range(depth):
        j = l // 2
        g_mix = norm_mix_g[l][None]
        if l % 2 == 0:
            w = att_w_in[j]
            o0 = 3 * wa
            w_perm = jnp.concatenate(
                [w[:, :o0], w[:, o0 + h_a:], w[:, o0:o0 + h_a], jnp.zeros((d, LANES - h_a), F32)], axis=1).astype(BF16)
            b_f_pad = jnp.pad(att_b_f[j], (0, LANES - h_a))[None]
            w1 = att_w_out[j][:wa].astype(BF16)
            w2 = att_w_out[j][wa:].astype(BF16)
            gn_g, gn_b = ret_gn_g[j][None], ret_gn_b[j][None]
            qa, ka, va, qr, kr, vr, gate, fa = _project(rows_p, yp, g_mix, mod_p, l, w_perm, att_segs)
            logf, fcum = _forget_gates(rows_p, fa, b_f_pad)
            hp8 = -(-h_a // 8) * 8
            fk_t = jnp.transpose(fcum.reshape(bp, sp, LANES)[:, :, :hp8], (0, 2, 1))
            o_fox = _fox_prompt(qa, ka, va, fcum, fk_t, bp, sp, dh_a)
            o_ret, st = _retention_prompt(qr, kr, vr, cos_p, sin_p, bp, sp, h_b, dk_b, dv_b)
            yp = _att_output(rows_p, o_fox, o_ret, gate, yp, mod_p, l, gn_g, gn_b, w1, w2, h_b, dv_b)
            k_p.append(ka.reshape(bp, sp, h_a, dh_a))
            v_p.append(va.reshape(bp, sp, h_a, dh_a))
            f_p.append(logf[:, :h_a].reshape(bp, sp, h_a))
            ret_p.append(st)
            qa, ka, va, qr, kr, vr, gate, fa = _project(rows_s, ys, g_mix, mod_s, l, w_perm, att_segs)
            logf, _ = _forget_gates(rows_s, fa, b_f_pad)
            o_fox = _fox_decode(page_table, qa.reshape(bs, h_a, dh_a), ka.reshape(bs, h_a, dh_a),
                                va.reshape(bs, h_a, dh_a), logf[:, :h_a].reshape(bs, h_a, 1),
                                cache_k, cache_v, lf_pages_t, j).reshape(bs, wa)
            qr2, kr2 = _rope_rows(qr, kr, cos_s, sin_s, dk_b)
            st, y_ret = _state_step(state_ret[j], gamma_col, kr2.reshape(bs, h_b, dk_b, 1),
                                    qr2.reshape(bs, h_b, dk_b, 1), vr.reshape(bs, h_b, 1, dv_b))
            ys = _att_output(rows_s, o_fox, y_ret.reshape(bs, wvr), gate, ys, mod_s, l, gn_g, gn_b, w1, w2, h_b, dv_b)
            k_s.append(ka.reshape(bs, 1, h_a, dh_a))
            v_s.append(va.reshape(bs, 1, h_a, dh_a))
            f_s.append(logf[:, :h_a].reshape(bs, 1, h_a))
            ret_s.append(st)
        else:
            w_pad = jnp.pad(ssm_w_in[j], ((0, 0), (0, LANES - h_c))).astype(BF16)
            cw, cb = ssm_conv_w[j], ssm_conv_b[j][None]
            dtb = jnp.pad(ssm_dt_bias[j], (0, LANES - h_c))[None]
            alog = jnp.pad(ssm_a_log[j], (0, LANES - h_c))[None]
            dsk = jnp.repeat(ssm_d[j], hd_c)[None]
            ng = ssm_norm_g[j][None]
            w_out = ssm_w_out[j].astype(BF16)
            z, xr, dtr = _project(rows_p, yp, g_mix, mod_p, l, w_pad, ssm_segs)
            y_n, st2, cv = _ssd_prompt(xr, z, dtr, cw, cb, dtb, alog, dsk, ng, e_mat, bp, sp,
                                       d_inner, n_groups, d_state, hd_c)
            yp = _matmul_residual(rows_p, y_n, yp, mod_p, l, w_out)
            st = st2.reshape(bp, h_c // 2, d_state, 2, hd_c).transpose(0, 1, 3, 2, 4).reshape(bp, h_c, d_state, hd_c)
            ssm_p.append(st)
            conv_p.append(cv[:, 8 - (conv_w_len - 1):, :])
            z, xr, dtr = _project(rows_s, ys, g_mix, mod_s, l, w_pad, ssm_segs)
            taps = [state_conv[j][:, i, :] for i in range(conv_w_len - 1)]
            x, bm, cm, v, a = _ssd_step_prep(xr, taps, dtr, cw, cb, dtb, alog, e_mat, d_inner, gn)
            st, y_s = _state_step(state_ssm[j], a[:, :h_c].reshape(bs, h_c, 1, 1),
                                  bm.reshape(bs, n_groups, d_state, 1), cm.reshape(bs, n_groups, d_state, 1),
                                  v.reshape(bs, h_c, 1, hd_c))
            y_n = _ssd_step_post(y_s.reshape(bs, d_inner), x, z, dsk, ng, n_groups)
            ys = _matmul_residual(rows_s, y_n, ys, mod_s, l, w_out)
            ssm_s.append(st)
            conv_s.append(jnp.concatenate([state_conv[j][:, 1:, :], xr[:, None, :]], axis=1))
        yp, ys = _moe_layer([(rows_p, mod_p), (rows_s, mod_s)], [yp, ys], l, norm_ffn_g[l][None],
                            router_w[l], router_b[l], exp_w_up[l], exp_b_up[l], exp_w_down[l], exp_b_down[l], tmf)
    y_prompt = _final_norm(rows_p, yp, norm_final_g[None]).reshape(bp, sp, d)
    y_sample = _final_norm(rows_s, ys, norm_final_g[None]).reshape(bs, 1, d)
    return (y_prompt, y_sample,
            jnp.stack(k_p), jnp.stack(v_p), jnp.stack(f_p),
            jnp.stack(k_s), jnp.stack(v_s), jnp.stack(f_s),
            jnp.stack(ret_p), jnp.stack(ret_s),
            jnp.stack(ssm_p), jnp.stack(ssm_s),
            jnp.stack(conv_p), jnp.stack(conv_s))
```

```python
import functools
import math

import jax
import jax.numpy as jnp
import numpy as np
from jax import lax
from jax.experimental import pallas as pl
from jax.experimental.pallas import tpu as pltpu

F32 = jnp.float32
BF16 = jnp.bfloat16
I32 = jnp.int32

LANES = 128
VMEM_LIMIT = 56 * 1024 * 1024
CHUNK = 128
TOP_K = 4
RET_DECAY_BASE = 5.0
ROPE_THETA = 10000.0
SWIGLU_LIMIT = 7.0
SWIGLU_ALPHA = 1.702
NEG_INF = float("-inf")


def _params(*sem):
    return pltpu.CompilerParams(dimension_semantics=sem, vmem_limit_bytes=VMEM_LIMIT)


def _dot(a, b):
    return jnp.dot(a, b, preferred_element_type=F32)


def _dot_nt(a, b):
    return lax.dot_general(a, b, (((1,), (1,)), ((), ())), preferred_element_type=F32)


def _dot_tn(a, b):
    return lax.dot_general(a, b, (((0,), (0,)), ((), ())), preferred_element_type=F32)


def _split3(x):
    hi = x.astype(BF16)
    r = x - hi.astype(F32)
    mid = r.astype(BF16)
    lo = (r - mid.astype(F32)).astype(BF16)
    return hi, mid, lo


def _dot_sel_rhs(x, m):
    hi, mid, lo = _split3(x)
    return _dot(hi, m) + _dot(mid, m) + _dot(lo, m)


def _dot_sel_lhs(m, x):
    hi, mid, lo = _split3(x)
    return _dot(m, hi) + _dot(m, mid) + _dot(m, lo)


def _dot3(a, b, dot=_dot):
    a_hi = a.astype(BF16)
    a_lo = (a - a_hi.astype(F32)).astype(BF16)
    b_hi = b.astype(BF16)
    b_lo = (b - b_hi.astype(F32)).astype(BF16)
    return dot(a_hi, b_hi) + dot(a_hi, b_lo) + dot(a_lo, b_hi)


def _sigmoid(x):
    return 1.0 / (1.0 + jnp.exp(-x))


def _silu(x):
    return x * _sigmoid(x)


def _softplus(x):
    return jnp.maximum(x, 0.0) + jnp.log(1.0 + jnp.exp(-jnp.abs(x)))


def _log_sigmoid(x):
    return -_softplus(-x)


def _modulate(x, g, shift, scale):
    ms = jnp.mean(x * x, axis=-1, keepdims=True)
    return (x * lax.rsqrt(ms + 1e-6)) * g * (1.0 + scale) + shift


def _iota2(shape, dim):
    return lax.broadcasted_iota(I32, shape, dim)


class _Rows:
    def __init__(self, n_batch, rows_per_batch, tile):
        self.n_batch = n_batch
        self.rows_per_batch = rows_per_batch
        self.rows = n_batch * rows_per_batch
        self.per_row_mod = rows_per_batch == 1
        self.tile = self.rows if self.per_row_mod else min(tile, rows_per_batch)
        assert self.rows % self.tile == 0 and (self.per_row_mod or rows_per_batch % self.tile == 0)
        self.n_tiles = self.rows // self.tile
        self.tiles_per_batch = 1 if self.per_row_mod else rows_per_batch // self.tile

    def spec(self, width, col=0):
        return pl.BlockSpec((self.tile, width), lambda i, *_: (i, col))

    def mod(self, mod_arr, layer, chunk, d):
        if self.per_row_mod:
            return mod_arr, pl.BlockSpec((None, self.rows, d), lambda i, *_: (layer, 0, chunk))
        tpb = self.tiles_per_batch
        arr = mod_arr.reshape(mod_arr.shape[0], mod_arr.shape[1], 1, mod_arr.shape[2])
        return arr, pl.BlockSpec((None, None, 1, d), lambda i, *_: (layer, i // tpb, 0, chunk))


def _full(arr):
    nd = arr.ndim
    return pl.BlockSpec(arr.shape, lambda *_: (0,) * nd)


def _mod_kernel(c_ref, w_ref, b_ref, o_ref):
    c = c_ref[...]
    o_ref[...] = _dot3(_silu(c), w_ref[...]) + b_ref[...]


def _modulation_all(c_all, ada_w, ada_b):
    n_layers, d, n6 = ada_w.shape
    rows = c_all.shape[0]
    tn = n6 // 8 if n6 % (8 * LANES) == 0 else n6
    return pl.pallas_call(
        _mod_kernel,
        out_shape=jax.ShapeDtypeStruct((n_layers, rows, n6), F32),
        grid=(n_layers, n6 // tn),
        in_specs=[pl.BlockSpec((rows, d), lambda l, j: (0, 0)),
                  pl.BlockSpec((None, d, tn), lambda l, j: (l, 0, j)),
                  pl.BlockSpec((None, 1, tn), lambda l, j: (l, 0, j))],
        out_specs=pl.BlockSpec((None, rows, tn), lambda l, j: (l, 0, j)),
        compiler_params=_params("arbitrary", "arbitrary"),
        name="adaln_modulation",
    )(c_all, ada_w, ada_b.reshape(n_layers, 1, n6))


def _proj_kernel(x_ref, g_ref, sh_ref, sc_ref, w_ref, *o_refs, segs):
    h = _modulate(x_ref[...], g_ref[...], sh_ref[...], sc_ref[...]).astype(BF16)
    for o_ref, (start, width) in zip(o_refs, segs):
        o_ref[...] = _dot(h, w_ref[:, start:start + width])


def _project(rows, x, g, mod, layer, w_bf16, segs):
    d = x.shape[1]
    sh_arr, sh_spec = rows.mod(mod, layer, 0, d)
    sc_arr, sc_spec = rows.mod(mod, layer, 1, d)
    return pl.pallas_call(
        functools.partial(_proj_kernel, segs=segs),
        out_shape=[jax.ShapeDtypeStruct((rows.rows, wd), F32) for _, wd in segs],
        grid=(rows.n_tiles,),
        in_specs=[rows.spec(d), _full(g), sh_spec, sc_spec, _full(w_bf16)],
        out_specs=[rows.spec(wd) for _, wd in segs],
        compiler_params=_params("arbitrary"),
        name="norm_mod_project",
    )(x, g, sh_arr, sc_arr, w_bf16)


def _logf_kernel(fa_ref, bf_ref, lf_ref, fc_ref, carry_ref, *, tiles_per_batch):
    i = pl.program_id(0)
    lf = _log_sigmoid(fa_ref[...] + bf_ref[...])
    lf_ref[...] = lf

    @pl.when(i % tiles_per_batch == 0)
    def _():
        carry_ref[...] = jnp.zeros_like(carry_ref)

    tm = lf.shape[0]
    tri = (_iota2((tm, tm), 1) <= _iota2((tm, tm), 0)).astype(BF16)
    cs = _dot_sel_lhs(tri, lf) + carry_ref[...]
    fc_ref[...] = cs
    carry_ref[...] = cs[tm - 1:tm, :]


def _forget_gates(rows, fa_raw, b_f_pad):
    return pl.pallas_call(
        functools.partial(_logf_kernel, tiles_per_batch=rows.tiles_per_batch),
        out_shape=[jax.ShapeDtypeStruct((rows.rows, LANES), F32)] * 2,
        grid=(rows.n_tiles,),
        in_specs=[rows.spec(LANES), _full(b_f_pad)],
        out_specs=[rows.spec(LANES)] * 2,
        scratch_shapes=[pltpu.VMEM((1, LANES), F32)],
        compiler_params=_params("arbitrary"),
        name="forget_gates",
    )(fa_raw, b_f_pad)


def _fox_kernel(q_ref, k_ref, v_ref, fc_ref, ft_ref, o_ref, fkb_ref, qm_ref, acc_ref, *, tq, dh, n_blocks, n_heads):
    qi = pl.program_id(1)
    seq = k_ref.shape[0]
    hpb = LANES // dh

    @pl.when(qi == 0)
    def _():
        for h in range(n_heads):
            fkb_ref[h] = jnp.broadcast_to(fc_ref[:, h:h + 1], (seq, LANES))

    q0 = pl.multiple_of(qi * tq, tq)
    lane = _iota2((1, LANES), 1)
    row_head = _iota2((LANES, 1), 0) // dh
    on_or_before = _iota2((tq, tq), 0) <= _iota2((tq, tq), 1)
    for p in range(n_blocks):
        q2 = q_ref[:, p * LANES:(p + 1) * LANES] * (dh ** -0.5)
        for j in range(hpb):
            qm_ref[p * hpb + j] = jnp.where(lane // dh == j, q2, 0.0).astype(BF16)
    acc_ref[...] = jnp.zeros_like(acc_ref)

    def block(k0, carry, diagonal):
        ms, ls = carry
        ss = []
        for p in range(n_blocks):
            k2 = k_ref[pl.ds(k0, tq), p * LANES:(p + 1) * LANES].astype(BF16)
            ss += [_dot_nt(k2, qm_ref[p * hpb + j]) for j in range(hpb)]
        new_ms, new_ls, alphas, pes = [], [], [], []
        for h in range(n_heads):
            fk = fkb_ref[h, pl.ds(k0, tq), :]
            s = ss[h] + (ft_ref[h:h + 1, pl.ds(q0, tq)] - jnp.concatenate([fk] * (tq // LANES), axis=1))
            if diagonal:
                s = jnp.where(on_or_before, s, NEG_INF)
            m_new = jnp.maximum(ms[h], jnp.max(s, axis=0, keepdims=True))
            alphas.append(jnp.exp(ms[h] - m_new))
            pe = jnp.exp(s - m_new)
            new_ms.append(m_new)
            new_ls.append(alphas[h] * ls[h] + jnp.sum(pe, axis=0, keepdims=True))
            pes.append(pe.astype(BF16))
        for p in range(n_blocks):
            v2 = v_ref[pl.ds(k0, tq), p * LANES:(p + 1) * LANES].astype(BF16)
            alpha2 = None
            pv2 = None
            for j in range(hpb):
                h = p * hpb + j
                pv = _dot_tn(v2, pes[h])
                sel = row_head == j
                alpha2 = jnp.where(sel, alphas[h], 0.0) if alpha2 is None else jnp.where(sel, alphas[h], alpha2)
                pv2 = jnp.where(sel, pv, 0.0) if pv2 is None else jnp.where(sel, pv, pv2)
            acc_ref[p] = alpha2 * acc_ref[p] + pv2
        return tuple(new_ms), tuple(new_ls)

    init = (tuple(jnp.full((1, tq), NEG_INF, F32) for _ in range(n_heads)),
            tuple(jnp.zeros((1, tq), F32) for _ in range(n_heads)))
    carry = lax.fori_loop(0, qi, lambda kv, c: block(pl.multiple_of(kv * tq, tq), c, False), init)
    ms, ls = block(q0, carry, True)
    for p in range(n_blocks):
        l2 = None
        for j in range(hpb):
            sel = row_head == j
            l2 = jnp.where(sel, ls[p * hpb + j], 1.0) if l2 is None else jnp.where(sel, ls[p * hpb + j], l2)
        o_ref[:, p * LANES:(p + 1) * LANES] = (acc_ref[p] / l2).T


def _fox_prompt(q, k, v, fcum, fcum_t, n_batch, seq, dh, n_heads):
    width = q.shape[1]
    assert width % LANES == 0 and LANES % dh == 0
    tq = min(256, seq)
    nq = seq // tq
    hp = fcum_t.shape[1]
    return pl.pallas_call(
        functools.partial(_fox_kernel, tq=tq, dh=dh, n_blocks=width // LANES, n_heads=n_heads),
        out_shape=jax.ShapeDtypeStruct(q.shape, F32),
        grid=(n_batch, nq),
        in_specs=[pl.BlockSpec((tq, width), lambda b, i: (b * nq + i, 0)),
                  pl.BlockSpec((seq, width), lambda b, i: (b, 0)),
                  pl.BlockSpec((seq, width), lambda b, i: (b, 0)),
                  pl.BlockSpec((seq, LANES), lambda b, i: (b, 0)),
                  pl.BlockSpec((None, hp, seq), lambda b, i: (b, 0, 0))],
        out_specs=pl.BlockSpec((tq, width), lambda b, i: (b * nq + i, 0)),
        scratch_shapes=[pltpu.VMEM((n_heads, seq, LANES), F32), pltpu.VMEM((n_heads, tq, LANES), BF16),
                        pltpu.VMEM((width // LANES, LANES, tq), F32)],
        compiler_params=_params("arbitrary", "arbitrary"),
        name="fox_prompt_attention",
    )(q, k, v, fcum, fcum_t)


def _rope_tables(pos, n_heads, dk):
    half = dk // 2
    freq = ROPE_THETA ** (-jnp.arange(half, dtype=F32) / half)
    ang = pos.astype(F32)[:, None] * freq[None, :]
    cos = jnp.cos(ang)
    sin = jnp.sin(ang)
    cos_h = jnp.concatenate([cos, cos], axis=-1)
    sin_h = jnp.concatenate([-sin, sin], axis=-1)
    return jnp.tile(cos_h, (1, n_heads)), jnp.tile(sin_h, (1, n_heads))


def _rope(x, cos, sin, dk):
    half = dk // 2
    lane = _iota2((1, LANES), 1)
    up = pltpu.roll(x, LANES - half, 1)
    down = pltpu.roll(x, half, 1)
    partner = jnp.where((lane % dk) < half, up, down)
    return x * cos + partner * sin


def _ret_kernel(q_ref, k_ref, v_ref, cos_ref, sin_ref, o_ref, st_ref, state, *, n_heads, dk, dv, log_gammas):
    c = pl.program_id(1)
    hpb = LANES // dk

    @pl.when(c == 0)
    def _():
        state[...] = jnp.zeros_like(state)

    lane = _iota2((1, LANES), 1)
    t_col = _iota2((CHUNK, 1), 0).astype(F32)
    seg = (_iota2((CHUNK, CHUNK), 0) - _iota2((CHUNK, CHUNK), 1)).astype(F32)
    causal = seg >= 0.0
    row_head = _iota2((LANES, 1), 0) // dk
    for p in range(n_heads // hpb):
        cols = slice(p * LANES, (p + 1) * LANES)
        cos = cos_ref[:, cols]
        sin = sin_ref[:, cols]
        q2 = _rope(q_ref[:, cols], cos, sin, dk)
        k2 = _rope(k_ref[:, cols], cos, sin, dk) * (dk ** -0.5)
        k2b = k2.astype(BF16)
        st2 = state[p]
        st2b = st2.astype(BF16)
        upd = None
        row_decay = None
        for j in range(hpb):
            h = p * hpb + j
            lg = log_gammas[h]
            sel = lane // dk == j
            qh = jnp.where(sel, q2, 0.0).astype(BF16)
            s = _dot_nt(qh, k2b)
            decay = jnp.exp(jnp.where(causal, seg * lg, NEG_INF))
            vh = v_ref[:, h * dv:(h + 1) * dv]
            y = _dot((s * decay).astype(BF16), vh.astype(BF16))
            y = y + jnp.exp((t_col + 1.0) * lg) * _dot(qh, st2b)
            o_ref[:, h * dv:(h + 1) * dv] = y
            tail = jnp.exp((CHUNK - 1.0 - t_col) * lg)
            kh = jnp.where(sel, k2, 0.0).astype(BF16)
            u = _dot_tn(kh, (vh * tail).astype(BF16))
            upd = u if upd is None else upd + u
            rd = jnp.where(row_head == j, math.exp(CHUNK * lg), 0.0)
            row_decay = rd if row_decay is None else row_decay + rd
        state[p] = row_decay * st2 + upd

    @pl.when(c == pl.num_programs(1) - 1)
    def _():
        for h in range(n_heads):
            p, j = divmod(h, hpb)
            st_ref[h] = state[p, j * dk:(j + 1) * dk, :]


def _retention_prompt(qr, kr, vr, cos, sin, n_batch, seq, n_heads, dk, dv):
    nc = seq // CHUNK
    log_gammas = tuple(math.log1p(-2.0 ** (-RET_DECAY_BASE - h)) for h in range(n_heads))
    wq = n_heads * dk
    wv = n_heads * dv
    assert wq % LANES == 0 and LANES % dk == 0 and dv % LANES == 0
    return pl.pallas_call(
        functools.partial(_ret_kernel, n_heads=n_heads, dk=dk, dv=dv, log_gammas=log_gammas),
        out_shape=[jax.ShapeDtypeStruct((n_batch * seq, wv), F32),
                   jax.ShapeDtypeStruct((n_batch, n_heads, dk, dv), F32)],
        grid=(n_batch, nc),
        in_specs=[pl.BlockSpec((CHUNK, wq), lambda b, c: (b * nc + c, 0)),
                  pl.BlockSpec((CHUNK, wq), lambda b, c: (b * nc + c, 0)),
                  pl.BlockSpec((CHUNK, wv), lambda b, c: (b * nc + c, 0)),
                  pl.BlockSpec((CHUNK, wq), lambda b, c: (c, 0)),
                  pl.BlockSpec((CHUNK, wq), lambda b, c: (c, 0))],
        out_specs=[pl.BlockSpec((CHUNK, wv), lambda b, c: (b * nc + c, 0)),
                   pl.BlockSpec((None, n_heads, dk, dv), lambda b, c: (b, 0, 0, 0))],
        scratch_shapes=[pltpu.VMEM((wq // LANES, LANES, dv), F32)],
        compiler_params=_params("arbitrary", "arbitrary"),
        name="retention_prompt_scan",
    )(qr, kr, vr, cos, sin)


def _att_out_kernel(of_ref, or_ref, gt_ref, y_ref, gm_ref, gg_ref, gb_ref, w1_ref, w2_ref, o_ref, *, n_heads, dv):
    parts = []
    for h in range(n_heads):
        r = or_ref[:, h * dv:(h + 1) * dv]
        mu = jnp.mean(r, axis=-1, keepdims=True)
        d = r - mu
        var = jnp.mean(d * d, axis=-1, keepdims=True)
        parts.append(d * lax.rsqrt(var + 1e-5))
    r = jnp.concatenate(parts, axis=1) * gg_ref[...] + gb_ref[...]
    r = r * _silu(gt_ref[...])
    o = _dot(of_ref[...].astype(BF16), w1_ref[...]) + _dot(r.astype(BF16), w2_ref[...])
    o_ref[...] = y_ref[...] + gm_ref[...] * o


def _att_output(rows, o_fox, o_ret, gate, y, mod, layer, gn_g, gn_b, w1, w2, n_heads, dv):
    d = y.shape[1]
    gm_arr, gm_spec = rows.mod(mod, layer, 2, d)
    return pl.pallas_call(
        functools.partial(_att_out_kernel, n_heads=n_heads, dv=dv),
        out_shape=jax.ShapeDtypeStruct(y.shape, F32),
        grid=(rows.n_tiles,),
        in_specs=[rows.spec(o_fox.shape[1]), rows.spec(o_ret.shape[1]), rows.spec(gate.shape[1]), rows.spec(d),
                  gm_spec, _full(gn_g), _full(gn_b), _full(w1), _full(w2)],
        out_specs=rows.spec(d),
        compiler_params=_params("arbitrary"),
        name="attention_output",
    )(o_fox, o_ret, gate, y, gm_arr, gn_g, gn_b, w1, w2)


def _mm_res_kernel(a_ref, y_ref, gm_ref, w_ref, o_ref):
    o_ref[...] = y_ref[...] + gm_ref[...] * _dot(a_ref[...].astype(BF16), w_ref[...])


def _matmul_residual(rows, a, y, mod, layer, w):
    d = y.shape[1]
    gm_arr, gm_spec = rows.mod(mod, layer, 2, d)
    return pl.pallas_call(
        _mm_res_kernel,
        out_shape=jax.ShapeDtypeStruct(y.shape, F32),
        grid=(rows.n_tiles,),
        in_specs=[rows.spec(a.shape[1]), rows.spec(d), gm_spec, _full(w)],
        out_specs=rows.spec(d),
        compiler_params=_params("arbitrary"),
        name="matmul_gated_residual",
    )(a, y, gm_arr, w)


def _ssd_conv_act(taps, cw_ref, cb_ref):
    conv = cb_ref[...] + cw_ref[0:1, :] * taps[0]
    for i in range(1, len(taps)):
        conv = conv + cw_ref[i:i + 1, :] * taps[i]
    return _silu(conv)


def _ssd_gate_norm(y, x, z, dsk, ng, n_groups):
    y = (y + dsk * x) * _silu(z)
    gw = y.shape[1] // n_groups
    parts = []
    for g in range(n_groups):
        blk = y[:, g * gw:(g + 1) * gw]
        ms = jnp.mean(blk * blk, axis=-1, keepdims=True)
        parts.append(blk * lax.rsqrt(ms + 1e-5))
    return jnp.concatenate(parts, axis=1) * ng


def _ssd_kernel(xr_ref, z_ref, dt_ref, cw_ref, cb_ref, dtb_ref, alog_ref, dsk_ref, ng_ref, e_ref,
                y_ref, st_ref, cv_ref, prev, state, *, d_inner, n_groups, d_state, hd):
    c = pl.program_id(1)
    hpg = d_inner // hd // n_groups
    ppg = hpg * hd // LANES

    @pl.when(c == 0)
    def _():
        prev[...] = jnp.zeros_like(prev)
        state[...] = jnp.zeros_like(state)

    cur = xr_ref[...]
    xw = jnp.concatenate([prev[...], cur], axis=0)
    taps = [xw[5 + i:5 + i + CHUNK] for i in range(3)] + [cur]
    tail_rows = cur[CHUNK - 8:CHUNK]
    prev[...] = tail_rows
    cv_ref[...] = tail_rows
    xbc = _ssd_conv_act(taps, cw_ref, cb_ref)
    gn = n_groups * d_state
    x = xbc[:, :d_inner]
    bm = xbc[:, d_inner:d_inner + gn].astype(BF16)
    cm = xbc[:, d_inner + gn:].astype(BF16)

    dt = _softplus(dt_ref[...] + dtb_ref[...])
    la = -dt * jnp.exp(alog_ref[...])
    seg = _iota2((CHUNK, CHUNK), 0) - _iota2((CHUNK, CHUNK), 1)
    causal = seg >= 0
    cum = _dot_sel_lhs(causal.astype(BF16), la)
    cum_t = cum.T
    e = e_ref[...]
    dt_e = _dot_sel_rhs(dt, e)
    cum_e = _dot_sel_rhs(cum, e)
    ecum_e = jnp.exp(cum_e)
    tail_e = jnp.exp(cum_e[CHUNK - 1:CHUNK, :] - cum_e)
    v = x * dt_e
    vb = v.astype(BF16)
    vtb = (v * tail_e).astype(BF16)
    lane = _iota2((1, LANES), 1)
    hpb = LANES // hd
    ys = []
    for g in range(n_groups):
        cmg = cm[:, g * d_state:(g + 1) * d_state]
        bmg = bm[:, g * d_state:(g + 1) * d_state]
        s = _dot_nt(cmg, bmg)
        for pp in range(ppg):
            p = g * ppg + pp
            cols = slice(p * LANES, (p + 1) * LANES)
            v2 = vb[:, cols]
            yp = None
            for j in range(hpb):
                h = p * hpb + j
                d = jnp.exp(jnp.where(causal, cum[:, h:h + 1] - cum_t[h:h + 1, :], NEG_INF))
                yj = _dot((s * d).astype(BF16), v2)
                sel = lane // hd == j
                yp = jnp.where(sel, yj, 0.0) if yp is None else jnp.where(sel, yj, yp)
            st2 = state[p]
            yp = yp + ecum_e[:, cols] * _dot(cmg, st2.astype(BF16))
            ys.append(yp)
            state[p] = ecum_e[CHUNK - 1:CHUNK, cols] * st2 + _dot_tn(bmg, vtb[:, cols])
    y = jnp.concatenate(ys, axis=1)
    y_ref[...] = _ssd_gate_norm(y, x, z_ref[...], dsk_ref[...], ng_ref[...], n_groups)

    @pl.when(c == pl.num_programs(1) - 1)
    def _():
        st_ref[...] = state[...]


def _ssd_prompt(xr, z, dtr, cw, cb, dtb, alog, dsk, ng, e_mat, n_batch, seq, d_inner, n_groups, d_state, hd):
    nc = seq // CHUNK
    cd = xr.shape[1]
    n_blk = d_inner // LANES
    row = lambda b, c: (b * nc + c, 0)
    return pl.pallas_call(
        functools.partial(_ssd_kernel, d_inner=d_inner, n_groups=n_groups, d_state=d_state, hd=hd),
        out_shape=[jax.ShapeDtypeStruct((n_batch * seq, d_inner), F32),
                   jax.ShapeDtypeStruct((n_batch, n_blk, d_state, LANES), F32),
                   jax.ShapeDtypeStruct((n_batch, 8, cd), F32)],
        grid=(n_batch, nc),
        in_specs=[pl.BlockSpec((CHUNK, cd), row), pl.BlockSpec((CHUNK, d_inner), row),
                  pl.BlockSpec((CHUNK, LANES), row),
                  _full(cw), _full(cb), _full(dtb), _full(alog), _full(dsk), _full(ng), _full(e_mat)],
        out_specs=[pl.BlockSpec((CHUNK, d_inner), row),
                   pl.BlockSpec((None, n_blk, d_state, LANES), lambda b, c: (b, 0, 0, 0)),
                   pl.BlockSpec((None, 8, cd), lambda b, c: (b, 0, 0))],
        scratch_shapes=[pltpu.VMEM((8, cd), F32), pltpu.VMEM((n_blk, d_state, LANES), F32)],
        compiler_params=_params("arbitrary", "arbitrary"),
        name="ssd_prompt_scan",
    )(xr, z, dtr, cw, cb, dtb, alog, dsk, ng, e_mat)


def _ssd_step_prep_kernel(xr_ref, c0_ref, c1_ref, c2_ref, dt_ref, cw_ref, cb_ref, dtb_ref, alog_ref, e_ref,
                          x_ref, bm_ref, cm_ref, v_ref, a_ref, *, d_inner, gn):
    xbc = _ssd_conv_act([c0_ref[...], c1_ref[...], c2_ref[...], xr_ref[...]], cw_ref, cb_ref)
    x = xbc[:, :d_inner]
    x_ref[...] = x
    bm_ref[...] = xbc[:, d_inner:d_inner + gn]
    cm_ref[...] = xbc[:, d_inner + gn:]
    dt = _softplus(dt_ref[...] + dtb_ref[...])
    a_ref[...] = jnp.exp(-dt * jnp.exp(alog_ref[...]))
    v_ref[...] = x * _dot_sel_rhs(dt, e_ref[...])


def _ssd_step_prep(xr, taps, dtr, cw, cb, dtb, alog, e_mat, d_inner, gn):
    n = xr.shape[0]
    args = (xr, *taps, dtr, cw, cb, dtb, alog, e_mat)
    return pl.pallas_call(
        functools.partial(_ssd_step_prep_kernel, d_inner=d_inner, gn=gn),
        out_shape=[jax.ShapeDtypeStruct((n, d_inner), F32), jax.ShapeDtypeStruct((n, gn), F32),
                   jax.ShapeDtypeStruct((n, gn), F32), jax.ShapeDtypeStruct((n, d_inner), F32),
                   jax.ShapeDtypeStruct((n, LANES), F32)],
        grid=(1,),
        in_specs=[_full(a) for a in args],
        out_specs=[pl.BlockSpec((n, d_inner), lambda i: (0, 0)), pl.BlockSpec((n, gn), lambda i: (0, 0)),
                   pl.BlockSpec((n, gn), lambda i: (0, 0)), pl.BlockSpec((n, d_inner), lambda i: (0, 0)),
                   pl.BlockSpec((n, LANES), lambda i: (0, 0))],
        compiler_params=_params("arbitrary"),
        name="ssd_step_prep",
    )(*args)


def _ssd_step_post_kernel(y_ref, x_ref, z_ref, dsk_ref, ng_ref, o_ref, *, n_groups):
    o_ref[...] = _ssd_gate_norm(y_ref[...], x_ref[...], z_ref[...], dsk_ref[...], ng_ref[...], n_groups)


def _ssd_step_post(y, x, z, dsk, ng, n_groups):
    args = (y, x, z, dsk, ng)
    return pl.pallas_call(
        functools.partial(_ssd_step_post_kernel, n_groups=n_groups),
        out_shape=jax.ShapeDtypeStruct(y.shape, F32),
        grid=(1,),
        in_specs=[_full(a) for a in args],
        out_specs=pl.BlockSpec(y.shape, lambda i: (0, 0)),
        compiler_params=_params("arbitrary"),
        name="ssd_step_post",
    )(*args)


def _state_step_kernel(s_ref, a_ref, k_ref, q_ref, v_ref, so_ref, y_ref, *, n_heads, heads_per_key):
    for h in range(n_heads):
        g = h // heads_per_key
        new = a_ref[h] * s_ref[h] + k_ref[g] * v_ref[h]
        so_ref[h] = new
        y_ref[h] = jnp.sum(q_ref[g] * new, axis=0, keepdims=True)


def _state_step(state, a, k_col, q_col, v_row):
    n_b, n_h, n_n, n_v = state.shape
    n_g = k_col.shape[1]
    blk = lambda arr: pl.BlockSpec((None,) + arr.shape[1:], lambda b: (b, 0, 0, 0))
    return pl.pallas_call(
        functools.partial(_state_step_kernel, n_heads=n_h, heads_per_key=n_h // n_g),
        out_shape=[jax.ShapeDtypeStruct(state.shape, F32), jax.ShapeDtypeStruct((n_b, n_h, 1, n_v), F32)],
        grid=(n_b,),
        in_specs=[blk(state), blk(a), blk(k_col), blk(q_col), blk(v_row)],
        out_specs=[blk(state), pl.BlockSpec((None, n_h, 1, n_v), lambda b: (b, 0, 0, 0))],
        compiler_params=_params("arbitrary"),
        name="state_step",
    )(state, a, k_col, q_col, v_row)


def _rope_rows_kernel(q_ref, k_ref, cos_ref, sin_ref, qo_ref, ko_ref, *, dk):
    for p in range(q_ref.shape[1] // LANES):
        cols = slice(p * LANES, (p + 1) * LANES)
        qo_ref[:, cols] = _rope(q_ref[:, cols], cos_ref[:, cols], sin_ref[:, cols], dk)
        ko_ref[:, cols] = _rope(k_ref[:, cols], cos_ref[:, cols], sin_ref[:, cols], dk) * (dk ** -0.5)


def _rope_rows(q, k, cos, sin, dk):
    args = (q, k, cos, sin)
    return pl.pallas_call(
        functools.partial(_rope_rows_kernel, dk=dk),
        out_shape=[jax.ShapeDtypeStruct(q.shape, F32)] * 2,
        grid=(1,),
        in_specs=[_full(a) for a in args],
        out_specs=[pl.BlockSpec(q.shape, lambda i: (0, 0))] * 2,
        compiler_params=_params("arbitrary"),
        name="rope_rows",
    )(*args)


def _fox_decode_kernel(pt_ref, q_ref, kn_ref, vn_ref, lfn_ref, kt_hbm, vt_hbm, lf_hbm, o_ref,
                       kbuf, vbuf, lbuf, sem, qb_ref, acc_ref, *, layer, n_pages, group, n_heads, dh, page):
    b = pl.program_id(0)
    nb = pl.num_programs(0)
    n_groups = n_pages // group
    hd = n_heads * dh

    def copies(bb, gi, slot):
        out = []
        for g in range(group):
            pid = pt_ref[bb, gi * group + g]
            out.append(pltpu.make_async_copy(kt_hbm.at[layer, pid], kbuf.at[slot, g], sem.at[slot, 0]))
            out.append(pltpu.make_async_copy(vt_hbm.at[layer, pid], vbuf.at[slot, g], sem.at[slot, 1]))
            out.append(pltpu.make_async_copy(lf_hbm.at[layer, pid], lbuf.at[slot, g], sem.at[slot, 2]))
        return out

    def start(bb, gi, slot):
        for c in copies(bb, gi, slot):
            c.start()

    def wait(bb, gi, slot):
        for c in copies(bb, gi, slot):
            c.wait()

    @pl.when(b == 0)
    def _():
        start(b, 0, 0)

    q_row = q_ref[...] * (dh ** -0.5)
    qb_ref[...] = jnp.broadcast_to(q_row, (page, hd)).T.reshape(n_heads, dh, page)
    acc_ref[...] = jnp.zeros_like(acc_ref)
    tri = (_iota2((page, page), 0) <= _iota2((page, page), 1)).astype(BF16)

    def page_update(slot, g, carry):
        m, l, run = carry
        s = jnp.concatenate(
            [jnp.sum(kbuf[slot, g, h] * qb_ref[h], axis=0, keepdims=True) for h in range(n_heads)], axis=0)
        fcum = _dot_sel_rhs(lbuf[slot, g], tri) + run
        s = s - fcum
        m_new = jnp.maximum(m, jnp.max(s, axis=-1, keepdims=True))
        alpha = jnp.exp(m - m_new)
        pe = jnp.exp(s - m_new)
        l_new = alpha * l + jnp.sum(pe, axis=-1, keepdims=True)
        for h in range(n_heads):
            acc_ref[h] = alpha[h:h + 1, :] * acc_ref[h] + pe[h:h + 1, :] * vbuf[slot, g, h]
        return m_new, l_new, fcum[:, page - 1:page]

    def pair(gp, carry):
        gi = 2 * gp
        wait(b, gi, 0)
        start(b, gi + 1, 1)
        for g in range(group):
            carry = page_update(0, g, carry)
        wait(b, gi + 1, 1)

        @pl.when(gi + 2 < n_groups)
        def _():
            start(b, gi + 2, 0)

        @pl.when(jnp.logical_and(gi + 2 >= n_groups, b + 1 < nb))
        def _():
            start(b + 1, 0, 0)

        for g in range(group):
            carry = page_update(1, g, carry)
        return carry

    init = (jnp.full((n_heads, 1), NEG_INF, F32), jnp.zeros((n_heads, 1), F32), jnp.zeros((n_heads, 1), F32))
    m, l, f_tot = lax.fori_loop(0, n_groups // 2, pair, init)

    own = (_iota2((n_heads, hd), 1) // dh) == _iota2((n_heads, hd), 0)
    spread = lambda col: jnp.sum(jnp.where(own, col, 0.0), axis=0, keepdims=True)
    lfn = jnp.sum(jnp.where(_iota2((n_heads, LANES), 1) == _iota2((n_heads, LANES), 0), lfn_ref[...], 0.0),
                  axis=1, keepdims=True)
    s_new = jnp.sum(jnp.where(own, q_row * kn_ref[...], 0.0), axis=1, keepdims=True) - (f_tot + lfn)
    m_fin = jnp.maximum(m, s_new)
    a2 = jnp.exp(m - m_fin)
    p_new = jnp.exp(s_new - m_fin)
    l_fin = a2 * l + p_new
    o_past = jnp.sum(acc_ref[...].reshape(hd, page).T, axis=0, keepdims=True)
    o_ref[...] = (spread(a2) * o_past + spread(p_new) * vn_ref[...]) / spread(l_fin)


def _fox_decode(page_table, q, k_new, v_new, lf_new, kt_pages, vt_pages, lf_pages_t, layer):
    n_b, _, hd = q.shape
    n_pages = page_table.shape[1]
    _, _, n_h, dh, page = kt_pages.shape
    group = max(1, min(4, n_pages // 2))
    assert n_pages % (2 * group) == 0
    tok = lambda arr: pl.BlockSpec((None, 1, arr.shape[2]), lambda b, pt: (b, 0, 0))
    grid_spec = pltpu.PrefetchScalarGridSpec(
        num_scalar_prefetch=1,
        grid=(n_b,),
        in_specs=[tok(q), tok(k_new), tok(v_new), tok(lf_new),
                  pl.BlockSpec(memory_space=pl.ANY), pl.BlockSpec(memory_space=pl.ANY),
                  pl.BlockSpec(memory_space=pl.ANY)],
        out_specs=pl.BlockSpec((None, 1, hd), lambda b, pt: (b, 0, 0)),
        scratch_shapes=[pltpu.VMEM((2, group, n_h, dh, page), F32), pltpu.VMEM((2, group, n_h, dh, page), F32),
                        pltpu.VMEM((2, group, n_h, page), F32), pltpu.SemaphoreType.DMA((2, 3)),
                        pltpu.VMEM((n_h, dh, page), F32), pltpu.VMEM((n_h, dh, page), F32)],
    )
    return pl.pallas_call(
        functools.partial(_fox_decode_kernel, layer=layer, n_pages=n_pages, group=group,
                          n_heads=n_h, dh=dh, page=page),
        out_shape=jax.ShapeDtypeStruct((n_b, 1, hd), F32),
        grid_spec=grid_spec,
        compiler_params=_params("arbitrary"),
        name="fox_paged_decode",
    )(page_table, q, k_new, v_new, lf_new, kt_pages, vt_pages, lf_pages_t)


def _router_kernel(y_ref, g_ref, sh_ref, sc_ref, rwt_ref, rb_ref, cin_ref,
                   h_ref, idx_ref, w_ref, rk_ref, cnt_ref, carry, *, n_exp):
    i = pl.program_id(0)

    @pl.when(i == 0)
    def _():
        carry[...] = cin_ref[...]

    h = _modulate(y_ref[...], g_ref[...], sh_ref[...], sc_ref[...])
    h_ref[...] = h
    tm = h.shape[0]
    logits = _dot3(rwt_ref[...], h, dot=_dot_nt) + rb_ref[...]
    eio = _iota2((n_exp, tm), 0)
    vals, idxs = [], []
    rest = logits
    for _ in range(TOP_K):
        m = jnp.max(rest, axis=0, keepdims=True)
        ik = jnp.min(jnp.where(rest == m, eio, n_exp), axis=0, keepdims=True)
        vals.append(m)
        idxs.append(ik)
        rest = jnp.where(eio == ik, NEG_INF, rest)
    ex = [jnp.exp(v - vals[0]) for v in vals]
    den = ex[0]
    for e in ex[1:]:
        den = den + e
    sel = jnp.zeros((n_exp, tm), F32)
    for ik in idxs:
        sel = sel + jnp.where(eio == ik, 1.0, 0.0)
    before = (_iota2((tm, tm), 0) < _iota2((tm, tm), 1)).astype(BF16)
    rank_all = _dot(sel.astype(BF16), before) + carry[:, 0:1]
    ranks = [jnp.sum(jnp.where(eio == ik, rank_all, 0.0), axis=0, keepdims=True) for ik in idxs]
    carry[...] = carry[...] + jnp.sum(sel, axis=1, keepdims=True)
    idx_ref[...] = jnp.concatenate(idxs, axis=0)
    w_ref[...] = jnp.concatenate([e / den for e in ex], axis=0)
    rk_ref[...] = jnp.concatenate(ranks, axis=0).astype(I32)
    cnt_ref[...] = carry[...]


def _route(rows, y, g, mod, layer, rw_t, rb_col, cnt_in):
    d = y.shape[1]
    n_exp = rw_t.shape[0]
    sh_arr, sh_spec = rows.mod(mod, layer, 3, d)
    sc_arr, sc_spec = rows.mod(mod, layer, 4, d)
    kt = pl.BlockSpec((TOP_K, rows.tile), lambda i: (0, i))
    return pl.pallas_call(
        functools.partial(_router_kernel, n_exp=n_exp),
        out_shape=[jax.ShapeDtypeStruct((rows.rows, d), F32),
                   jax.ShapeDtypeStruct((TOP_K, rows.rows), I32),
                   jax.ShapeDtypeStruct((TOP_K, rows.rows), F32),
                   jax.ShapeDtypeStruct((TOP_K, rows.rows), I32),
                   jax.ShapeDtypeStruct((n_exp, LANES), F32)],
        grid=(rows.n_tiles,),
        in_specs=[rows.spec(d), _full(g), sh_spec, sc_spec, _full(rw_t), _full(rb_col), _full(cnt_in)],
        out_specs=[rows.spec(d), kt, kt, kt, pl.BlockSpec((n_exp, LANES), lambda i: (0, 0))],
        scratch_shapes=[pltpu.VMEM((n_exp, LANES), F32)],
        compiler_params=_params("arbitrary"),
        name="moe_router",
    )(y, g, sh_arr, sc_arr, rw_t, rb_col, cnt_in)


def _dispatch_kernel(slot_ref, h_ref, xs_in_ref, xs_ref, sem, *, tm, n_rows):
    del xs_in_ref
    i = pl.program_id(0)

    def issue(t, carry):
        for k in range(TOP_K):
            s = slot_ref[k * n_rows + i * tm + t]
            pltpu.make_async_copy(h_ref.at[pl.ds(t, 1), :], xs_ref.at[pl.ds(s, 1), :], sem).start(priority=k % 2)
        return carry

    lax.fori_loop(0, tm, issue, 0, unroll=2)
    for k in range(TOP_K):
        pltpu.make_async_copy(h_ref, xs_ref.at[pl.ds(0, tm), :], sem).wait()


def _dispatch(rows, slots_flat, h, xs):
    d = h.shape[1]
    grid_spec = pltpu.PrefetchScalarGridSpec(
        num_scalar_prefetch=1,
        grid=(rows.n_tiles,),
        in_specs=[pl.BlockSpec((rows.tile, d), lambda i, s: (i, 0)), pl.BlockSpec(memory_space=pl.ANY)],
        out_specs=pl.BlockSpec(memory_space=pl.ANY),
        scratch_shapes=[pltpu.SemaphoreType.DMA(())],
    )
    return pl.pallas_call(
        functools.partial(_dispatch_kernel, tm=rows.tile, n_rows=rows.rows),
        out_shape=jax.ShapeDtypeStruct(xs.shape, F32),
        grid_spec=grid_spec,
        input_output_aliases={2: 0},
        compiler_params=_params("arbitrary"),
        name="moe_dispatch",
    )(slots_flat, h, xs)


def _split_gate_up_kernel(w_ref, wg_ref, wu_ref, wt_ref, *, chunk):
    half = chunk // 2
    for c in range(w_ref.shape[1] // chunk):
        for r in range(w_ref.shape[0] // LANES):
            rows = slice(r * LANES, (r + 1) * LANES)
            wt_ref[...] = w_ref[rows, c * chunk:(c + 1) * chunk].T
            wg_ref[rows, c * half:(c + 1) * half] = wt_ref[pl.ds(0, half, stride=2), :].T.astype(BF16)
            wu_ref[rows, c * half:(c + 1) * half] = wt_ref[pl.ds(1, half, stride=2), :].T.astype(BF16)


def _split_gate_up(w_up):
    n_l, n_e, d, f2 = w_up.shape
    chunk = min(512, f2)
    blk = lambda width: pl.BlockSpec((None, None, d, width), lambda l, e: (l, e, 0, 0))
    return pl.pallas_call(
        functools.partial(_split_gate_up_kernel, chunk=chunk),
        out_shape=[jax.ShapeDtypeStruct((n_l, n_e, d, f2 // 2), BF16)] * 2,
        grid=(n_l, n_e),
        in_specs=[blk(f2)],
        out_specs=[blk(f2 // 2)] * 2,
        scratch_shapes=[pltpu.VMEM((chunk, LANES), F32)],
        compiler_params=_params("arbitrary", "arbitrary"),
        name="moe_split_gate_up",
    )(w_up)


def _ffn_kernel(te_ref, tv_ref, xs_ref, wg_ref, wu_ref, bg_ref, bu_ref, wd_ref, bd_ref, ys_ref):
    del te_ref
    i = pl.program_id(0)

    @pl.when(tv_ref[i] == 1)
    def _():
        x = xs_ref[...].astype(BF16)
        gate = jnp.minimum(_dot(x, wg_ref[...]) + bg_ref[...], SWIGLU_LIMIT)
        up = jnp.clip(_dot(x, wu_ref[...]) + bu_ref[...], -SWIGLU_LIMIT, SWIGLU_LIMIT)
        act = (up + 1.0) * gate * _sigmoid(SWIGLU_ALPHA * gate)
        ys_ref[...] = _dot(act.astype(BF16), wd_ref[...].astype(BF16)) + bd_ref[...]

    @pl.when(tv_ref[i] == 0)
    def _():
        ys_ref[...] = jnp.zeros_like(ys_ref)


def _expert_ffn(tile_expert, tile_valid, xs, layer, wg, wu, bg, bu, wd, bd, tmf):
    n_slots, d = xs.shape
    f = wg.shape[3]
    ex = lambda i, te, tv: (te[i], 0, 0)
    lex = lambda i, te, tv: (layer, te[i], 0, 0)
    grid_spec = pltpu.PrefetchScalarGridSpec(
        num_scalar_prefetch=2,
        grid=(n_slots // tmf,),
        in_specs=[pl.BlockSpec((tmf, d), lambda i, te, tv: (i, 0)),
                  pl.BlockSpec((None, None, d, f), lex), pl.BlockSpec((None, None, d, f), lex),
                  pl.BlockSpec((None, 1, f), ex), pl.BlockSpec((None, 1, f), ex),
                  pl.BlockSpec((None, None, f, d), lex), pl.BlockSpec((None, 1, d), ex)],
        out_specs=pl.BlockSpec((tmf, d), lambda i, te, tv: (i, 0)),
    )
    return pl.pallas_call(
        _ffn_kernel,
        out_shape=jax.ShapeDtypeStruct((n_slots, d), F32),
        grid_spec=grid_spec,
        compiler_params=_params("arbitrary"),
        name="moe_expert_ffn",
    )(tile_expert, tile_valid, xs, wg, wu, bg, bu, wd, bd)


def _combine_kernel(slot_ref, ys_ref, w_ref, y_ref, gm_ref, o_ref, buf, sem, *, tm, n_rows):
    i = pl.program_id(0)

    def issue(t, carry):
        for k in range(TOP_K):
            s = slot_ref[k * n_rows + i * tm + t]
            pltpu.make_async_copy(ys_ref.at[pl.ds(s, 1), :], buf.at[k, pl.ds(t, 1), :], sem).start(priority=k % 2)
        return carry

    lax.fori_loop(0, tm, issue, 0, unroll=2)
    for k in range(TOP_K):
        pltpu.make_async_copy(ys_ref.at[pl.ds(0, tm), :], buf.at[k], sem).wait()
    acc = w_ref[:, 0:1] * buf[0]
    for k in range(1, TOP_K):
        acc = acc + w_ref[:, k:k + 1] * buf[k]
    o_ref[...] = y_ref[...] + gm_ref[...] * acc


def _combine(rows, slots_flat, ys, w_tok, y, mod, layer):
    d = y.shape[1]
    gm_arr, gm_spec = rows.mod(mod, layer, 5, d)
    grid_spec = pltpu.PrefetchScalarGridSpec(
        num_scalar_prefetch=1,
        grid=(rows.n_tiles,),
        in_specs=[pl.BlockSpec(memory_space=pl.ANY),
                  pl.BlockSpec((rows.tile, TOP_K), lambda i, s: (i, 0)),
                  pl.BlockSpec((rows.tile, d), lambda i, s: (i, 0)), gm_spec],
        out_specs=pl.BlockSpec((rows.tile, d), lambda i, s: (i, 0)),
        scratch_shapes=[pltpu.VMEM((TOP_K, rows.tile, d), F32), pltpu.SemaphoreType.DMA(())],
    )
    return pl.pallas_call(
        functools.partial(_combine_kernel, tm=rows.tile, n_rows=rows.rows),
        out_shape=jax.ShapeDtypeStruct(y.shape, F32),
        grid_spec=grid_spec,
        compiler_params=_params("arbitrary"),
        name="moe_combine",
    )(slots_flat, ys, w_tok, y, gm_arr)


def _rmsnorm_kernel(x_ref, g_ref, o_ref):
    x = x_ref[...]
    o_ref[...] = x * lax.rsqrt(jnp.mean(x * x, axis=-1, keepdims=True) + 1e-6) * g_ref[...]


def _final_norm(rows, x, g):
    d = x.shape[1]
    return pl.pallas_call(
        _rmsnorm_kernel,
        out_shape=jax.ShapeDtypeStruct(x.shape, F32),
        grid=(rows.n_tiles,),
        in_specs=[rows.spec(d), _full(g)],
        out_specs=rows.spec(d),
        compiler_params=_params("arbitrary"),
        name="final_rmsnorm",
    )(x, g)


def _moe_layer(groups, ys, layer, g_ffn, router_w, router_b, wg_all, wu_all, exp_b_up, wd_all, exp_b_down, tmf):
    n_exp = router_w.shape[1]
    d = router_w.shape[0]
    rw_t = router_w.T
    rb_col = router_b.reshape(n_exp, 1)
    cnt = jnp.zeros((n_exp, LANES), F32)
    routed = []
    for (rows, m), y in zip(groups, ys):
        h, idx, w, rank, cnt = _route(rows, y, g_ffn, m, layer, rw_t, rb_col, cnt)
        routed.append((h, idx, w, rank))
    total = sum(rows.rows for rows, _ in groups) * TOP_K
    n_tiles = -(-total // tmf) + n_exp
    counts = cnt[:, 0].astype(I32)
    tiles_e = (counts + tmf - 1) // tmf
    tile_end = jnp.cumsum(tiles_e)
    starts = (tile_end - tiles_e) * tmf
    tile_ids = jnp.arange(n_tiles, dtype=I32)
    tile_expert = jnp.minimum(jnp.sum((tile_end[None, :] <= tile_ids[:, None]).astype(I32), axis=1), n_exp - 1)
    tile_valid = (tile_ids < tile_end[-1]).astype(I32)
    xs = jnp.zeros((n_tiles * tmf, d), F32)
    slots = []
    e_ids = jnp.arange(n_exp, dtype=I32)
    for (rows, _), (h, idx, w, rank) in zip(groups, routed):
        sl = (jnp.sum(jnp.where(idx[..., None] == e_ids, starts, 0), axis=-1) + rank).reshape(-1)
        slots.append(sl)
        xs = _dispatch(rows, sl, h, xs)
    bg = exp_b_up[:, None, 0::2]
    bu = exp_b_up[:, None, 1::2]
    y_sorted = _expert_ffn(tile_expert, tile_valid, xs, layer, wg_all, wu_all, bg, bu,
                           wd_all, exp_b_down[:, None, :], tmf)
    outs = []
    for (rows, m), y, sl, (h, idx, w, rank) in zip(groups, ys, slots, routed):
        outs.append(_combine(rows, sl, y_sorted, w.T, y, m, layer))
    return outs


def kernel(x_prompt, x_sample, c_prompt, c_sample, cache_k, cache_v, cache_logf, page_table, state_ret, state_ssm, state_conv, ada_w, ada_b, norm_mix_g, norm_ffn_g, norm_final_g, att_w_in, att_b_f, ret_gn_g, ret_gn_b, att_w_out, ssm_w_in, ssm_conv_w, ssm_conv_b, ssm_dt_bias, ssm_a_log, ssm_d, ssm_norm_g, ssm_w_out, router_w, router_b, exp_w_up, exp_b_up, exp_w_down, exp_b_down):
    bp, sp, d = x_prompt.shape
    bs = x_sample.shape[0]
    depth = ada_w.shape[0]
    n_pages, page = page_table.shape[1], cache_k.shape[2]
    past = n_pages * page
    h_a, dh_a = cache_k.shape[3], cache_k.shape[4]
    h_b, dk_b, dv_b = state_ret.shape[2:]
    h_c, d_state, hd_c = state_ssm.shape[2:]
    d_inner = h_c * hd_c
    conv_w_len, conv_dim = ssm_conv_w.shape[1:]
    gn = (conv_dim - d_inner) // 2
    n_groups = gn // d_state
    assert conv_w_len == 4 and hd_c * 2 == LANES and h_a <= LANES and h_c <= LANES

    rows_p = _Rows(bp, sp, 256)
    rows_s = _Rows(bs, 1, bs)
    yp = x_prompt.reshape(bp * sp, d)
    ys = x_sample.reshape(bs, d)

    mod = _modulation_all(jnp.concatenate([c_prompt, c_sample], axis=0), ada_w, ada_b)
    mod_p, mod_s = mod[:, :bp], mod[:, bp:]

    wa, wr, wvr = h_a * dh_a, h_b * dk_b, h_b * dv_b
    att_segs = []
    off = 0
    for wdt in (wa, wa, wa, wr, wr, wvr, wvr, LANES):
        att_segs.append((off, wdt))
        off += wdt
    ssm_segs = ((0, d_inner), (d_inner, conv_dim), (d_inner + conv_dim, LANES))

    cos_p, sin_p = _rope_tables(jnp.arange(sp), h_b, dk_b)
    cos_s, sin_s = _rope_tables(jnp.full((1,), past), h_b, dk_b)
    log_gammas = [math.log1p(-2.0 ** (-RET_DECAY_BASE - h)) for h in range(h_b)]
    gamma_col = jnp.broadcast_to(jnp.asarray(np.exp(log_gammas), F32).reshape(1, h_b, 1, 1), (bs, h_b, 1, 1))
    kt_pages = jnp.transpose(cache_k, (0, 1, 3, 4, 2))
    vt_pages = jnp.transpose(cache_v, (0, 1, 3, 4, 2))
    lf_pages_t = jnp.swapaxes(cache_logf, 2, 3)
    n_exp = router_w.shape[2]
    tmf = 512 if (bp * sp + bs) * TOP_K >= 512 * n_exp else 64
    e_mat = (jnp.arange(LANES)[:, None] == (jnp.arange(d_inner) // hd_c)[None, :]).astype(BF16)
    wg_all, wu_all = _split_gate_up(exp_w_up)

    k_p, v_p, f_p, k_s, v_s, f_s, ret_p, ret_s = [], [], [], [], [], [], [], []
    ssm_p, ssm_s, conv_p, conv_s = [], [], [], []
    for l in range(depth):
        j = l // 2
        g_mix = norm_mix_g[l][None]
        if l % 2 == 0:
            w = att_w_in[j]
            o0 = 3 * wa
            w_perm = jnp.concatenate(
                [w[:, :o0], w[:, o0 + h_a:], w[:, o0:o0 + h_a], jnp.zeros((d, LANES - h_a), F32)], axis=1).astype(BF16)
            b_f_pad = jnp.pad(att_b_f[j], (0, LANES - h_a))[None]
            w1 = att_w_out[j][:wa].astype(BF16)
            w2 = att_w_out[j][wa:].astype(BF16)
            gn_g, gn_b = ret_gn_g[j][None], ret_gn_b[j][None]
            qa, ka, va, qr, kr, vr, gate, fa = _project(rows_p, yp, g_mix, mod_p, l, w_perm, att_segs)
            logf, fcum = _forget_gates(rows_p, fa, b_f_pad)
            hp8 = -(-h_a // 8) * 8
            fk_t = jnp.transpose(fcum.reshape(bp, sp, LANES)[:, :, :hp8], (0, 2, 1))
            o_fox = _fox_prompt(qa, ka, va, fcum, fk_t, bp, sp, dh_a, h_a)
            o_ret, st = _retention_prompt(qr, kr, vr, cos_p, sin_p, bp, sp, h_b, dk_b, dv_b)
            yp = _att_output(rows_p, o_fox, o_ret, gate, yp, mod_p, l, gn_g, gn_b, w1, w2, h_b, dv_b)
            k_p.append(ka.reshape(bp, sp, h_a, dh_a))
            v_p.append(va.reshape(bp, sp, h_a, dh_a))
            f_p.append(logf[:, :h_a].reshape(bp, sp, h_a))
            ret_p.append(st)
            qa, ka, va, qr, kr, vr, gate, fa = _project(rows_s, ys, g_mix, mod_s, l, w_perm, att_segs)
            logf, _ = _forget_gates(rows_s, fa, b_f_pad)
            o_fox = _fox_decode(page_table, qa[:, None, :], ka[:, None, :], va[:, None, :], logf[:, None, :],
                                kt_pages, vt_pages, lf_pages_t, j).reshape(bs, wa)
            qr2, kr2 = _rope_rows(qr, kr, cos_s, sin_s, dk_b)
            st, y_ret = _state_step(state_ret[j], gamma_col, kr2.reshape(bs, h_b, dk_b, 1),
                                    qr2.reshape(bs, h_b, dk_b, 1), vr.reshape(bs, h_b, 1, dv_b))
            ys = _att_output(rows_s, o_fox, y_ret.reshape(bs, wvr), gate, ys, mod_s, l, gn_g, gn_b, w1, w2, h_b, dv_b)
            k_s.append(ka.reshape(bs, 1, h_a, dh_a))
            v_s.append(va.reshape(bs, 1, h_a, dh_a))
            f_s.append(logf[:, :h_a].reshape(bs, 1, h_a))
            ret_s.append(st)
        else:
            w_pad = jnp.pad(ssm_w_in[j], ((0, 0), (0, LANES - h_c))).astype(BF16)
            cw, cb = ssm_conv_w[j], ssm_conv_b[j][None]
            dtb = jnp.pad(ssm_dt_bias[j], (0, LANES - h_c))[None]
            alog = jnp.pad(ssm_a_log[j], (0, LANES - h_c))[None]
            dsk = jnp.repeat(ssm_d[j], hd_c)[None]
            ng = ssm_norm_g[j][None]
            w_out = ssm_w_out[j].astype(BF16)
            z, xr, dtr = _project(rows_p, yp, g_mix, mod_p, l, w_pad, ssm_segs)
            y_n, st2, cv = _ssd_prompt(xr, z, dtr, cw, cb, dtb, alog, dsk, ng, e_mat, bp, sp,
                                       d_inner, n_groups, d_state, hd_c)
            yp = _matmul_residual(rows_p, y_n, yp, mod_p, l, w_out)
            st = st2.reshape(bp, h_c // 2, d_state, 2, hd_c).transpose(0, 1, 3, 2, 4).reshape(bp, h_c, d_state, hd_c)
            ssm_p.append(st)
            conv_p.append(cv[:, 8 - (conv_w_len - 1):, :])
            z, xr, dtr = _project(rows_s, ys, g_mix, mod_s, l, w_pad, ssm_segs)
            taps = [state_conv[j][:, i, :] for i in range(conv_w_len - 1)]
            x, bm, cm, v, a = _ssd_step_prep(xr, taps, dtr, cw, cb, dtb, alog, e_mat, d_inner, gn)
            st, y_s = _state_step(state_ssm[j], a[:, :h_c].reshape(bs, h_c, 1, 1),
                                  bm.reshape(bs, n_groups, d_state, 1), cm.reshape(bs, n_groups, d_state, 1),
                                  v.reshape(bs, h_c, 1, hd_c))
            y_n = _ssd_step_post(y_s.reshape(bs, d_inner), x, z, dsk, ng, n_groups)
            ys = _matmul_residual(rows_s, y_n, ys, mod_s, l, w_out)
            ssm_s.append(st)
            conv_s.append(jnp.concatenate([state_conv[j][:, 1:, :], xr[:, None, :]], axis=1))
        yp, ys = _moe_layer([(rows_p, mod_p), (rows_s, mod_s)], [yp, ys], l, norm_ffn_g[l][None],
                            router_w[l], router_b[l], wg_all, wu_all, exp_b_up[l], exp_w_down, exp_b_down[l], tmf)
    y_prompt = _final_norm(rows_p, yp, norm_final_g[None]).reshape(bp, sp, d)
    y_sample = _final_norm(rows_s, ys, norm_final_g[None]).reshape(bs, 1, d)
    return (y_prompt, y_sample,
            jnp.stack(k_p), jnp.stack(v_p), jnp.stack(f_p),
            jnp.stack(k_s), jnp.stack(v_s), jnp.stack(f_s),
            jnp.stack(ret_p), jnp.stack(ret_s),
            jnp.stack(ssm_p), jnp.stack(ssm_s),
            jnp.stack(conv_p), jnp.stack(conv_s))
```

```python
import functools
import math

import jax
import jax.numpy as jnp
import numpy as np
from jax import lax
from jax.experimental import pallas as pl
from jax.experimental.pallas import tpu as pltpu

F32 = jnp.float32
BF16 = jnp.bfloat16
I32 = jnp.int32

LANES = 128
VMEM_LIMIT = 56 * 1024 * 1024
CHUNK = 128
TOP_K = 4
RET_DECAY_BASE = 5.0
ROPE_THETA = 10000.0
SWIGLU_LIMIT = 7.0
SWIGLU_ALPHA = 1.702
NEG_INF = float("-inf")


def _params(*sem):
    return pltpu.CompilerParams(dimension_semantics=sem, vmem_limit_bytes=VMEM_LIMIT)


def _dot(a, b):
    return jnp.dot(a, b, preferred_element_type=F32)


def _dot_nt(a, b):
    return lax.dot_general(a, b, (((1,), (1,)), ((), ())), preferred_element_type=F32)


def _dot_tn(a, b):
    return lax.dot_general(a, b, (((0,), (0,)), ((), ())), preferred_element_type=F32)


def _split3(x):
    hi = x.astype(BF16)
    r = x - hi.astype(F32)
    mid = r.astype(BF16)
    lo = (r - mid.astype(F32)).astype(BF16)
    return hi, mid, lo


def _dot_sel_rhs(x, m):
    hi, mid, lo = _split3(x)
    return _dot(hi, m) + _dot(mid, m) + _dot(lo, m)


def _dot_sel_lhs(m, x):
    hi, mid, lo = _split3(x)
    return _dot(m, hi) + _dot(m, mid) + _dot(m, lo)


def _dot3(a, b, dot=_dot):
    a_hi = a.astype(BF16)
    a_lo = (a - a_hi.astype(F32)).astype(BF16)
    b_hi = b.astype(BF16)
    b_lo = (b - b_hi.astype(F32)).astype(BF16)
    return dot(a_hi, b_hi) + dot(a_hi, b_lo) + dot(a_lo, b_hi)


def _dot_w(a, w_ref):
    w = w_ref[...]
    return _dot3(a, w) if w.dtype == F32 else _dot(a.astype(BF16), w)


def _sigmoid(x):
    return 1.0 / (1.0 + jnp.exp(-x))


def _silu(x):
    return x * _sigmoid(x)


def _softplus(x):
    return jnp.maximum(x, 0.0) + jnp.log(1.0 + jnp.exp(-jnp.abs(x)))


def _log_sigmoid(x):
    return -_softplus(-x)


def _modulate(x, g, shift, scale):
    ms = jnp.mean(x * x, axis=-1, keepdims=True)
    return (x * lax.rsqrt(ms + 1e-6)) * g * (1.0 + scale) + shift


def _iota2(shape, dim):
    return lax.broadcasted_iota(I32, shape, dim)


class _Rows:
    def __init__(self, n_batch, rows_per_batch, tile):
        self.n_batch = n_batch
        self.rows_per_batch = rows_per_batch
        self.rows = n_batch * rows_per_batch
        self.per_row_mod = rows_per_batch == 1
        self.tile = self.rows if self.per_row_mod else min(tile, rows_per_batch)
        assert self.rows % self.tile == 0 and (self.per_row_mod or rows_per_batch % self.tile == 0)
        self.n_tiles = self.rows // self.tile
        self.tiles_per_batch = 1 if self.per_row_mod else rows_per_batch // self.tile

    def spec(self, width, col=0):
        return pl.BlockSpec((self.tile, width), lambda i, *_: (i, col))

    def mod(self, mod_arr, layer, chunk, d):
        if self.per_row_mod:
            return mod_arr, pl.BlockSpec((None, self.rows, d), lambda i, *_: (layer, 0, chunk))
        tpb = self.tiles_per_batch
        arr = mod_arr.reshape(mod_arr.shape[0], mod_arr.shape[1], 1, mod_arr.shape[2])
        return arr, pl.BlockSpec((None, None, 1, d), lambda i, *_: (layer, i // tpb, 0, chunk))


def _full(arr):
    nd = arr.ndim
    return pl.BlockSpec(arr.shape, lambda *_: (0,) * nd)


def _mod_kernel(c_ref, w_ref, b_ref, o_ref):
    c = c_ref[...]
    o_ref[...] = _dot3(_silu(c), w_ref[...]) + b_ref[...]


def _modulation_all(c_all, ada_w, ada_b):
    n_layers, d, n6 = ada_w.shape
    rows = c_all.shape[0]
    tn = n6 // 8 if n6 % (8 * LANES) == 0 else n6
    return pl.pallas_call(
        _mod_kernel,
        out_shape=jax.ShapeDtypeStruct((n_layers, rows, n6), F32),
        grid=(n_layers, n6 // tn),
        in_specs=[pl.BlockSpec((rows, d), lambda l, j: (0, 0)),
                  pl.BlockSpec((None, d, tn), lambda l, j: (l, 0, j)),
                  pl.BlockSpec((None, 1, tn), lambda l, j: (l, 0, j))],
        out_specs=pl.BlockSpec((None, rows, tn), lambda l, j: (l, 0, j)),
        compiler_params=_params("arbitrary", "arbitrary"),
        name="adaln_modulation",
    )(c_all, ada_w, ada_b.reshape(n_layers, 1, n6))


def _proj_kernel(x_ref, g_ref, sh_ref, sc_ref, w_ref, *o_refs, segs):
    h = _modulate(x_ref[...], g_ref[...], sh_ref[...], sc_ref[...]).astype(BF16)
    for o_ref, (start, width) in zip(o_refs, segs):
        o_ref[...] = _dot(h, w_ref[:, start:start + width])


def _project(rows, x, g, mod, layer, w_bf16, segs):
    d = x.shape[1]
    sh_arr, sh_spec = rows.mod(mod, layer, 0, d)
    sc_arr, sc_spec = rows.mod(mod, layer, 1, d)
    return pl.pallas_call(
        functools.partial(_proj_kernel, segs=segs),
        out_shape=[jax.ShapeDtypeStruct((rows.rows, wd), F32) for _, wd in segs],
        grid=(rows.n_tiles,),
        in_specs=[rows.spec(d), _full(g), sh_spec, sc_spec, _full(w_bf16)],
        out_specs=[rows.spec(wd) for _, wd in segs],
        compiler_params=_params("arbitrary"),
        name="norm_mod_project",
    )(x, g, sh_arr, sc_arr, w_bf16)


def _proj_f32_kernel(x_ref, g_ref, sh_ref, sc_ref, w_ref, o_ref):
    o_ref[...] = _dot3(_modulate(x_ref[...], g_ref[...], sh_ref[...], sc_ref[...]), w_ref[...])


def _project_f32(rows, x, g, mod, layer, w_f32, segs):
    d, n = w_f32.shape
    cb = 512
    n_pad = -(-n // cb) * cb
    w_pad = jnp.pad(w_f32, ((0, 0), (0, n_pad - n)))
    sh_arr, sh_spec = rows.mod(mod, layer, 0, d)
    sc_arr, sc_spec = rows.mod(mod, layer, 1, d)
    assert rows.n_tiles == 1
    out = pl.pallas_call(
        _proj_f32_kernel,
        out_shape=jax.ShapeDtypeStruct((rows.rows, n_pad), F32),
        grid=(n_pad // cb,),
        in_specs=[pl.BlockSpec((rows.rows, d), lambda j: (0, 0)), _full(g),
                  pl.BlockSpec(sh_spec.block_shape, lambda j: sh_spec.index_map(0)),
                  pl.BlockSpec(sc_spec.block_shape, lambda j: sc_spec.index_map(0)),
                  pl.BlockSpec((d, cb), lambda j: (0, j))],
        out_specs=pl.BlockSpec((rows.rows, cb), lambda j: (0, j)),
        compiler_params=_params("arbitrary"),
        name="norm_mod_project_f32",
    )(x, g, sh_arr, sc_arr, w_pad)
    return [out[:, s:s + wd] for s, wd in segs]


def _logf_kernel(fa_ref, bf_ref, lf_ref, fc_ref, carry_ref, *, tiles_per_batch):
    i = pl.program_id(0)
    lf = _log_sigmoid(fa_ref[...] + bf_ref[...])
    lf_ref[...] = lf

    @pl.when(i % tiles_per_batch == 0)
    def _():
        carry_ref[...] = jnp.zeros_like(carry_ref)

    tm = lf.shape[0]
    tri = (_iota2((tm, tm), 1) <= _iota2((tm, tm), 0)).astype(BF16)
    cs = _dot_sel_lhs(tri, lf) + carry_ref[...]
    fc_ref[...] = cs
    carry_ref[...] = cs[tm - 1:tm, :]


def _forget_gates(rows, fa_raw, b_f_pad):
    return pl.pallas_call(
        functools.partial(_logf_kernel, tiles_per_batch=rows.tiles_per_batch),
        out_shape=[jax.ShapeDtypeStruct((rows.rows, LANES), F32)] * 2,
        grid=(rows.n_tiles,),
        in_specs=[rows.spec(LANES), _full(b_f_pad)],
        out_specs=[rows.spec(LANES)] * 2,
        scratch_shapes=[pltpu.VMEM((1, LANES), F32)],
        compiler_params=_params("arbitrary"),
        name="forget_gates",
    )(fa_raw, b_f_pad)


def _fox_kernel(q_ref, k_ref, v_ref, fc_ref, ft_ref, o_ref, fkb_ref, qm_ref, acc_ref, *, tq, dh, n_blocks, n_heads):
    qi = pl.program_id(1)
    seq = k_ref.shape[0]
    hpb = LANES // dh

    @pl.when(qi == 0)
    def _():
        for h in range(n_heads):
            fkb_ref[h] = jnp.broadcast_to(fc_ref[:, h:h + 1], (seq, LANES))

    q0 = pl.multiple_of(qi * tq, tq)
    lane = _iota2((1, LANES), 1)
    row_head = _iota2((LANES, 1), 0) // dh
    on_or_before = _iota2((tq, tq), 0) <= _iota2((tq, tq), 1)
    for p in range(n_blocks):
        q2 = q_ref[:, p * LANES:(p + 1) * LANES] * (dh ** -0.5)
        for j in range(hpb):
            qm_ref[p * hpb + j] = jnp.where(lane // dh == j, q2, 0.0).astype(BF16)
    acc_ref[...] = jnp.zeros_like(acc_ref)

    def block(k0, carry, diagonal):
        ms, ls = carry
        ss = []
        for p in range(n_blocks):
            k2 = k_ref[pl.ds(k0, tq), p * LANES:(p + 1) * LANES].astype(BF16)
            ss += [_dot_nt(k2, qm_ref[p * hpb + j]) for j in range(hpb)]
        new_ms, new_ls, alphas, pes = [], [], [], []
        for h in range(n_heads):
            fk = fkb_ref[h, pl.ds(k0, tq), :]
            s = ss[h] + (ft_ref[h:h + 1, pl.ds(q0, tq)] - jnp.concatenate([fk] * (tq // LANES), axis=1))
            if diagonal:
                s = jnp.where(on_or_before, s, NEG_INF)
            m_new = jnp.maximum(ms[h], jnp.max(s, axis=0, keepdims=True))
            alphas.append(jnp.exp(ms[h] - m_new))
            pe = jnp.exp(s - m_new)
            new_ms.append(m_new)
            new_ls.append(alphas[h] * ls[h] + jnp.sum(pe, axis=0, keepdims=True))
            pes.append(pe.astype(BF16))
        for p in range(n_blocks):
            v2 = v_ref[pl.ds(k0, tq), p * LANES:(p + 1) * LANES].astype(BF16)
            alpha2 = None
            pv2 = None
            for j in range(hpb):
                h = p * hpb + j
                pv = _dot_tn(v2, pes[h])
                sel = row_head == j
                alpha2 = jnp.where(sel, alphas[h], 0.0) if alpha2 is None else jnp.where(sel, alphas[h], alpha2)
                pv2 = jnp.where(sel, pv, 0.0) if pv2 is None else jnp.where(sel, pv, pv2)
            acc_ref[p] = alpha2 * acc_ref[p] + pv2
        return tuple(new_ms), tuple(new_ls)

    init = (tuple(jnp.full((1, tq), NEG_INF, F32) for _ in range(n_heads)),
            tuple(jnp.zeros((1, tq), F32) for _ in range(n_heads)))
    carry = lax.fori_loop(0, qi, lambda kv, c: block(pl.multiple_of(kv * tq, tq), c, False), init)
    ms, ls = block(q0, carry, True)
    for p in range(n_blocks):
        l2 = None
        for j in range(hpb):
            sel = row_head == j
            l2 = jnp.where(sel, ls[p * hpb + j], 1.0) if l2 is None else jnp.where(sel, ls[p * hpb + j], l2)
        o_ref[:, p * LANES:(p + 1) * LANES] = (acc_ref[p] / l2).T


def _fox_prompt(q, k, v, fcum, fcum_t, n_batch, seq, dh, n_heads):
    width = q.shape[1]
    assert width % LANES == 0 and LANES % dh == 0
    tq = min(256, seq)
    nq = seq // tq
    hp = fcum_t.shape[1]
    return pl.pallas_call(
        functools.partial(_fox_kernel, tq=tq, dh=dh, n_blocks=width // LANES, n_heads=n_heads),
        out_shape=jax.ShapeDtypeStruct(q.shape, F32),
        grid=(n_batch, nq),
        in_specs=[pl.BlockSpec((tq, width), lambda b, i: (b * nq + i, 0)),
                  pl.BlockSpec((seq, width), lambda b, i: (b, 0)),
                  pl.BlockSpec((seq, width), lambda b, i: (b, 0)),
                  pl.BlockSpec((seq, LANES), lambda b, i: (b, 0)),
                  pl.BlockSpec((None, hp, seq), lambda b, i: (b, 0, 0))],
        out_specs=pl.BlockSpec((tq, width), lambda b, i: (b * nq + i, 0)),
        scratch_shapes=[pltpu.VMEM((n_heads, seq, LANES), F32), pltpu.VMEM((n_heads, tq, LANES), BF16),
                        pltpu.VMEM((width // LANES, LANES, tq), F32)],
        compiler_params=_params("arbitrary", "arbitrary"),
        name="fox_prompt_attention",
    )(q, k, v, fcum, fcum_t)


def _rope_tables(pos, n_heads, dk):
    half = dk // 2
    freq = ROPE_THETA ** (-jnp.arange(half, dtype=F32) / half)
    ang = pos.astype(F32)[:, None] * freq[None, :]
    cos = jnp.cos(ang)
    sin = jnp.sin(ang)
    cos_h = jnp.concatenate([cos, cos], axis=-1)
    sin_h = jnp.concatenate([-sin, sin], axis=-1)
    return jnp.tile(cos_h, (1, n_heads)), jnp.tile(sin_h, (1, n_heads))


def _rope(x, cos, sin, dk):
    half = dk // 2
    lane = _iota2((1, LANES), 1)
    up = pltpu.roll(x, LANES - half, 1)
    down = pltpu.roll(x, half, 1)
    partner = jnp.where((lane % dk) < half, up, down)
    return x * cos + partner * sin


def _ret_kernel(q_ref, k_ref, v_ref, cos_ref, sin_ref, o_ref, st_ref, state, *, n_heads, dk, dv, log_gammas):
    c = pl.program_id(1)
    hpb = LANES // dk

    @pl.when(c == 0)
    def _():
        state[...] = jnp.zeros_like(state)

    lane = _iota2((1, LANES), 1)
    t_col = _iota2((CHUNK, 1), 0).astype(F32)
    seg = (_iota2((CHUNK, CHUNK), 0) - _iota2((CHUNK, CHUNK), 1)).astype(F32)
    causal = seg >= 0.0
    row_head = _iota2((LANES, 1), 0) // dk
    for p in range(n_heads // hpb):
        cols = slice(p * LANES, (p + 1) * LANES)
        cos = cos_ref[:, cols]
        sin = sin_ref[:, cols]
        q2 = _rope(q_ref[:, cols], cos, sin, dk)
        k2 = _rope(k_ref[:, cols], cos, sin, dk) * (dk ** -0.5)
        k2b = k2.astype(BF16)
        st2 = state[p]
        st2b = st2.astype(BF16)
        upd = None
        row_decay = None
        for j in range(hpb):
            h = p * hpb + j
            lg = log_gammas[h]
            sel = lane // dk == j
            qh = jnp.where(sel, q2, 0.0).astype(BF16)
            s = _dot_nt(qh, k2b)
            decay = jnp.exp(jnp.where(causal, seg * lg, NEG_INF))
            vh = v_ref[:, h * dv:(h + 1) * dv]
            y = _dot((s * decay).astype(BF16), vh.astype(BF16))
            y = y + jnp.exp((t_col + 1.0) * lg) * _dot(qh, st2b)
            o_ref[:, h * dv:(h + 1) * dv] = y
            tail = jnp.exp((CHUNK - 1.0 - t_col) * lg)
            kh = jnp.where(sel, k2, 0.0).astype(BF16)
            u = _dot_tn(kh, (vh * tail).astype(BF16))
            upd = u if upd is None else upd + u
            rd = jnp.where(row_head == j, math.exp(CHUNK * lg), 0.0)
            row_decay = rd if row_decay is None else row_decay + rd
        state[p] = row_decay * st2 + upd

    @pl.when(c == pl.num_programs(1) - 1)
    def _():
        for h in range(n_heads):
            p, j = divmod(h, hpb)
            st_ref[h] = state[p, j * dk:(j + 1) * dk, :]


def _retention_prompt(qr, kr, vr, cos, sin, n_batch, seq, n_heads, dk, dv):
    nc = seq // CHUNK
    log_gammas = tuple(math.log1p(-2.0 ** (-RET_DECAY_BASE - h)) for h in range(n_heads))
    wq = n_heads * dk
    wv = n_heads * dv
    assert wq % LANES == 0 and LANES % dk == 0 and dv % LANES == 0
    return pl.pallas_call(
        functools.partial(_ret_kernel, n_heads=n_heads, dk=dk, dv=dv, log_gammas=log_gammas),
        out_shape=[jax.ShapeDtypeStruct((n_batch * seq, wv), F32),
                   jax.ShapeDtypeStruct((n_batch, n_heads, dk, dv), F32)],
        grid=(n_batch, nc),
        in_specs=[pl.BlockSpec((CHUNK, wq), lambda b, c: (b * nc + c, 0)),
                  pl.BlockSpec((CHUNK, wq), lambda b, c: (b * nc + c, 0)),
                  pl.BlockSpec((CHUNK, wv), lambda b, c: (b * nc + c, 0)),
                  pl.BlockSpec((CHUNK, wq), lambda b, c: (c, 0)),
                  pl.BlockSpec((CHUNK, wq), lambda b, c: (c, 0))],
        out_specs=[pl.BlockSpec((CHUNK, wv), lambda b, c: (b * nc + c, 0)),
                   pl.BlockSpec((None, n_heads, dk, dv), lambda b, c: (b, 0, 0, 0))],
        scratch_shapes=[pltpu.VMEM((wq // LANES, LANES, dv), F32)],
        compiler_params=_params("arbitrary", "arbitrary"),
        name="retention_prompt_scan",
    )(qr, kr, vr, cos, sin)


def _att_out_kernel(of_ref, or_ref, gt_ref, y_ref, gm_ref, gg_ref, gb_ref, w1_ref, w2_ref, o_ref, *, n_heads, dv):
    parts = []
    for h in range(n_heads):
        r = or_ref[:, h * dv:(h + 1) * dv]
        mu = jnp.mean(r, axis=-1, keepdims=True)
        d = r - mu
        var = jnp.mean(d * d, axis=-1, keepdims=True)
        parts.append(d * lax.rsqrt(var + 1e-5))
    r = jnp.concatenate(parts, axis=1) * gg_ref[...] + gb_ref[...]
    r = r * _silu(gt_ref[...])
    o = _dot_w(of_ref[...], w1_ref) + _dot_w(r, w2_ref)
    o_ref[...] = y_ref[...] + gm_ref[...] * o


def _att_output(rows, o_fox, o_ret, gate, y, mod, layer, gn_g, gn_b, w1, w2, n_heads, dv):
    d = y.shape[1]
    gm_arr, gm_spec = rows.mod(mod, layer, 2, d)
    return pl.pallas_call(
        functools.partial(_att_out_kernel, n_heads=n_heads, dv=dv),
        out_shape=jax.ShapeDtypeStruct(y.shape, F32),
        grid=(rows.n_tiles,),
        in_specs=[rows.spec(o_fox.shape[1]), rows.spec(o_ret.shape[1]), rows.spec(gate.shape[1]), rows.spec(d),
                  gm_spec, _full(gn_g), _full(gn_b), _full(w1), _full(w2)],
        out_specs=rows.spec(d),
        compiler_params=_params("arbitrary"),
        name="attention_output",
    )(o_fox, o_ret, gate, y, gm_arr, gn_g, gn_b, w1, w2)


def _mm_res_kernel(a_ref, y_ref, gm_ref, w_ref, o_ref):
    o_ref[...] = y_ref[...] + gm_ref[...] * _dot_w(a_ref[...], w_ref)


def _matmul_residual(rows, a, y, mod, layer, w):
    d = y.shape[1]
    gm_arr, gm_spec = rows.mod(mod, layer, 2, d)
    return pl.pallas_call(
        _mm_res_kernel,
        out_shape=jax.ShapeDtypeStruct(y.shape, F32),
        grid=(rows.n_tiles,),
        in_specs=[rows.spec(a.shape[1]), rows.spec(d), gm_spec, _full(w)],
        out_specs=rows.spec(d),
        compiler_params=_params("arbitrary"),
        name="matmul_gated_residual",
    )(a, y, gm_arr, w)


def _ssd_conv_act(taps, cw_ref, cb_ref):
    conv = cb_ref[...] + cw_ref[0:1, :] * taps[0]
    for i in range(1, len(taps)):
        conv = conv + cw_ref[i:i + 1, :] * taps[i]
    return _silu(conv)


def _ssd_gate_norm(y, x, z, dsk, ng, n_groups):
    y = (y + dsk * x) * _silu(z)
    gw = y.shape[1] // n_groups
    parts = []
    for g in range(n_groups):
        blk = y[:, g * gw:(g + 1) * gw]
        ms = jnp.mean(blk * blk, axis=-1, keepdims=True)
        parts.append(blk * lax.rsqrt(ms + 1e-5))
    return jnp.concatenate(parts, axis=1) * ng


def _ssd_kernel(xr_ref, z_ref, dt_ref, cw_ref, cb_ref, dtb_ref, alog_ref, dsk_ref, ng_ref, e_ref,
                y_ref, st_ref, cv_ref, prev, state, *, d_inner, n_groups, d_state, hd):
    c = pl.program_id(1)
    hpg = d_inner // hd // n_groups
    ppg = hpg * hd // LANES

    @pl.when(c == 0)
    def _():
        prev[...] = jnp.zeros_like(prev)
        state[...] = jnp.zeros_like(state)

    cur = xr_ref[...]
    xw = jnp.concatenate([prev[...], cur], axis=0)
    taps = [xw[5 + i:5 + i + CHUNK] for i in range(3)] + [cur]
    tail_rows = cur[CHUNK - 8:CHUNK]
    prev[...] = tail_rows
    cv_ref[...] = tail_rows
    xbc = _ssd_conv_act(taps, cw_ref, cb_ref)
    gn = n_groups * d_state
    x = xbc[:, :d_inner]
    bm = xbc[:, d_inner:d_inner + gn].astype(BF16)
    cm = xbc[:, d_inner + gn:].astype(BF16)

    dt = _softplus(dt_ref[...] + dtb_ref[...])
    la = -dt * jnp.exp(alog_ref[...])
    seg = _iota2((CHUNK, CHUNK), 0) - _iota2((CHUNK, CHUNK), 1)
    causal = seg >= 0
    cum = _dot_sel_lhs(causal.astype(BF16), la)
    cum_t = cum.T
    e = e_ref[...]
    dt_e = _dot_sel_rhs(dt, e)
    cum_e = _dot_sel_rhs(cum, e)
    ecum_e = jnp.exp(cum_e)
    tail_e = jnp.exp(cum_e[CHUNK - 1:CHUNK, :] - cum_e)
    v = x * dt_e
    vb = v.astype(BF16)
    vtb = (v * tail_e).astype(BF16)
    lane = _iota2((1, LANES), 1)
    hpb = LANES // hd
    ys = []
    for g in range(n_groups):
        cmg = cm[:, g * d_state:(g + 1) * d_state]
        bmg = bm[:, g * d_state:(g + 1) * d_state]
        s = _dot_nt(cmg, bmg)
        for pp in range(ppg):
            p = g * ppg + pp
            cols = slice(p * LANES, (p + 1) * LANES)
            v2 = vb[:, cols]
            yp = None
            for j in range(hpb):
                h = p * hpb + j
                d = jnp.exp(jnp.where(causal, cum[:, h:h + 1] - cum_t[h:h + 1, :], NEG_INF))
                yj = _dot((s * d).astype(BF16), v2)
                sel = lane // hd == j
                yp = jnp.where(sel, yj, 0.0) if yp is None else jnp.where(sel, yj, yp)
            st2 = state[p]
            yp = yp + ecum_e[:, cols] * _dot(cmg, st2.astype(BF16))
            ys.append(yp)
            state[p] = ecum_e[CHUNK - 1:CHUNK, cols] * st2 + _dot_tn(bmg, vtb[:, cols])
    y = jnp.concatenate(ys, axis=1)
    y_ref[...] = _ssd_gate_norm(y, x, z_ref[...], dsk_ref[...], ng_ref[...], n_groups)

    @pl.when(c == pl.num_programs(1) - 1)
    def _():
        st_ref[...] = state[...]


def _ssd_prompt(xr, z, dtr, cw, cb, dtb, alog, dsk, ng, e_mat, n_batch, seq, d_inner, n_groups, d_state, hd):
    nc = seq // CHUNK
    cd = xr.shape[1]
    n_blk = d_inner // LANES
    row = lambda b, c: (b * nc + c, 0)
    return pl.pallas_call(
        functools.partial(_ssd_kernel, d_inner=d_inner, n_groups=n_groups, d_state=d_state, hd=hd),
        out_shape=[jax.ShapeDtypeStruct((n_batch * seq, d_inner), F32),
                   jax.ShapeDtypeStruct((n_batch, n_blk, d_state, LANES), F32),
                   jax.ShapeDtypeStruct((n_batch, 8, cd), F32)],
        grid=(n_batch, nc),
        in_specs=[pl.BlockSpec((CHUNK, cd), row), pl.BlockSpec((CHUNK, d_inner), row),
                  pl.BlockSpec((CHUNK, LANES), row),
                  _full(cw), _full(cb), _full(dtb), _full(alog), _full(dsk), _full(ng), _full(e_mat)],
        out_specs=[pl.BlockSpec((CHUNK, d_inner), row),
                   pl.BlockSpec((None, n_blk, d_state, LANES), lambda b, c: (b, 0, 0, 0)),
                   pl.BlockSpec((None, 8, cd), lambda b, c: (b, 0, 0))],
        scratch_shapes=[pltpu.VMEM((8, cd), F32), pltpu.VMEM((n_blk, d_state, LANES), F32)],
        compiler_params=_params("arbitrary", "arbitrary"),
        name="ssd_prompt_scan",
    )(xr, z, dtr, cw, cb, dtb, alog, dsk, ng, e_mat)


def _ssd_step_prep_kernel(xr_ref, c0_ref, c1_ref, c2_ref, dt_ref, cw_ref, cb_ref, dtb_ref, alog_ref, e_ref,
                          x_ref, bm_ref, cm_ref, v_ref, a_ref, *, d_inner, gn):
    xbc = _ssd_conv_act([c0_ref[...], c1_ref[...], c2_ref[...], xr_ref[...]], cw_ref, cb_ref)
    x = xbc[:, :d_inner]
    x_ref[...] = x
    bm_ref[...] = xbc[:, d_inner:d_inner + gn]
    cm_ref[...] = xbc[:, d_inner + gn:]
    dt = _softplus(dt_ref[...] + dtb_ref[...])
    a_ref[...] = jnp.exp(-dt * jnp.exp(alog_ref[...]))
    v_ref[...] = x * _dot_sel_rhs(dt, e_ref[...])


def _ssd_step_prep(xr, taps, dtr, cw, cb, dtb, alog, e_mat, d_inner, gn):
    n = xr.shape[0]
    args = (xr, *taps, dtr, cw, cb, dtb, alog, e_mat)
    return pl.pallas_call(
        functools.partial(_ssd_step_prep_kernel, d_inner=d_inner, gn=gn),
        out_shape=[jax.ShapeDtypeStruct((n, d_inner), F32), jax.ShapeDtypeStruct((n, gn), F32),
                   jax.ShapeDtypeStruct((n, gn), F32), jax.ShapeDtypeStruct((n, d_inner), F32),
                   jax.ShapeDtypeStruct((n, LANES), F32)],
        grid=(1,),
        in_specs=[_full(a) for a in args],
        out_specs=[pl.BlockSpec((n, d_inner), lambda i: (0, 0)), pl.BlockSpec((n, gn), lambda i: (0, 0)),
                   pl.BlockSpec((n, gn), lambda i: (0, 0)), pl.BlockSpec((n, d_inner), lambda i: (0, 0)),
                   pl.BlockSpec((n, LANES), lambda i: (0, 0))],
        compiler_params=_params("arbitrary"),
        name="ssd_step_prep",
    )(*args)


def _ssd_step_post_kernel(y_ref, x_ref, z_ref, dsk_ref, ng_ref, o_ref, *, n_groups):
    o_ref[...] = _ssd_gate_norm(y_ref[...], x_ref[...], z_ref[...], dsk_ref[...], ng_ref[...], n_groups)


def _ssd_step_post(y, x, z, dsk, ng, n_groups):
    args = (y, x, z, dsk, ng)
    return pl.pallas_call(
        functools.partial(_ssd_step_post_kernel, n_groups=n_groups),
        out_shape=jax.ShapeDtypeStruct(y.shape, F32),
        grid=(1,),
        in_specs=[_full(a) for a in args],
        out_specs=pl.BlockSpec(y.shape, lambda i: (0, 0)),
        compiler_params=_params("arbitrary"),
        name="ssd_step_post",
    )(*args)


def _state_step_kernel(s_ref, a_ref, k_ref, q_ref, v_ref, so_ref, y_ref, *, n_heads, heads_per_key):
    for h in range(n_heads):
        g = h // heads_per_key
        new = a_ref[h] * s_ref[h] + k_ref[g] * v_ref[h]
        so_ref[h] = new
        y_ref[h] = jnp.sum(q_ref[g] * new, axis=0, keepdims=True)


def _state_step(state, a, k_col, q_col, v_row):
    n_b, n_h, n_n, n_v = state.shape
    n_g = k_col.shape[1]
    blk = lambda arr: pl.BlockSpec((None,) + arr.shape[1:], lambda b: (b, 0, 0, 0))
    return pl.pallas_call(
        functools.partial(_state_step_kernel, n_heads=n_h, heads_per_key=n_h // n_g),
        out_shape=[jax.ShapeDtypeStruct(state.shape, F32), jax.ShapeDtypeStruct((n_b, n_h, 1, n_v), F32)],
        grid=(n_b,),
        in_specs=[blk(state), blk(a), blk(k_col), blk(q_col), blk(v_row)],
        out_specs=[blk(state), pl.BlockSpec((None, n_h, 1, n_v), lambda b: (b, 0, 0, 0))],
        compiler_params=_params("arbitrary"),
        name="state_step",
    )(state, a, k_col, q_col, v_row)


def _rope_rows_kernel(q_ref, k_ref, cos_ref, sin_ref, qo_ref, ko_ref, *, dk):
    for p in range(q_ref.shape[1] // LANES):
        cols = slice(p * LANES, (p + 1) * LANES)
        qo_ref[:, cols] = _rope(q_ref[:, cols], cos_ref[:, cols], sin_ref[:, cols], dk)
        ko_ref[:, cols] = _rope(k_ref[:, cols], cos_ref[:, cols], sin_ref[:, cols], dk) * (dk ** -0.5)


def _rope_rows(q, k, cos, sin, dk):
    args = (q, k, cos, sin)
    return pl.pallas_call(
        functools.partial(_rope_rows_kernel, dk=dk),
        out_shape=[jax.ShapeDtypeStruct(q.shape, F32)] * 2,
        grid=(1,),
        in_specs=[_full(a) for a in args],
        out_specs=[pl.BlockSpec(q.shape, lambda i: (0, 0))] * 2,
        compiler_params=_params("arbitrary"),
        name="rope_rows",
    )(*args)


def _fox_decode_kernel(pt_ref, q_ref, kn_ref, vn_ref, lfn_ref, kt_hbm, vt_hbm, lf_hbm, o_ref,
                       kbuf, vbuf, lbuf, sem, qb_ref, acc_ref, *, layer, n_pages, group, n_slots, n_heads, dh, page):
    b = pl.program_id(0)
    nb = pl.num_programs(0)
    n_groups = n_pages // group
    hd = n_heads * dh

    def copies(bb, gi, slot):
        out = []
        for g in range(group):
            pid = pt_ref[bb, gi * group + g]
            out.append(pltpu.make_async_copy(kt_hbm.at[layer, pid], kbuf.at[slot, g], sem.at[slot, 0]))
            out.append(pltpu.make_async_copy(vt_hbm.at[layer, pid], vbuf.at[slot, g], sem.at[slot, 1]))
            out.append(pltpu.make_async_copy(lf_hbm.at[layer, pid], lbuf.at[slot, g], sem.at[slot, 2]))
        return out

    def start(bb, gi, slot):
        for c in copies(bb, gi, slot):
            c.start()

    def wait(bb, gi, slot):
        for c in copies(bb, gi, slot):
            c.wait()

    @pl.when(b == 0)
    def _():
        for g in range(n_slots - 1):
            start(b, g, g)

    q_row = q_ref[...] * (dh ** -0.5)
    qb_ref[...] = jnp.broadcast_to(q_row, (page, hd)).T.reshape(n_heads, dh, page)
    acc_ref[...] = jnp.zeros_like(acc_ref)
    tri = (_iota2((page, page), 0) <= _iota2((page, page), 1)).astype(BF16)

    def page_update(slot, g, carry):
        m, l, run = carry
        s = jnp.concatenate(
            [jnp.sum(kbuf[slot, g, h] * qb_ref[h], axis=0, keepdims=True) for h in range(n_heads)], axis=0)
        fcum = _dot_sel_rhs(lbuf[slot, g], tri) + run
        s = s - fcum
        m_new = jnp.maximum(m, jnp.max(s, axis=-1, keepdims=True))
        alpha = jnp.exp(m - m_new)
        pe = jnp.exp(s - m_new)
        l_new = alpha * l + jnp.sum(pe, axis=-1, keepdims=True)
        for h in range(n_heads):
            acc_ref[h] = alpha[h:h + 1, :] * acc_ref[h] + pe[h:h + 1, :] * vbuf[slot, g, h]
        return m_new, l_new, fcum[:, page - 1:page]

    def ring(gq, carry):
        for s in range(n_slots):
            gi = gq * n_slots + s
            wait(b, gi, s)
            ahead = gi + n_slots - 1
            refill = (s + n_slots - 1) % n_slots

            @pl.when(ahead < n_groups)
            def _():
                start(b, ahead, refill)

            @pl.when(jnp.logical_and(ahead >= n_groups, b + 1 < nb))
            def _():
                start(b + 1, ahead - n_groups, refill)

            for g in range(group):
                carry = page_update(s, g, carry)
        return carry

    init = (jnp.full((n_heads, 1), NEG_INF, F32), jnp.zeros((n_heads, 1), F32), jnp.zeros((n_heads, 1), F32))
    m, l, f_tot = lax.fori_loop(0, n_groups // n_slots, ring, init)

    own = (_iota2((n_heads, hd), 1) // dh) == _iota2((n_heads, hd), 0)
    spread = lambda col: jnp.sum(jnp.where(own, col, 0.0), axis=0, keepdims=True)
    lfn = jnp.sum(jnp.where(_iota2((n_heads, LANES), 1) == _iota2((n_heads, LANES), 0), lfn_ref[...], 0.0),
                  axis=1, keepdims=True)
    s_new = jnp.sum(jnp.where(own, q_row * kn_ref[...], 0.0), axis=1, keepdims=True) - (f_tot + lfn)
    m_fin = jnp.maximum(m, s_new)
    a2 = jnp.exp(m - m_fin)
    p_new = jnp.exp(s_new - m_fin)
    l_fin = a2 * l + p_new
    o_past = jnp.sum(acc_ref[...].reshape(hd, page).T, axis=0, keepdims=True)
    o_ref[...] = (spread(a2) * o_past + spread(p_new) * vn_ref[...]) / spread(l_fin)


def _fox_decode(page_table, q, k_new, v_new, lf_new, kt_pages, vt_pages, lf_pages_t, layer):
    n_b, _, hd = q.shape
    n_pages = page_table.shape[1]
    _, _, n_h, dh, page = kt_pages.shape
    group = max(1, min(4, n_pages // 2))
    n_slots = 4 if n_pages % (4 * group) == 0 else 2
    assert n_pages % (n_slots * group) == 0
    tok = lambda arr: pl.BlockSpec((None, 1, arr.shape[2]), lambda b, pt: (b, 0, 0))
    grid_spec = pltpu.PrefetchScalarGridSpec(
        num_scalar_prefetch=1,
        grid=(n_b,),
        in_specs=[tok(q), tok(k_new), tok(v_new), tok(lf_new),
                  pl.BlockSpec(memory_space=pl.ANY), pl.BlockSpec(memory_space=pl.ANY),
                  pl.BlockSpec(memory_space=pl.ANY)],
        out_specs=pl.BlockSpec((None, 1, hd), lambda b, pt: (b, 0, 0)),
        scratch_shapes=[pltpu.VMEM((n_slots, group, n_h, dh, page), F32),
                        pltpu.VMEM((n_slots, group, n_h, dh, page), F32),
                        pltpu.VMEM((n_slots, group, n_h, page), F32), pltpu.SemaphoreType.DMA((n_slots, 3)),
                        pltpu.VMEM((n_h, dh, page), F32), pltpu.VMEM((n_h, dh, page), F32)],
    )
    return pl.pallas_call(
        functools.partial(_fox_decode_kernel, layer=layer, n_pages=n_pages, group=group, n_slots=n_slots,
                          n_heads=n_h, dh=dh, page=page),
        out_shape=jax.ShapeDtypeStruct((n_b, 1, hd), F32),
        grid_spec=grid_spec,
        compiler_params=_params("arbitrary"),
        name="fox_paged_decode",
    )(page_table, q, k_new, v_new, lf_new, kt_pages, vt_pages, lf_pages_t)


def _store_row_tiles(ref, val):
    n, d = val.shape
    nb = d // LANES
    for c in range(nb):
        ref[pl.ds(c, n, stride=nb), :] = val[:, c * LANES:(c + 1) * LANES]


def _load_row_tiles(ref, n, nb):
    return jnp.concatenate([ref[pl.ds(c, n, stride=nb), :] for c in range(nb)], axis=1)


def _router_kernel(y_ref, g_ref, sh_ref, sc_ref, rwt_ref, rb_ref, cin_ref,
                   h_ref, idx_ref, w_ref, rk_ref, cnt_ref, carry, *, n_exp):
    i = pl.program_id(0)

    @pl.when(i == 0)
    def _():
        carry[...] = cin_ref[...]

    h = _modulate(y_ref[...], g_ref[...], sh_ref[...], sc_ref[...])
    _store_row_tiles(h_ref, h)
    tm = h.shape[0]
    logits = _dot3(rwt_ref[...], h, dot=_dot_nt) + rb_ref[...]
    eio = _iota2((n_exp, tm), 0)
    vals, idxs = [], []
    rest = logits
    for _ in range(TOP_K):
        m = jnp.max(rest, axis=0, keepdims=True)
        ik = jnp.min(jnp.where(rest == m, eio, n_exp), axis=0, keepdims=True)
        vals.append(m)
        idxs.append(ik)
        rest = jnp.where(eio == ik, NEG_INF, rest)
    ex = [jnp.exp(v - vals[0]) for v in vals]
    den = ex[0]
    for e in ex[1:]:
        den = den + e
    sel = jnp.zeros((n_exp, tm), F32)
    for ik in idxs:
        sel = sel + jnp.where(eio == ik, 1.0, 0.0)
    before = (_iota2((tm, tm), 0) < _iota2((tm, tm), 1)).astype(BF16)
    rank_all = _dot(sel.astype(BF16), before) + carry[:, 0:1]
    ranks = [jnp.sum(jnp.where(eio == ik, rank_all, 0.0), axis=0, keepdims=True) for ik in idxs]
    carry[...] = carry[...] + jnp.sum(sel, axis=1, keepdims=True)
    idx_ref[...] = jnp.concatenate(idxs, axis=0)
    w_ref[...] = jnp.concatenate([e / den for e in ex], axis=0)
    rk_ref[...] = jnp.concatenate(ranks, axis=0).astype(I32)
    cnt_ref[...] = carry[...]


def _route(rows, y, g, mod, layer, rw_t, rb_col, cnt_in):
    d = y.shape[1]
    n_exp = rw_t.shape[0]
    sh_arr, sh_spec = rows.mod(mod, layer, 3, d)
    sc_arr, sc_spec = rows.mod(mod, layer, 4, d)
    kt = pl.BlockSpec((TOP_K, rows.tile), lambda i: (0, i))
    nb = d // LANES
    return pl.pallas_call(
        functools.partial(_router_kernel, n_exp=n_exp),
        out_shape=[jax.ShapeDtypeStruct((rows.rows * nb, LANES), F32),
                   jax.ShapeDtypeStruct((TOP_K, rows.rows), I32),
                   jax.ShapeDtypeStruct((TOP_K, rows.rows), F32),
                   jax.ShapeDtypeStruct((TOP_K, rows.rows), I32),
                   jax.ShapeDtypeStruct((n_exp, LANES), F32)],
        grid=(rows.n_tiles,),
        in_specs=[rows.spec(d), _full(g), sh_spec, sc_spec, _full(rw_t), _full(rb_col), _full(cnt_in)],
        out_specs=[pl.BlockSpec((rows.tile * nb, LANES), lambda i: (i, 0)), kt, kt, kt,
                   pl.BlockSpec((n_exp, LANES), lambda i: (0, 0))],
        scratch_shapes=[pltpu.VMEM((n_exp, LANES), F32)],
        compiler_params=_params("arbitrary"),
        name="moe_router",
    )(y, g, sh_arr, sc_arr, rw_t, rb_col, cnt_in)


def _dispatch_kernel(slot_ref, slot2_ref, pad_ref, nv_ref, h_ref, h2_ref, xs_ref, sem, zeros,
                     *, tm, n_rows, n_rows2, nb, n_exp, tmf, n_tiles):
    i = pl.program_id(0)

    @pl.when(i == 0)
    def _():
        zeros[...] = jnp.zeros_like(zeros)
        fills = [pltpu.make_async_copy(zeros, xs_ref.at[pl.ds(pl.multiple_of(pad_ref[e] * nb, nb), tmf * nb), :], sem)
                 for e in range(n_exp)]
        for c in fills:
            c.start()
        for c in fills:
            c.wait()

        def tail(j):
            return pltpu.make_async_copy(zeros, xs_ref.at[pl.ds(pl.multiple_of(j * (tmf * nb), tmf * nb), tmf * nb), :], sem)

        def start_tail(j, carry):
            tail(j).start()
            return carry

        def wait_tail(j, carry):
            tail(j).wait()
            return carry

        lax.fori_loop(nv_ref[0], n_tiles, start_tail, 0)
        lax.fori_loop(nv_ref[0], n_tiles, wait_tail, 0)

    def scatter(src_ref, slots, base, count, stride):
        def issue(t, carry):
            for k in range(TOP_K):
                s = slots[k * stride + base + t]
                pltpu.make_async_copy(src_ref.at[pl.ds(pl.multiple_of(t * nb, nb), nb), :],
                                      xs_ref.at[pl.ds(pl.multiple_of(s * nb, nb), nb), :], sem).start(priority=k % 2)
            return carry

        lax.fori_loop(0, count, issue, 0, unroll=2)
        for k in range(TOP_K):
            pltpu.make_async_copy(src_ref, xs_ref.at[pl.ds(0, count * nb), :], sem).wait()

    scatter(h_ref, slot_ref, i * tm, tm, n_rows)

    @pl.when(i == pl.num_programs(0) - 1)
    def _():
        scatter(h2_ref, slot2_ref, 0, n_rows2, n_rows2)


def _dispatch(rows, slots, slots2, pad_start, n_valid, h, h2, n_tiles, tmf, nb):
    n_exp = pad_start.shape[0]
    n_rows2 = h2.shape[0] // nb
    grid_spec = pltpu.PrefetchScalarGridSpec(
        num_scalar_prefetch=4,
        grid=(rows.n_tiles,),
        in_specs=[pl.BlockSpec((rows.tile * nb, LANES), lambda i, *_: (i, 0)),
                  pl.BlockSpec(h2.shape, lambda i, *_: (0, 0))],
        out_specs=pl.BlockSpec(memory_space=pl.ANY),
        scratch_shapes=[pltpu.SemaphoreType.DMA(()), pltpu.VMEM((tmf * nb, LANES), F32)],
    )
    return pl.pallas_call(
        functools.partial(_dispatch_kernel, tm=rows.tile, n_rows=rows.rows, n_rows2=n_rows2, nb=nb, n_exp=n_exp,
                          tmf=tmf, n_tiles=n_tiles),
        out_shape=jax.ShapeDtypeStruct((n_tiles * tmf * nb, LANES), F32),
        grid_spec=grid_spec,
        compiler_params=_params("arbitrary"),
        name="moe_dispatch",
    )(slots, slots2, pad_start, n_valid, h, h2)


def _split_gate_up_kernel(w_ref, wg_ref, wu_ref, wt_ref, *, chunk):
    half = chunk // 2
    for c in range(w_ref.shape[1] // chunk):
        for r in range(w_ref.shape[0] // LANES):
            rows = slice(r * LANES, (r + 1) * LANES)
            wt_ref[...] = w_ref[rows, c * chunk:(c + 1) * chunk].T
            wg_ref[rows, c * half:(c + 1) * half] = wt_ref[pl.ds(0, half, stride=2), :].T.astype(BF16)
            wu_ref[rows, c * half:(c + 1) * half] = wt_ref[pl.ds(1, half, stride=2), :].T.astype(BF16)


def _split_gate_up(w_up):
    n_l, n_e, d, f2 = w_up.shape
    chunk = min(512, f2)
    blk = lambda width: pl.BlockSpec((None, None, d, width), lambda l, e: (l, e, 0, 0))
    return pl.pallas_call(
        functools.partial(_split_gate_up_kernel, chunk=chunk),
        out_shape=[jax.ShapeDtypeStruct((n_l, n_e, d, f2 // 2), BF16)] * 2,
        grid=(n_l, n_e),
        in_specs=[blk(f2)],
        out_specs=[blk(f2 // 2)] * 2,
        scratch_shapes=[pltpu.VMEM((chunk, LANES), F32)],
        compiler_params=_params("arbitrary", "arbitrary"),
        name="moe_split_gate_up",
    )(w_up)


def _ffn_kernel(te_ref, tv_ref, nv_ref, xs_ref, wg_ref, wu_ref, bg_ref, bu_ref, wd_ref, bd_ref, ys_ref, *, tmf, nb):
    del te_ref, nv_ref
    i = pl.program_id(0)

    @pl.when(tv_ref[i] == 1)
    def _():
        x = _load_row_tiles(xs_ref, tmf, nb).astype(BF16)
        gate = jnp.minimum(_dot(x, wg_ref[...]) + bg_ref[...], SWIGLU_LIMIT)
        up = jnp.clip(_dot(x, wu_ref[...]) + bu_ref[...], -SWIGLU_LIMIT, SWIGLU_LIMIT)
        act = (up + 1.0) * gate * _sigmoid(SWIGLU_ALPHA * gate)
        _store_row_tiles(ys_ref, _dot(act.astype(BF16), wd_ref[...].astype(BF16)) + bd_ref[...])

    @pl.when(tv_ref[i] == 0)
    def _():
        ys_ref[...] = jnp.zeros_like(ys_ref)


def _expert_ffn(tile_expert, tile_valid, n_valid, xs, layer, wg, wu, bg, bu, wd, bd, tmf, nb):
    n_tiles = tile_expert.shape[0]
    d, f = wg.shape[2], wg.shape[3]
    ex = lambda i, te, tv, nv: (te[i], 0, 0)
    lex = lambda i, te, tv, nv: (layer, te[i], 0, 0)
    grid_spec = pltpu.PrefetchScalarGridSpec(
        num_scalar_prefetch=3,
        grid=(n_tiles,),
        in_specs=[pl.BlockSpec((tmf * nb, LANES), lambda i, te, tv, nv: (jnp.minimum(i, nv[0] - 1), 0)),
                  pl.BlockSpec((None, None, d, f), lex), pl.BlockSpec((None, None, d, f), lex),
                  pl.BlockSpec((None, 1, f), ex), pl.BlockSpec((None, 1, f), ex),
                  pl.BlockSpec((None, None, f, d), lex), pl.BlockSpec((None, 1, d), ex)],
        out_specs=pl.BlockSpec((tmf * nb, LANES), lambda i, te, tv, nv: (i, 0)),
    )
    return pl.pallas_call(
        functools.partial(_ffn_kernel, tmf=tmf, nb=nb),
        out_shape=jax.ShapeDtypeStruct((n_tiles * tmf * nb, LANES), F32),
        grid_spec=grid_spec,
        compiler_params=_params("arbitrary"),
        name="moe_expert_ffn",
    )(tile_expert, tile_valid, n_valid, xs, wg, wu, bg, bu, wd, bd)


def _combine_kernel(slot_ref, ys_ref, w_ref, y_ref, gm_ref, o_ref, buf, sem, *, tm, n_rows, nb):
    i = pl.program_id(0)

    def issue(t, carry):
        for k in range(TOP_K):
            s = slot_ref[k * n_rows + i * tm + t]
            pltpu.make_async_copy(ys_ref.at[pl.ds(pl.multiple_of(s * nb, nb), nb), :],
                                  buf.at[k, pl.ds(pl.multiple_of(t * nb, nb), nb), :], sem).start(priority=k % 2)
        return carry

    lax.fori_loop(0, tm, issue, 0, unroll=2)
    for k in range(TOP_K):
        pltpu.make_async_copy(ys_ref.at[pl.ds(0, tm * nb), :], buf.at[k], sem).wait()
    for c in range(nb):
        cols = slice(c * LANES, (c + 1) * LANES)
        acc = w_ref[:, 0:1] * buf[0, pl.ds(c, tm, stride=nb), :]
        for k in range(1, TOP_K):
            acc = acc + w_ref[:, k:k + 1] * buf[k, pl.ds(c, tm, stride=nb), :]
        o_ref[:, cols] = y_ref[:, cols] + gm_ref[:, cols] * acc


def _combine(rows, slots_flat, ys, w_tok, y, mod, layer):
    d = y.shape[1]
    nb = d // LANES
    gm_arr, gm_spec = rows.mod(mod, layer, 5, d)
    grid_spec = pltpu.PrefetchScalarGridSpec(
        num_scalar_prefetch=1,
        grid=(rows.n_tiles,),
        in_specs=[pl.BlockSpec(memory_space=pl.ANY),
                  pl.BlockSpec((rows.tile, TOP_K), lambda i, s: (i, 0)),
                  pl.BlockSpec((rows.tile, d), lambda i, s: (i, 0)), gm_spec],
        out_specs=pl.BlockSpec((rows.tile, d), lambda i, s: (i, 0)),
        scratch_shapes=[pltpu.VMEM((TOP_K, rows.tile * nb, LANES), F32), pltpu.SemaphoreType.DMA(())],
    )
    return pl.pallas_call(
        functools.partial(_combine_kernel, tm=rows.tile, n_rows=rows.rows, nb=nb),
        out_shape=jax.ShapeDtypeStruct(y.shape, F32),
        grid_spec=grid_spec,
        compiler_params=_params("arbitrary"),
        name="moe_combine",
    )(slots_flat, ys, w_tok, y, gm_arr)


def _rmsnorm_kernel(x_ref, g_ref, o_ref):
    x = x_ref[...]
    o_ref[...] = x * lax.rsqrt(jnp.mean(x * x, axis=-1, keepdims=True) + 1e-6) * g_ref[...]


def _final_norm(rows, x, g):
    d = x.shape[1]
    return pl.pallas_call(
        _rmsnorm_kernel,
        out_shape=jax.ShapeDtypeStruct(x.shape, F32),
        grid=(rows.n_tiles,),
        in_specs=[rows.spec(d), _full(g)],
        out_specs=rows.spec(d),
        compiler_params=_params("arbitrary"),
        name="final_rmsnorm",
    )(x, g)


def _moe_layer(groups, ys, layer, g_ffn, router_w, router_b, wg_all, wu_all, exp_b_up, wd_all, exp_b_down, tmf):
    n_exp = router_w.shape[1]
    d = router_w.shape[0]
    rw_t = router_w.T
    rb_col = router_b.reshape(n_exp, 1)
    cnt = jnp.zeros((n_exp, LANES), F32)
    routed = []
    for (rows, m), y in zip(groups, ys):
        h, idx, w, rank, cnt = _route(rows, y, g_ffn, m, layer, rw_t, rb_col, cnt)
        routed.append((h, idx, w, rank))
    total = sum(rows.rows for rows, _ in groups) * TOP_K
    n_tiles = -(-total // tmf) + n_exp
    counts = cnt[:, 0].astype(I32)
    tiles_e = (counts + tmf - 1) // tmf
    tile_end = jnp.cumsum(tiles_e)
    starts = (tile_end - tiles_e) * tmf
    tile_ids = jnp.arange(n_tiles, dtype=I32)
    tile_expert = jnp.minimum(jnp.sum((tile_end[None, :] <= tile_ids[:, None]).astype(I32), axis=1), n_exp - 1)
    tile_valid = (tile_ids < tile_end[-1]).astype(I32)
    nb = d // LANES
    pad_start = starts + counts
    e_ids = jnp.arange(n_exp, dtype=I32)
    slots = [(jnp.sum(jnp.where(idx[..., None] == e_ids, starts, 0), axis=-1) + rank).reshape(-1)
             for (h, idx, w, rank) in routed]
    xs = _dispatch(groups[0][0], slots[0], slots[1], pad_start, tile_end[-1:], routed[0][0], routed[1][0],
                   n_tiles + 1, tmf, nb)
    bg = exp_b_up[:, None, 0::2]
    bu = exp_b_up[:, None, 1::2]
    y_sorted = _expert_ffn(tile_expert, tile_valid, tile_end[-1:], xs, layer, wg_all, wu_all, bg, bu,
                           wd_all, exp_b_down[:, None, :], tmf, nb)
    outs = []
    for (rows, m), y, sl, (h, idx, w, rank) in zip(groups, ys, slots, routed):
        outs.append(_combine(rows, sl, y_sorted, w.T, y, m, layer))
    return outs


def kernel(x_prompt, x_sample, c_prompt, c_sample, cache_k, cache_v, cache_logf, page_table, state_ret, state_ssm, state_conv, ada_w, ada_b, norm_mix_g, norm_ffn_g, norm_final_g, att_w_in, att_b_f, ret_gn_g, ret_gn_b, att_w_out, ssm_w_in, ssm_conv_w, ssm_conv_b, ssm_dt_bias, ssm_a_log, ssm_d, ssm_norm_g, ssm_w_out, router_w, router_b, exp_w_up, exp_b_up, exp_w_down, exp_b_down):
    bp, sp, d = x_prompt.shape
    bs = x_sample.shape[0]
    depth = ada_w.shape[0]
    n_pages, page = page_table.shape[1], cache_k.shape[2]
    past = n_pages * page
    h_a, dh_a = cache_k.shape[3], cache_k.shape[4]
    h_b, dk_b, dv_b = state_ret.shape[2:]
    h_c, d_state, hd_c = state_ssm.shape[2:]
    d_inner = h_c * hd_c
    conv_w_len, conv_dim = ssm_conv_w.shape[1:]
    gn = (conv_dim - d_inner) // 2
    n_groups = gn // d_state
    assert conv_w_len == 4 and hd_c * 2 == LANES and h_a <= LANES and h_c <= LANES

    rows_p = _Rows(bp, sp, 256)
    rows_s = _Rows(bs, 1, bs)
    yp = x_prompt.reshape(bp * sp, d)
    ys = x_sample.reshape(bs, d)

    mod = _modulation_all(jnp.concatenate([c_prompt, c_sample], axis=0), ada_w, ada_b)
    mod_p, mod_s = mod[:, :bp], mod[:, bp:]

    wa, wr, wvr = h_a * dh_a, h_b * dk_b, h_b * dv_b
    att_segs = []
    off = 0
    for wdt in (wa, wa, wa, wr, wr, wvr, wvr, LANES):
        att_segs.append((off, wdt))
        off += wdt
    ssm_segs = ((0, d_inner), (d_inner, conv_dim), (d_inner + conv_dim, LANES))

    cos_p, sin_p = _rope_tables(jnp.arange(sp), h_b, dk_b)
    cos_s, sin_s = _rope_tables(jnp.full((1,), past), h_b, dk_b)
    log_gammas = [math.log1p(-2.0 ** (-RET_DECAY_BASE - h)) for h in range(h_b)]
    gamma_col = jnp.broadcast_to(jnp.asarray(np.exp(log_gammas), F32).reshape(1, h_b, 1, 1), (bs, h_b, 1, 1))
    kt_pages = jnp.transpose(cache_k, (0, 1, 3, 4, 2))
    vt_pages = jnp.transpose(cache_v, (0, 1, 3, 4, 2))
    lf_pages_t = jnp.swapaxes(cache_logf, 2, 3)
    n_exp = router_w.shape[2]
    tmf = 512 if (bp * sp + bs) * TOP_K >= 512 * n_exp else 64
    e_mat = (jnp.arange(LANES)[:, None] == (jnp.arange(d_inner) // hd_c)[None, :]).astype(BF16)
    wg_all, wu_all = _split_gate_up(exp_w_up)

    k_p, v_p, f_p, k_s, v_s, f_s, ret_p, ret_s = [], [], [], [], [], [], [], []
    ssm_p, ssm_s, conv_p, conv_s = [], [], [], []
    for l in range(depth):
        j = l // 2
        g_mix = norm_mix_g[l][None]
        if l % 2 == 0:
            w = att_w_in[j]
            o0 = 3 * wa
            w_perm32 = jnp.concatenate(
                [w[:, :o0], w[:, o0 + h_a:], w[:, o0:o0 + h_a], jnp.zeros((d, LANES - h_a), F32)], axis=1)
            w_perm = w_perm32.astype(BF16)
            b_f_pad = jnp.pad(att_b_f[j], (0, LANES - h_a))[None]
            w1_32, w2_32 = att_w_out[j][:wa], att_w_out[j][wa:]
            w1, w2 = w1_32.astype(BF16), w2_32.astype(BF16)
            gn_g, gn_b = ret_gn_g[j][None], ret_gn_b[j][None]
            qa, ka, va, qr, kr, vr, gate, fa = _project(rows_p, yp, g_mix, mod_p, l, w_perm, att_segs)
            logf, fcum = _forget_gates(rows_p, fa, b_f_pad)
            hp8 = -(-h_a // 8) * 8
            fk_t = jnp.transpose(fcum.reshape(bp, sp, LANES)[:, :, :hp8], (0, 2, 1))
            o_fox = _fox_prompt(qa, ka, va, fcum, fk_t, bp, sp, dh_a, h_a)
            o_ret, st = _retention_prompt(qr, kr, vr, cos_p, sin_p, bp, sp, h_b, dk_b, dv_b)
            yp = _att_output(rows_p, o_fox, o_ret, gate, yp, mod_p, l, gn_g, gn_b, w1, w2, h_b, dv_b)
            k_p.append(ka.reshape(bp, sp, h_a, dh_a))
            v_p.append(va.reshape(bp, sp, h_a, dh_a))
            f_p.append(logf[:, :h_a].reshape(bp, sp, h_a))
            ret_p.append(st)
            qa, ka, va, qr, kr, vr, gate, fa = _project_f32(rows_s, ys, g_mix, mod_s, l, w_perm32, att_segs)
            logf, _ = _forget_gates(rows_s, fa, b_f_pad)
            o_fox = _fox_decode(page_table, qa[:, None, :], ka[:, None, :], va[:, None, :], logf[:, None, :],
                                kt_pages, vt_pages, lf_pages_t, j).reshape(bs, wa)
            qr2, kr2 = _rope_rows(qr, kr, cos_s, sin_s, dk_b)
            st, y_ret = _state_step(state_ret[j], gamma_col, kr2.reshape(bs, h_b, dk_b, 1),
                                    qr2.reshape(bs, h_b, dk_b, 1), vr.reshape(bs, h_b, 1, dv_b))
            ys = _att_output(rows_s, o_fox, y_ret.reshape(bs, wvr), gate, ys, mod_s, l, gn_g, gn_b, w1_32, w2_32,
                             h_b, dv_b)
            k_s.append(ka.reshape(bs, 1, h_a, dh_a))
            v_s.append(va.reshape(bs, 1, h_a, dh_a))
            f_s.append(logf[:, :h_a].reshape(bs, 1, h_a))
            ret_s.append(st)
        else:
            w_pad32 = jnp.pad(ssm_w_in[j], ((0, 0), (0, LANES - h_c)))
            w_pad = w_pad32.astype(BF16)
            cw, cb = ssm_conv_w[j], ssm_conv_b[j][None]
            dtb = jnp.pad(ssm_dt_bias[j], (0, LANES - h_c))[None]
            alog = jnp.pad(ssm_a_log[j], (0, LANES - h_c))[None]
            dsk = jnp.repeat(ssm_d[j], hd_c)[None]
            ng = ssm_norm_g[j][None]
            w_out = ssm_w_out[j].astype(BF16)
            z, xr, dtr = _project(rows_p, yp, g_mix, mod_p, l, w_pad, ssm_segs)
            y_n, st2, cv = _ssd_prompt(xr, z, dtr, cw, cb, dtb, alog, dsk, ng, e_mat, bp, sp,
                                       d_inner, n_groups, d_state, hd_c)
            yp = _matmul_residual(rows_p, y_n, yp, mod_p, l, w_out)
            st = st2.reshape(bp, h_c // 2, d_state, 2, hd_c).transpose(0, 1, 3, 2, 4).reshape(bp, h_c, d_state, hd_c)
            ssm_p.append(st)
            conv_p.append(cv[:, 8 - (conv_w_len - 1):, :])
            z, xr, dtr = _project_f32(rows_s, ys, g_mix, mod_s, l, w_pad32, ssm_segs)
            taps = [state_conv[j][:, i, :] for i in range(conv_w_len - 1)]
            x, bm, cm, v, a = _ssd_step_prep(xr, taps, dtr, cw, cb, dtb, alog, e_mat, d_inner, gn)
            st, y_s = _state_step(state_ssm[j], a[:, :h_c].reshape(bs, h_c, 1, 1),
                                  bm.reshape(bs, n_groups, d_state, 1), cm.reshape(bs, n_groups, d_state, 1),
                                  v.reshape(bs, h_c, 1, hd_c))
            y_n = _ssd_step_post(y_s.reshape(bs, d_inner), x, z, dsk, ng, n_groups)
            ys = _matmul_residual(rows_s, y_n, ys, mod_s, l, ssm_w_out[j])
            ssm_s.append(st)
            conv_s.append(jnp.concatenate([state_conv[j][:, 1:, :], xr[:, None, :]], axis=1))
        yp, ys = _moe_layer([(rows_p, mod_p), (rows_s, mod_s)], [yp, ys], l, norm_ffn_g[l][None],
                            router_w[l], router_b[l], wg_all, wu_all, exp_b_up[l], exp_w_down, exp_b_down[l], tmf)
    y_prompt = _final_norm(rows_p, yp, norm_final_g[None]).reshape(bp, sp, d)
    y_sample = _final_norm(rows_s, ys, norm_final_g[None]).reshape(bs, 1, d)
    return (y_prompt, y_sample,
            jnp.stack(k_p), jnp.stack(v_p), jnp.stack(f_p),
            jnp.stack(k_s), jnp.stack(v_s), jnp.stack(f_s),
            jnp.stack(ret_p), jnp.stack(ret_s),
            jnp.stack(ssm_p), jnp.stack(ssm_s),
            jnp.stack(conv_p), jnp.stack(conv_s))
```

```python
import functools
import math

import jax
import jax.numpy as jnp
import numpy as np
from jax import lax
from jax.experimental import pallas as pl
from jax.experimental.pallas import tpu as pltpu

F32 = jnp.float32
BF16 = jnp.bfloat16
I32 = jnp.int32

LANES = 128
VMEM_LIMIT = 56 * 1024 * 1024
CHUNK = 128
F32_COL_BLOCK = 512
TOP_K = 4
RET_DECAY_BASE = 5.0
ROPE_THETA = 10000.0
SWIGLU_LIMIT = 7.0
SWIGLU_ALPHA = 1.702
NEG_INF = float("-inf")


def _params(*sem):
    return pltpu.CompilerParams(dimension_semantics=sem, vmem_limit_bytes=VMEM_LIMIT)


def _dot(a, b):
    return jnp.dot(a, b, preferred_element_type=F32)


def _dot_nt(a, b):
    return lax.dot_general(a, b, (((1,), (1,)), ((), ())), preferred_element_type=F32)


def _dot_tn(a, b):
    return lax.dot_general(a, b, (((0,), (0,)), ((), ())), preferred_element_type=F32)


def _split3(x):
    hi = x.astype(BF16)
    r = x - hi.astype(F32)
    mid = r.astype(BF16)
    lo = (r - mid.astype(F32)).astype(BF16)
    return hi, mid, lo


def _dot_sel_rhs(x, m):
    hi, mid, lo = _split3(x)
    return _dot(hi, m) + _dot(mid, m) + _dot(lo, m)


def _dot_sel_lhs(m, x):
    hi, mid, lo = _split3(x)
    return _dot(m, hi) + _dot(m, mid) + _dot(m, lo)


def _dot3(a, b, dot=_dot):
    a_hi = a.astype(BF16)
    a_lo = (a - a_hi.astype(F32)).astype(BF16)
    b_hi = b.astype(BF16)
    b_lo = (b - b_hi.astype(F32)).astype(BF16)
    return dot(a_hi, b_hi) + dot(a_hi, b_lo) + dot(a_lo, b_hi)


def _dot_w(a, w_ref):
    w = w_ref[...]
    return _dot3(a, w) if w.dtype == F32 else _dot(a.astype(BF16), w)


def _sigmoid(x):
    return 1.0 / (1.0 + jnp.exp(-x))


def _silu(x):
    return x * _sigmoid(x)


def _softplus(x):
    return jnp.maximum(x, 0.0) + jnp.log(1.0 + jnp.exp(-jnp.abs(x)))


def _log_sigmoid(x):
    return -_softplus(-x)


def _modulate(x, g, shift, scale):
    ms = jnp.mean(x * x, axis=-1, keepdims=True)
    return (x * lax.rsqrt(ms + 1e-6)) * g * (1.0 + scale) + shift


def _iota2(shape, dim):
    return lax.broadcasted_iota(I32, shape, dim)


class _Rows:
    def __init__(self, n_batch, rows_per_batch, tile):
        self.n_batch = n_batch
        self.rows_per_batch = rows_per_batch
        self.rows = n_batch * rows_per_batch
        self.per_row_mod = rows_per_batch == 1
        self.tile = self.rows if self.per_row_mod else min(tile, rows_per_batch)
        assert self.rows % self.tile == 0 and (self.per_row_mod or rows_per_batch % self.tile == 0)
        self.n_tiles = self.rows // self.tile
        self.tiles_per_batch = 1 if self.per_row_mod else rows_per_batch // self.tile

    def spec(self, width, col=0):
        return pl.BlockSpec((self.tile, width), lambda i, *_: (i, col))

    def mod(self, mod_arr, layer, chunk, d):
        if self.per_row_mod:
            return mod_arr, pl.BlockSpec((None, self.rows, d), lambda i, *_: (layer, 0, chunk))
        tpb = self.tiles_per_batch
        arr = mod_arr.reshape(mod_arr.shape[0], mod_arr.shape[1], 1, mod_arr.shape[2])
        return arr, pl.BlockSpec((None, None, 1, d), lambda i, *_: (layer, i // tpb, 0, chunk))


def _full(arr):
    nd = arr.ndim
    return pl.BlockSpec(arr.shape, lambda *_: (0,) * nd)


def _mod_kernel(c_ref, w_ref, b_ref, o_ref):
    c = c_ref[...]
    o_ref[...] = _dot3(_silu(c), w_ref[...]) + b_ref[...]


def _modulation_all(c_all, ada_w, ada_b):
    n_layers, d, n6 = ada_w.shape
    rows = c_all.shape[0]
    tn = n6 // 8 if n6 % (8 * LANES) == 0 else n6
    return pl.pallas_call(
        _mod_kernel,
        out_shape=jax.ShapeDtypeStruct((n_layers, rows, n6), F32),
        grid=(n_layers, n6 // tn),
        in_specs=[pl.BlockSpec((rows, d), lambda l, j: (0, 0)),
                  pl.BlockSpec((None, d, tn), lambda l, j: (l, 0, j)),
                  pl.BlockSpec((None, 1, tn), lambda l, j: (l, 0, j))],
        out_specs=pl.BlockSpec((None, rows, tn), lambda l, j: (l, 0, j)),
        compiler_params=_params("arbitrary", "arbitrary"),
        name="adaln_modulation",
    )(c_all, ada_w, ada_b.reshape(n_layers, 1, n6))


def _proj_kernel(x_ref, g_ref, sh_ref, sc_ref, w_ref, *o_refs, segs):
    h = _modulate(x_ref[...], g_ref[...], sh_ref[...], sc_ref[...]).astype(BF16)
    for o_ref, (start, width) in zip(o_refs, segs):
        o_ref[...] = _dot(h, w_ref[:, start:start + width])


def _project(rows, x, g, mod, layer, w_bf16, segs):
    d = x.shape[1]
    sh_arr, sh_spec = rows.mod(mod, layer, 0, d)
    sc_arr, sc_spec = rows.mod(mod, layer, 1, d)
    return pl.pallas_call(
        functools.partial(_proj_kernel, segs=segs),
        out_shape=[jax.ShapeDtypeStruct((rows.rows, wd), F32) for _, wd in segs],
        grid=(rows.n_tiles,),
        in_specs=[rows.spec(d), _full(g), sh_spec, sc_spec, _full(w_bf16)],
        out_specs=[rows.spec(wd) for _, wd in segs],
        compiler_params=_params("arbitrary"),
        name="norm_mod_project",
    )(x, g, sh_arr, sc_arr, w_bf16)


def _proj_f32_kernel(x_ref, g_ref, sh_ref, sc_ref, w_ref, o_ref):
    o_ref[...] = _dot3(_modulate(x_ref[...], g_ref[...], sh_ref[...], sc_ref[...]), w_ref[...])


def _project_f32(rows, x, g, mod, layer, w_f32, segs):
    d, n_pad = w_f32.shape
    cb = F32_COL_BLOCK
    assert n_pad % cb == 0
    w_pad = w_f32
    sh_arr, sh_spec = rows.mod(mod, layer, 0, d)
    sc_arr, sc_spec = rows.mod(mod, layer, 1, d)
    assert rows.n_tiles == 1
    out = pl.pallas_call(
        _proj_f32_kernel,
        out_shape=jax.ShapeDtypeStruct((rows.rows, n_pad), F32),
        grid=(n_pad // cb,),
        in_specs=[pl.BlockSpec((rows.rows, d), lambda j: (0, 0)), _full(g),
                  pl.BlockSpec(sh_spec.block_shape, lambda j: sh_spec.index_map(0)),
                  pl.BlockSpec(sc_spec.block_shape, lambda j: sc_spec.index_map(0)),
                  pl.BlockSpec((d, cb), lambda j: (0, j))],
        out_specs=pl.BlockSpec((rows.rows, cb), lambda j: (0, j)),
        compiler_params=_params("arbitrary"),
        name="norm_mod_project_f32",
    )(x, g, sh_arr, sc_arr, w_pad)
    return [out[:, s:s + wd] for s, wd in segs]


def _logf_kernel(fa_ref, bf_ref, lf_ref, fc_ref, carry_ref, *, tiles_per_batch):
    i = pl.program_id(0)
    lf = _log_sigmoid(fa_ref[...] + bf_ref[...])
    lf_ref[...] = lf

    @pl.when(i % tiles_per_batch == 0)
    def _():
        carry_ref[...] = jnp.zeros_like(carry_ref)

    tm = lf.shape[0]
    tri = (_iota2((tm, tm), 1) <= _iota2((tm, tm), 0)).astype(BF16)
    cs = _dot_sel_lhs(tri, lf) + carry_ref[...]
    fc_ref[...] = cs
    carry_ref[...] = cs[tm - 1:tm, :]


def _forget_gates(rows, fa_raw, b_f_pad):
    return pl.pallas_call(
        functools.partial(_logf_kernel, tiles_per_batch=rows.tiles_per_batch),
        out_shape=[jax.ShapeDtypeStruct((rows.rows, LANES), F32)] * 2,
        grid=(rows.n_tiles,),
        in_specs=[rows.spec(LANES), _full(b_f_pad)],
        out_specs=[rows.spec(LANES)] * 2,
        scratch_shapes=[pltpu.VMEM((1, LANES), F32)],
        compiler_params=_params("arbitrary"),
        name="forget_gates",
    )(fa_raw, b_f_pad)


def _fox_kernel(q_ref, k_ref, v_ref, fc_ref, ft_ref, o_ref, fkb_ref, qm_ref, acc_ref, *, tq, dh, n_blocks, n_heads):
    qi = pl.program_id(1)
    seq = k_ref.shape[0]
    hpb = LANES // dh

    @pl.when(qi == 0)
    def _():
        for h in range(n_heads):
            fkb_ref[h] = jnp.broadcast_to(fc_ref[:, h:h + 1], (seq, LANES))

    q0 = pl.multiple_of(qi * tq, tq)
    lane = _iota2((1, LANES), 1)
    row_head = _iota2((LANES, 1), 0) // dh
    on_or_before = _iota2((tq, tq), 0) <= _iota2((tq, tq), 1)
    for p in range(n_blocks):
        q2 = q_ref[:, p * LANES:(p + 1) * LANES] * (dh ** -0.5)
        for j in range(hpb):
            qm_ref[p * hpb + j] = jnp.where(lane // dh == j, q2, 0.0).astype(BF16)
    acc_ref[...] = jnp.zeros_like(acc_ref)

    def block(k0, carry, diagonal):
        ms, ls = carry
        ss = []
        for p in range(n_blocks):
            k2 = k_ref[pl.ds(k0, tq), p * LANES:(p + 1) * LANES].astype(BF16)
            ss += [_dot_nt(k2, qm_ref[p * hpb + j]) for j in range(hpb)]
        new_ms, new_ls, alphas, pes = [], [], [], []
        for h in range(n_heads):
            fk = fkb_ref[h, pl.ds(k0, tq), :]
            s = ss[h] + (ft_ref[h:h + 1, pl.ds(q0, tq)] - jnp.concatenate([fk] * (tq // LANES), axis=1))
            if diagonal:
                s = jnp.where(on_or_before, s, NEG_INF)
            m_new = jnp.maximum(ms[h], jnp.max(s, axis=0, keepdims=True))
            alphas.append(jnp.exp(ms[h] - m_new))
            pe = jnp.exp(s - m_new)
            new_ms.append(m_new)
            new_ls.append(alphas[h] * ls[h] + jnp.sum(pe, axis=0, keepdims=True))
            pes.append(pe.astype(BF16))
        for p in range(n_blocks):
            v2 = v_ref[pl.ds(k0, tq), p * LANES:(p + 1) * LANES].astype(BF16)
            alpha2 = None
            pv2 = None
            for j in range(hpb):
                h = p * hpb + j
                pv = _dot_tn(v2, pes[h])
                sel = row_head == j
                alpha2 = jnp.where(sel, alphas[h], 0.0) if alpha2 is None else jnp.where(sel, alphas[h], alpha2)
                pv2 = jnp.where(sel, pv, 0.0) if pv2 is None else jnp.where(sel, pv, pv2)
            acc_ref[p] = alpha2 * acc_ref[p] + pv2
        return tuple(new_ms), tuple(new_ls)

    init = (tuple(jnp.full((1, tq), NEG_INF, F32) for _ in range(n_heads)),
            tuple(jnp.zeros((1, tq), F32) for _ in range(n_heads)))
    carry = lax.fori_loop(0, qi, lambda kv, c: block(pl.multiple_of(kv * tq, tq), c, False), init)
    ms, ls = block(q0, carry, True)
    for p in range(n_blocks):
        l2 = None
        for j in range(hpb):
            sel = row_head == j
            l2 = jnp.where(sel, ls[p * hpb + j], 1.0) if l2 is None else jnp.where(sel, ls[p * hpb + j], l2)
        o_ref[:, p * LANES:(p + 1) * LANES] = (acc_ref[p] / l2).T


def _fox_prompt(q, k, v, fcum, fcum_t, n_batch, seq, dh, n_heads):
    width = q.shape[1]
    assert width % LANES == 0 and LANES % dh == 0
    tq = min(256, seq)
    nq = seq // tq
    hp = fcum_t.shape[1]
    return pl.pallas_call(
        functools.partial(_fox_kernel, tq=tq, dh=dh, n_blocks=width // LANES, n_heads=n_heads),
        out_shape=jax.ShapeDtypeStruct(q.shape, F32),
        grid=(n_batch, nq),
        in_specs=[pl.BlockSpec((tq, width), lambda b, i: (b * nq + i, 0)),
                  pl.BlockSpec((seq, width), lambda b, i: (b, 0)),
                  pl.BlockSpec((seq, width), lambda b, i: (b, 0)),
                  pl.BlockSpec((seq, LANES), lambda b, i: (b, 0)),
                  pl.BlockSpec((None, hp, seq), lambda b, i: (b, 0, 0))],
        out_specs=pl.BlockSpec((tq, width), lambda b, i: (b * nq + i, 0)),
        scratch_shapes=[pltpu.VMEM((n_heads, seq, LANES), F32), pltpu.VMEM((n_heads, tq, LANES), BF16),
                        pltpu.VMEM((width // LANES, LANES, tq), F32)],
        compiler_params=_params("arbitrary", "arbitrary"),
        name="fox_prompt_attention",
    )(q, k, v, fcum, fcum_t)


def _rope_tables(pos, n_heads, dk):
    half = dk // 2
    freq = ROPE_THETA ** (-jnp.arange(half, dtype=F32) / half)
    ang = pos.astype(F32)[:, None] * freq[None, :]
    cos = jnp.cos(ang)
    sin = jnp.sin(ang)
    cos_h = jnp.concatenate([cos, cos], axis=-1)
    sin_h = jnp.concatenate([-sin, sin], axis=-1)
    return jnp.tile(cos_h, (1, n_heads)), jnp.tile(sin_h, (1, n_heads))


def _rope(x, cos, sin, dk):
    half = dk // 2
    lane = _iota2((1, LANES), 1)
    up = pltpu.roll(x, LANES - half, 1)
    down = pltpu.roll(x, half, 1)
    partner = jnp.where((lane % dk) < half, up, down)
    return x * cos + partner * sin


def _ret_kernel(q_ref, k_ref, v_ref, cos_ref, sin_ref, o_ref, st_ref, state, *, n_heads, dk, dv, log_gammas):
    c = pl.program_id(1)
    hpb = LANES // dk

    @pl.when(c == 0)
    def _():
        state[...] = jnp.zeros_like(state)

    lane = _iota2((1, LANES), 1)
    t_col = _iota2((CHUNK, 1), 0).astype(F32)
    seg = (_iota2((CHUNK, CHUNK), 0) - _iota2((CHUNK, CHUNK), 1)).astype(F32)
    causal = seg >= 0.0
    row_head = _iota2((LANES, 1), 0) // dk
    for p in range(n_heads // hpb):
        cols = slice(p * LANES, (p + 1) * LANES)
        cos = cos_ref[:, cols]
        sin = sin_ref[:, cols]
        q2 = _rope(q_ref[:, cols], cos, sin, dk)
        k2 = _rope(k_ref[:, cols], cos, sin, dk) * (dk ** -0.5)
        k2b = k2.astype(BF16)
        st2 = state[p]
        st2b = st2.astype(BF16)
        upd = None
        row_decay = None
        for j in range(hpb):
            h = p * hpb + j
            lg = log_gammas[h]
            sel = lane // dk == j
            qh = jnp.where(sel, q2, 0.0).astype(BF16)
            s = _dot_nt(qh, k2b)
            decay = jnp.exp(jnp.where(causal, seg * lg, NEG_INF))
            vh = v_ref[:, h * dv:(h + 1) * dv]
            y = _dot((s * decay).astype(BF16), vh.astype(BF16))
            y = y + jnp.exp((t_col + 1.0) * lg) * _dot(qh, st2b)
            o_ref[:, h * dv:(h + 1) * dv] = y
            tail = jnp.exp((CHUNK - 1.0 - t_col) * lg)
            kh = jnp.where(sel, k2, 0.0).astype(BF16)
            u = _dot_tn(kh, (vh * tail).astype(BF16))
            upd = u if upd is None else upd + u
            rd = jnp.where(row_head == j, math.exp(CHUNK * lg), 0.0)
            row_decay = rd if row_decay is None else row_decay + rd
        state[p] = row_decay * st2 + upd

    @pl.when(c == pl.num_programs(1) - 1)
    def _():
        for h in range(n_heads):
            p, j = divmod(h, hpb)
            st_ref[h] = state[p, j * dk:(j + 1) * dk, :]


def _retention_prompt(qr, kr, vr, cos, sin, n_batch, seq, n_heads, dk, dv):
    nc = seq // CHUNK
    log_gammas = tuple(math.log1p(-2.0 ** (-RET_DECAY_BASE - h)) for h in range(n_heads))
    wq = n_heads * dk
    wv = n_heads * dv
    assert wq % LANES == 0 and LANES % dk == 0 and dv % LANES == 0
    return pl.pallas_call(
        functools.partial(_ret_kernel, n_heads=n_heads, dk=dk, dv=dv, log_gammas=log_gammas),
        out_shape=[jax.ShapeDtypeStruct((n_batch * seq, wv), F32),
                   jax.ShapeDtypeStruct((n_batch, n_heads, dk, dv), F32)],
        grid=(n_batch, nc),
        in_specs=[pl.BlockSpec((CHUNK, wq), lambda b, c: (b * nc + c, 0)),
                  pl.BlockSpec((CHUNK, wq), lambda b, c: (b * nc + c, 0)),
                  pl.BlockSpec((CHUNK, wv), lambda b, c: (b * nc + c, 0)),
                  pl.BlockSpec((CHUNK, wq), lambda b, c: (c, 0)),
                  pl.BlockSpec((CHUNK, wq), lambda b, c: (c, 0))],
        out_specs=[pl.BlockSpec((CHUNK, wv), lambda b, c: (b * nc + c, 0)),
                   pl.BlockSpec((None, n_heads, dk, dv), lambda b, c: (b, 0, 0, 0))],
        scratch_shapes=[pltpu.VMEM((wq // LANES, LANES, dv), F32)],
        compiler_params=_params("arbitrary", "arbitrary"),
        name="retention_prompt_scan",
    )(qr, kr, vr, cos, sin)


def _att_out_kernel(of_ref, or_ref, gt_ref, y_ref, gm_ref, gg_ref, gb_ref, w1_ref, w2_ref, o_ref, *, n_heads, dv):
    parts = []
    for h in range(n_heads):
        r = or_ref[:, h * dv:(h + 1) * dv]
        mu = jnp.mean(r, axis=-1, keepdims=True)
        d = r - mu
        var = jnp.mean(d * d, axis=-1, keepdims=True)
        parts.append(d * lax.rsqrt(var + 1e-5))
    r = jnp.concatenate(parts, axis=1) * gg_ref[...] + gb_ref[...]
    r = r * _silu(gt_ref[...])
    o = _dot_w(of_ref[...], w1_ref) + _dot_w(r, w2_ref)
    o_ref[...] = y_ref[...] + gm_ref[...] * o


def _att_output(rows, o_fox, o_ret, gate, y, mod, layer, gn_g, gn_b, w1, w2, n_heads, dv):
    d = y.shape[1]
    gm_arr, gm_spec = rows.mod(mod, layer, 2, d)
    return pl.pallas_call(
        functools.partial(_att_out_kernel, n_heads=n_heads, dv=dv),
        out_shape=jax.ShapeDtypeStruct(y.shape, F32),
        grid=(rows.n_tiles,),
        in_specs=[rows.spec(o_fox.shape[1]), rows.spec(o_ret.shape[1]), rows.spec(gate.shape[1]), rows.spec(d),
                  gm_spec, _full(gn_g), _full(gn_b), _full(w1), _full(w2)],
        out_specs=rows.spec(d),
        compiler_params=_params("arbitrary"),
        name="attention_output",
    )(o_fox, o_ret, gate, y, gm_arr, gn_g, gn_b, w1, w2)


def _mm_res_kernel(a_ref, y_ref, gm_ref, w_ref, o_ref):
    o_ref[...] = y_ref[...] + gm_ref[...] * _dot_w(a_ref[...], w_ref)


def _matmul_residual(rows, a, y, mod, layer, w):
    d = y.shape[1]
    gm_arr, gm_spec = rows.mod(mod, layer, 2, d)
    return pl.pallas_call(
        _mm_res_kernel,
        out_shape=jax.ShapeDtypeStruct(y.shape, F32),
        grid=(rows.n_tiles,),
        in_specs=[rows.spec(a.shape[1]), rows.spec(d), gm_spec, _full(w)],
        out_specs=rows.spec(d),
        compiler_params=_params("arbitrary"),
        name="matmul_gated_residual",
    )(a, y, gm_arr, w)


def _ssd_conv_act(taps, cw_ref, cb_ref):
    conv = cb_ref[...] + cw_ref[0:1, :] * taps[0]
    for i in range(1, len(taps)):
        conv = conv + cw_ref[i:i + 1, :] * taps[i]
    return _silu(conv)


def _ssd_gate_norm(y, x, z, dsk, ng, n_groups):
    y = (y + dsk * x) * _silu(z)
    gw = y.shape[1] // n_groups
    parts = []
    for g in range(n_groups):
        blk = y[:, g * gw:(g + 1) * gw]
        ms = jnp.mean(blk * blk, axis=-1, keepdims=True)
        parts.append(blk * lax.rsqrt(ms + 1e-5))
    return jnp.concatenate(parts, axis=1) * ng


def _ssd_kernel(xr_ref, z_ref, dt_ref, cw_ref, cb_ref, dtb_ref, alog_ref, dsk_ref, ng_ref, e_ref,
                y_ref, st_ref, cv_ref, prev, state, *, d_inner, n_groups, d_state, hd):
    c = pl.program_id(1)
    hpg = d_inner // hd // n_groups
    ppg = hpg * hd // LANES

    @pl.when(c == 0)
    def _():
        prev[...] = jnp.zeros_like(prev)
        state[...] = jnp.zeros_like(state)

    cur = xr_ref[...]
    xw = jnp.concatenate([prev[...], cur], axis=0)
    taps = [xw[5 + i:5 + i + CHUNK] for i in range(3)] + [cur]
    tail_rows = cur[CHUNK - 8:CHUNK]
    prev[...] = tail_rows
    cv_ref[...] = tail_rows
    xbc = _ssd_conv_act(taps, cw_ref, cb_ref)
    gn = n_groups * d_state
    x = xbc[:, :d_inner]
    bm = xbc[:, d_inner:d_inner + gn].astype(BF16)
    cm = xbc[:, d_inner + gn:].astype(BF16)

    dt = _softplus(dt_ref[...] + dtb_ref[...])
    la = -dt * jnp.exp(alog_ref[...])
    seg = _iota2((CHUNK, CHUNK), 0) - _iota2((CHUNK, CHUNK), 1)
    causal = seg >= 0
    cum = _dot_sel_lhs(causal.astype(BF16), la)
    cum_t = cum.T
    e = e_ref[...]
    dt_e = _dot_sel_rhs(dt, e)
    cum_e = _dot_sel_rhs(cum, e)
    ecum_e = jnp.exp(cum_e)
    tail_e = jnp.exp(cum_e[CHUNK - 1:CHUNK, :] - cum_e)
    v = x * dt_e
    vb = v.astype(BF16)
    vtb = (v * tail_e).astype(BF16)
    lane = _iota2((1, LANES), 1)
    hpb = LANES // hd
    ys = []
    for g in range(n_groups):
        cmg = cm[:, g * d_state:(g + 1) * d_state]
        bmg = bm[:, g * d_state:(g + 1) * d_state]
        s = _dot_nt(cmg, bmg)
        for pp in range(ppg):
            p = g * ppg + pp
            cols = slice(p * LANES, (p + 1) * LANES)
            v2 = vb[:, cols]
            yp = None
            for j in range(hpb):
                h = p * hpb + j
                d = jnp.exp(jnp.where(causal, cum[:, h:h + 1] - cum_t[h:h + 1, :], NEG_INF))
                yj = _dot((s * d).astype(BF16), v2)
                sel = lane // hd == j
                yp = jnp.where(sel, yj, 0.0) if yp is None else jnp.where(sel, yj, yp)
            st2 = state[p]
            yp = yp + ecum_e[:, cols] * _dot(cmg, st2.astype(BF16))
            ys.append(yp)
            state[p] = ecum_e[CHUNK - 1:CHUNK, cols] * st2 + _dot_tn(bmg, vtb[:, cols])
    y = jnp.concatenate(ys, axis=1)
    y_ref[...] = _ssd_gate_norm(y, x, z_ref[...], dsk_ref[...], ng_ref[...], n_groups)

    @pl.when(c == pl.num_programs(1) - 1)
    def _():
        st_ref[...] = state[...]


def _ssd_prompt(xr, z, dtr, cw, cb, dtb, alog, dsk, ng, e_mat, n_batch, seq, d_inner, n_groups, d_state, hd):
    nc = seq // CHUNK
    cd = xr.shape[1]
    n_blk = d_inner // LANES
    row = lambda b, c: (b * nc + c, 0)
    return pl.pallas_call(
        functools.partial(_ssd_kernel, d_inner=d_inner, n_groups=n_groups, d_state=d_state, hd=hd),
        out_shape=[jax.ShapeDtypeStruct((n_batch * seq, d_inner), F32),
                   jax.ShapeDtypeStruct((n_batch, n_blk, d_state, LANES), F32),
                   jax.ShapeDtypeStruct((n_batch, 8, cd), F32)],
        grid=(n_batch, nc),
        in_specs=[pl.BlockSpec((CHUNK, cd), row), pl.BlockSpec((CHUNK, d_inner), row),
                  pl.BlockSpec((CHUNK, LANES), row),
                  _full(cw), _full(cb), _full(dtb), _full(alog), _full(dsk), _full(ng), _full(e_mat)],
        out_specs=[pl.BlockSpec((CHUNK, d_inner), row),
                   pl.BlockSpec((None, n_blk, d_state, LANES), lambda b, c: (b, 0, 0, 0)),
                   pl.BlockSpec((None, 8, cd), lambda b, c: (b, 0, 0))],
        scratch_shapes=[pltpu.VMEM((8, cd), F32), pltpu.VMEM((n_blk, d_state, LANES), F32)],
        compiler_params=_params("arbitrary", "arbitrary"),
        name="ssd_prompt_scan",
    )(xr, z, dtr, cw, cb, dtb, alog, dsk, ng, e_mat)


def _ssd_step_prep_kernel(xr_ref, c0_ref, c1_ref, c2_ref, dt_ref, cw_ref, cb_ref, dtb_ref, alog_ref, e_ref,
                          x_ref, bm_ref, cm_ref, v_ref, a_ref, *, d_inner, gn):
    xbc = _ssd_conv_act([c0_ref[...], c1_ref[...], c2_ref[...], xr_ref[...]], cw_ref, cb_ref)
    x = xbc[:, :d_inner]
    x_ref[...] = x
    bm_ref[...] = xbc[:, d_inner:d_inner + gn]
    cm_ref[...] = xbc[:, d_inner + gn:]
    dt = _softplus(dt_ref[...] + dtb_ref[...])
    a_ref[...] = jnp.exp(-dt * jnp.exp(alog_ref[...]))
    v_ref[...] = x * _dot_sel_rhs(dt, e_ref[...])


def _ssd_step_prep(xr, taps, dtr, cw, cb, dtb, alog, e_mat, d_inner, gn):
    n = xr.shape[0]
    args = (xr, *taps, dtr, cw, cb, dtb, alog, e_mat)
    return pl.pallas_call(
        functools.partial(_ssd_step_prep_kernel, d_inner=d_inner, gn=gn),
        out_shape=[jax.ShapeDtypeStruct((n, d_inner), F32), jax.ShapeDtypeStruct((n, gn), F32),
                   jax.ShapeDtypeStruct((n, gn), F32), jax.ShapeDtypeStruct((n, d_inner), F32),
                   jax.ShapeDtypeStruct((n, LANES), F32)],
        grid=(1,),
        in_specs=[_full(a) for a in args],
        out_specs=[pl.BlockSpec((n, d_inner), lambda i: (0, 0)), pl.BlockSpec((n, gn), lambda i: (0, 0)),
                   pl.BlockSpec((n, gn), lambda i: (0, 0)), pl.BlockSpec((n, d_inner), lambda i: (0, 0)),
                   pl.BlockSpec((n, LANES), lambda i: (0, 0))],
        compiler_params=_params("arbitrary"),
        name="ssd_step_prep",
    )(*args)


def _ssd_step_post_kernel(y_ref, x_ref, z_ref, dsk_ref, ng_ref, o_ref, *, n_groups):
    o_ref[...] = _ssd_gate_norm(y_ref[...], x_ref[...], z_ref[...], dsk_ref[...], ng_ref[...], n_groups)


def _ssd_step_post(y, x, z, dsk, ng, n_groups):
    args = (y, x, z, dsk, ng)
    return pl.pallas_call(
        functools.partial(_ssd_step_post_kernel, n_groups=n_groups),
        out_shape=jax.ShapeDtypeStruct(y.shape, F32),
        grid=(1,),
        in_specs=[_full(a) for a in args],
        out_specs=pl.BlockSpec(y.shape, lambda i: (0, 0)),
        compiler_params=_params("arbitrary"),
        name="ssd_step_post",
    )(*args)


def _state_step_kernel(s_ref, a_ref, k_ref, q_ref, v_ref, so_ref, y_ref, *, n_heads, heads_per_key):
    for h in range(n_heads):
        g = h // heads_per_key
        new = a_ref[h] * s_ref[h] + k_ref[g] * v_ref[h]
        so_ref[h] = new
        y_ref[h] = jnp.sum(q_ref[g] * new, axis=0, keepdims=True)


def _state_step(state, a, k_col, q_col, v_row):
    n_b, n_h, n_n, n_v = state.shape
    n_g = k_col.shape[1]
    blk = lambda arr: pl.BlockSpec((None,) + arr.shape[1:], lambda b: (b, 0, 0, 0))
    return pl.pallas_call(
        functools.partial(_state_step_kernel, n_heads=n_h, heads_per_key=n_h // n_g),
        out_shape=[jax.ShapeDtypeStruct(state.shape, F32), jax.ShapeDtypeStruct((n_b, n_h, 1, n_v), F32)],
        grid=(n_b,),
        in_specs=[blk(state), blk(a), blk(k_col), blk(q_col), blk(v_row)],
        out_specs=[blk(state), pl.BlockSpec((None, n_h, 1, n_v), lambda b: (b, 0, 0, 0))],
        compiler_params=_params("arbitrary"),
        name="state_step",
    )(state, a, k_col, q_col, v_row)


def _rope_rows_kernel(q_ref, k_ref, cos_ref, sin_ref, qo_ref, ko_ref, *, dk):
    for p in range(q_ref.shape[1] // LANES):
        cols = slice(p * LANES, (p + 1) * LANES)
        qo_ref[:, cols] = _rope(q_ref[:, cols], cos_ref[:, cols], sin_ref[:, cols], dk)
        ko_ref[:, cols] = _rope(k_ref[:, cols], cos_ref[:, cols], sin_ref[:, cols], dk) * (dk ** -0.5)


def _rope_rows(q, k, cos, sin, dk):
    args = (q, k, cos, sin)
    return pl.pallas_call(
        functools.partial(_rope_rows_kernel, dk=dk),
        out_shape=[jax.ShapeDtypeStruct(q.shape, F32)] * 2,
        grid=(1,),
        in_specs=[_full(a) for a in args],
        out_specs=[pl.BlockSpec(q.shape, lambda i: (0, 0))] * 2,
        compiler_params=_params("arbitrary"),
        name="rope_rows",
    )(*args)


def _fox_decode_kernel(pt_ref, q_ref, kn_ref, vn_ref, lfn_ref, kt_hbm, vt_hbm, lf_hbm, o_ref,
                       kbuf, vbuf, lbuf, sem, qb_ref, acc_ref, *, layer, n_pages, group, n_slots, n_heads, dh, page):
    b = pl.program_id(0)
    nb = pl.num_programs(0)
    n_groups = n_pages // group
    hd = n_heads * dh

    def copies(bb, gi, slot):
        out = []
        for g in range(group):
            pid = pt_ref[bb, gi * group + g]
            out.append(pltpu.make_async_copy(kt_hbm.at[layer, pid], kbuf.at[slot, g], sem.at[slot, 0]))
            out.append(pltpu.make_async_copy(vt_hbm.at[layer, pid], vbuf.at[slot, g], sem.at[slot, 1]))
            out.append(pltpu.make_async_copy(lf_hbm.at[layer, pid], lbuf.at[slot, g], sem.at[slot, 2]))
        return out

    def start(bb, gi, slot):
        for c in copies(bb, gi, slot):
            c.start()

    def wait(bb, gi, slot):
        for c in copies(bb, gi, slot):
            c.wait()

    @pl.when(b == 0)
    def _():
        for g in range(n_slots - 1):
            start(b, g, g)

    q_row = q_ref[...] * (dh ** -0.5)
    qb_ref[...] = jnp.broadcast_to(q_row, (page, hd)).T.reshape(n_heads, dh, page)
    acc_ref[...] = jnp.zeros_like(acc_ref)
    tri = (_iota2((page, page), 0) <= _iota2((page, page), 1)).astype(BF16)

    def page_update(slot, g, carry):
        m, l, run = carry
        s = jnp.concatenate(
            [jnp.sum(kbuf[slot, g, h] * qb_ref[h], axis=0, keepdims=True) for h in range(n_heads)], axis=0)
        fcum = _dot_sel_rhs(lbuf[slot, g], tri) + run
        s = s - fcum
        m_new = jnp.maximum(m, jnp.max(s, axis=-1, keepdims=True))
        alpha = jnp.exp(m - m_new)
        pe = jnp.exp(s - m_new)
        l_new = alpha * l + jnp.sum(pe, axis=-1, keepdims=True)
        for h in range(n_heads):
            acc_ref[h] = alpha[h:h + 1, :] * acc_ref[h] + pe[h:h + 1, :] * vbuf[slot, g, h]
        return m_new, l_new, fcum[:, page - 1:page]

    def ring(gq, carry):
        for s in range(n_slots):
            gi = gq * n_slots + s
            wait(b, gi, s)
            ahead = gi + n_slots - 1
            refill = (s + n_slots - 1) % n_slots

            @pl.when(ahead < n_groups)
            def _():
                start(b, ahead, refill)

            @pl.when(jnp.logical_and(ahead >= n_groups, b + 1 < nb))
            def _():
                start(b + 1, ahead - n_groups, refill)

            for g in range(group):
                carry = page_update(s, g, carry)
        return carry

    init = (jnp.full((n_heads, 1), NEG_INF, F32), jnp.zeros((n_heads, 1), F32), jnp.zeros((n_heads, 1), F32))
    m, l, f_tot = lax.fori_loop(0, n_groups // n_slots, ring, init)

    own = (_iota2((n_heads, hd), 1) // dh) == _iota2((n_heads, hd), 0)
    spread = lambda col: jnp.sum(jnp.where(own, col, 0.0), axis=0, keepdims=True)
    lfn = jnp.sum(jnp.where(_iota2((n_heads, LANES), 1) == _iota2((n_heads, LANES), 0), lfn_ref[...], 0.0),
                  axis=1, keepdims=True)
    s_new = jnp.sum(jnp.where(own, q_row * kn_ref[...], 0.0), axis=1, keepdims=True) - (f_tot + lfn)
    m_fin = jnp.maximum(m, s_new)
    a2 = jnp.exp(m - m_fin)
    p_new = jnp.exp(s_new - m_fin)
    l_fin = a2 * l + p_new
    o_past = jnp.sum(acc_ref[...].reshape(hd, page).T, axis=0, keepdims=True)
    o_ref[...] = (spread(a2) * o_past + spread(p_new) * vn_ref[...]) / spread(l_fin)


def _fox_decode(page_table, q, k_new, v_new, lf_new, kt_pages, vt_pages, lf_pages_t, layer):
    n_b, _, hd = q.shape
    n_pages = page_table.shape[1]
    _, _, n_h, dh, page = kt_pages.shape
    group = max(1, min(4, n_pages // 2))
    n_slots = 4 if n_pages % (4 * group) == 0 else 2
    assert n_pages % (n_slots * group) == 0
    tok = lambda arr: pl.BlockSpec((None, 1, arr.shape[2]), lambda b, pt: (b, 0, 0))
    grid_spec = pltpu.PrefetchScalarGridSpec(
        num_scalar_prefetch=1,
        grid=(n_b,),
        in_specs=[tok(q), tok(k_new), tok(v_new), tok(lf_new),
                  pl.BlockSpec(memory_space=pl.ANY), pl.BlockSpec(memory_space=pl.ANY),
                  pl.BlockSpec(memory_space=pl.ANY)],
        out_specs=pl.BlockSpec((None, 1, hd), lambda b, pt: (b, 0, 0)),
        scratch_shapes=[pltpu.VMEM((n_slots, group, n_h, dh, page), F32),
                        pltpu.VMEM((n_slots, group, n_h, dh, page), F32),
                        pltpu.VMEM((n_slots, group, n_h, page), F32), pltpu.SemaphoreType.DMA((n_slots, 3)),
                        pltpu.VMEM((n_h, dh, page), F32), pltpu.VMEM((n_h, dh, page), F32)],
    )
    return pl.pallas_call(
        functools.partial(_fox_decode_kernel, layer=layer, n_pages=n_pages, group=group, n_slots=n_slots,
                          n_heads=n_h, dh=dh, page=page),
        out_shape=jax.ShapeDtypeStruct((n_b, 1, hd), F32),
        grid_spec=grid_spec,
        compiler_params=_params("arbitrary"),
        name="fox_paged_decode",
    )(page_table, q, k_new, v_new, lf_new, kt_pages, vt_pages, lf_pages_t)


def _store_row_tiles(ref, val):
    n, d = val.shape
    nb = d // LANES
    for c in range(nb):
        ref[pl.ds(c, n, stride=nb), :] = val[:, c * LANES:(c + 1) * LANES]


def _load_row_tiles(ref, n, nb):
    return jnp.concatenate([ref[pl.ds(c, n, stride=nb), :] for c in range(nb)], axis=1)


def _router_kernel(y_ref, g_ref, sh_ref, sc_ref, rwt_ref, rb_ref, cin_ref,
                   h_ref, idx_ref, w_ref, rk_ref, cnt_ref, carry, *, n_exp):
    i = pl.program_id(0)

    @pl.when(i == 0)
    def _():
        carry[...] = cin_ref[...]

    h = _modulate(y_ref[...], g_ref[...], sh_ref[...], sc_ref[...])
    _store_row_tiles(h_ref, h)
    tm = h.shape[0]
    logits = _dot3(rwt_ref[...], h, dot=_dot_nt) + rb_ref[...]
    eio = _iota2((n_exp, tm), 0)
    vals, idxs = [], []
    rest = logits
    for _ in range(TOP_K):
        m = jnp.max(rest, axis=0, keepdims=True)
        ik = jnp.min(jnp.where(rest == m, eio, n_exp), axis=0, keepdims=True)
        vals.append(m)
        idxs.append(ik)
        rest = jnp.where(eio == ik, NEG_INF, rest)
    ex = [jnp.exp(v - vals[0]) for v in vals]
    den = ex[0]
    for e in ex[1:]:
        den = den + e
    sel = jnp.zeros((n_exp, tm), F32)
    for ik in idxs:
        sel = sel + jnp.where(eio == ik, 1.0, 0.0)
    before = (_iota2((tm, tm), 0) < _iota2((tm, tm), 1)).astype(BF16)
    rank_all = _dot(sel.astype(BF16), before) + carry[:, 0:1]
    ranks = [jnp.sum(jnp.where(eio == ik, rank_all, 0.0), axis=0, keepdims=True) for ik in idxs]
    carry[...] = carry[...] + jnp.sum(sel, axis=1, keepdims=True)
    idx_ref[...] = jnp.concatenate(idxs, axis=0)
    w_ref[...] = jnp.concatenate([e / den for e in ex], axis=0)
    rk_ref[...] = jnp.concatenate(ranks, axis=0).astype(I32)
    cnt_ref[...] = carry[...]


def _route(rows, y, g, mod, layer, rw_t, rb_col, cnt_in):
    d = y.shape[1]
    n_exp = rw_t.shape[0]
    sh_arr, sh_spec = rows.mod(mod, layer, 3, d)
    sc_arr, sc_spec = rows.mod(mod, layer, 4, d)
    kt = pl.BlockSpec((TOP_K, rows.tile), lambda i: (0, i))
    nb = d // LANES
    return pl.pallas_call(
        functools.partial(_router_kernel, n_exp=n_exp),
        out_shape=[jax.ShapeDtypeStruct((rows.rows * nb, LANES), F32),
                   jax.ShapeDtypeStruct((TOP_K, rows.rows), I32),
                   jax.ShapeDtypeStruct((TOP_K, rows.rows), F32),
                   jax.ShapeDtypeStruct((TOP_K, rows.rows), I32),
                   jax.ShapeDtypeStruct((n_exp, LANES), F32)],
        grid=(rows.n_tiles,),
        in_specs=[rows.spec(d), _full(g), sh_spec, sc_spec, _full(rw_t), _full(rb_col), _full(cnt_in)],
        out_specs=[pl.BlockSpec((rows.tile * nb, LANES), lambda i: (i, 0)), kt, kt, kt,
                   pl.BlockSpec((n_exp, LANES), lambda i: (0, 0))],
        scratch_shapes=[pltpu.VMEM((n_exp, LANES), F32)],
        compiler_params=_params("arbitrary"),
        name="moe_router",
    )(y, g, sh_arr, sc_arr, rw_t, rb_col, cnt_in)


def _dispatch_kernel(slot_ref, slot2_ref, pad_ref, nv_ref, h_ref, h2_ref, xs_ref, sem, zeros,
                     *, tm, n_rows, n_rows2, nb, n_exp, tmf, n_tiles):
    i = pl.program_id(0)

    @pl.when(i == 0)
    def _():
        zeros[...] = jnp.zeros_like(zeros)
        fills = [pltpu.make_async_copy(zeros, xs_ref.at[pl.ds(pl.multiple_of(pad_ref[e] * nb, nb), tmf * nb), :], sem)
                 for e in range(n_exp)]
        for c in fills:
            c.start()
        for c in fills:
            c.wait()

        def tail(j):
            return pltpu.make_async_copy(zeros, xs_ref.at[pl.ds(pl.multiple_of(j * (tmf * nb), tmf * nb), tmf * nb), :], sem)

        def start_tail(j, carry):
            tail(j).start()
            return carry

        def wait_tail(j, carry):
            tail(j).wait()
            return carry

        lax.fori_loop(nv_ref[0], n_tiles, start_tail, 0)
        lax.fori_loop(nv_ref[0], n_tiles, wait_tail, 0)

    def scatter(src_ref, slots, base, count, stride):
        def issue(t, carry):
            for k in range(TOP_K):
                s = slots[k * stride + base + t]
                pltpu.make_async_copy(src_ref.at[pl.ds(pl.multiple_of(t * nb, nb), nb), :],
                                      xs_ref.at[pl.ds(pl.multiple_of(s * nb, nb), nb), :], sem).start(priority=k % 2)
            return carry

        lax.fori_loop(0, count, issue, 0, unroll=2)
        for k in range(TOP_K):
            pltpu.make_async_copy(src_ref, xs_ref.at[pl.ds(0, count * nb), :], sem).wait()

    scatter(h_ref, slot_ref, i * tm, tm, n_rows)

    @pl.when(i == pl.num_programs(0) - 1)
    def _():
        scatter(h2_ref, slot2_ref, 0, n_rows2, n_rows2)


def _dispatch(rows, slots, slots2, pad_start, n_valid, h, h2, n_tiles, tmf, nb):
    n_exp = pad_start.shape[0]
    n_rows2 = h2.shape[0] // nb
    grid_spec = pltpu.PrefetchScalarGridSpec(
        num_scalar_prefetch=4,
        grid=(rows.n_tiles,),
        in_specs=[pl.BlockSpec((rows.tile * nb, LANES), lambda i, *_: (i, 0)),
                  pl.BlockSpec(h2.shape, lambda i, *_: (0, 0))],
        out_specs=pl.BlockSpec(memory_space=pl.ANY),
        scratch_shapes=[pltpu.SemaphoreType.DMA(()), pltpu.VMEM((tmf * nb, LANES), F32)],
    )
    return pl.pallas_call(
        functools.partial(_dispatch_kernel, tm=rows.tile, n_rows=rows.rows, n_rows2=n_rows2, nb=nb, n_exp=n_exp,
                          tmf=tmf, n_tiles=n_tiles),
        out_shape=jax.ShapeDtypeStruct((n_tiles * tmf * nb, LANES), F32),
        grid_spec=grid_spec,
        compiler_params=_params("arbitrary"),
        name="moe_dispatch",
    )(slots, slots2, pad_start, n_valid, h, h2)


def _split_gate_up_kernel(w_ref, wg_ref, wu_ref, wt_ref, *, chunk):
    half = chunk // 2
    n_r = w_ref.shape[0] // LANES
    for c in range(w_ref.shape[1] // chunk):
        for r in range(n_r):
            rows = slice(r * LANES, (r + 1) * LANES)
            wt = wt_ref.at[(c * n_r + r) % wt_ref.shape[0]]
            wt[...] = w_ref[rows, c * chunk:(c + 1) * chunk].T
            wg_ref[rows, c * half:(c + 1) * half] = wt[pl.ds(0, half, stride=2), :].T.astype(BF16)
            wu_ref[rows, c * half:(c + 1) * half] = wt[pl.ds(1, half, stride=2), :].T.astype(BF16)


def _split_gate_up(w_up):
    n_l, n_e, d, f2 = w_up.shape
    chunk = min(512, f2)
    blk = lambda width: pl.BlockSpec((None, None, d, width), lambda l, e: (l, e, 0, 0))
    return pl.pallas_call(
        functools.partial(_split_gate_up_kernel, chunk=chunk),
        out_shape=[jax.ShapeDtypeStruct((n_l, n_e, d, f2 // 2), BF16)] * 2,
        grid=(n_l, n_e),
        in_specs=[blk(f2)],
        out_specs=[blk(f2 // 2)] * 2,
        scratch_shapes=[pltpu.VMEM((4, chunk, LANES), F32)],
        compiler_params=_params("arbitrary", "arbitrary"),
        name="moe_split_gate_up",
    )(w_up)


def _ffn_kernel(te_ref, tv_ref, nv_ref, xs_ref, wg_ref, wu_ref, bg_ref, bu_ref, wd_ref, bd_ref, ys_ref, *, tmf, nb):
    del te_ref, nv_ref
    i = pl.program_id(0)

    @pl.when(tv_ref[i] == 1)
    def _():
        x = _load_row_tiles(xs_ref, tmf, nb).astype(BF16)
        gate = jnp.minimum(_dot(x, wg_ref[...]) + bg_ref[...], SWIGLU_LIMIT)
        up = jnp.clip(_dot(x, wu_ref[...]) + bu_ref[...], -SWIGLU_LIMIT, SWIGLU_LIMIT)
        act = (up + 1.0) * gate * _sigmoid(SWIGLU_ALPHA * gate)
        _store_row_tiles(ys_ref, _dot(act.astype(BF16), wd_ref[...].astype(BF16)) + bd_ref[...])

    @pl.when(tv_ref[i] == 0)
    def _():
        ys_ref[...] = jnp.zeros_like(ys_ref)


def _expert_ffn(tile_expert, tile_valid, n_valid, xs, layer, wg, wu, bg, bu, wd, bd, tmf, nb):
    n_tiles = tile_expert.shape[0]
    d, f = wg.shape[2], wg.shape[3]
    ex = lambda i, te, tv, nv: (te[i], 0, 0)
    lex = lambda i, te, tv, nv: (layer, te[i], 0, 0)
    grid_spec = pltpu.PrefetchScalarGridSpec(
        num_scalar_prefetch=3,
        grid=(n_tiles,),
        in_specs=[pl.BlockSpec((tmf * nb, LANES), lambda i, te, tv, nv: (jnp.minimum(i, nv[0] - 1), 0)),
                  pl.BlockSpec((None, None, d, f), lex), pl.BlockSpec((None, None, d, f), lex),
                  pl.BlockSpec((None, 1, f), ex), pl.BlockSpec((None, 1, f), ex),
                  pl.BlockSpec((None, None, f, d), lex), pl.BlockSpec((None, 1, d), ex)],
        out_specs=pl.BlockSpec((tmf * nb, LANES), lambda i, te, tv, nv: (i, 0)),
    )
    return pl.pallas_call(
        functools.partial(_ffn_kernel, tmf=tmf, nb=nb),
        out_shape=jax.ShapeDtypeStruct((n_tiles * tmf * nb, LANES), F32),
        grid_spec=grid_spec,
        compiler_params=_params("arbitrary"),
        name="moe_expert_ffn",
    )(tile_expert, tile_valid, n_valid, xs, wg, wu, bg, bu, wd, bd)


def _combine_kernel(slot_ref, ys_ref, w_ref, y_ref, gm_ref, o_ref, buf, sem, *, tm, n_rows, nb):
    i = pl.program_id(0)
    n = pl.num_programs(0)

    def gather(tile, slot):
        def issue(t, carry):
            for k in range(TOP_K):
                s = slot_ref[k * n_rows + tile * tm + t]
                pltpu.make_async_copy(ys_ref.at[pl.ds(pl.multiple_of(s * nb, nb), nb), :],
                                      buf.at[slot, k, pl.ds(pl.multiple_of(t * nb, nb), nb), :],
                                      sem.at[slot]).start(priority=k % 2)
            return carry

        lax.fori_loop(0, tm, issue, 0, unroll=2)

    @pl.when(i == 0)
    def _():
        gather(0, 0)

    cur = i % 2

    @pl.when(i + 1 < n)
    def _():
        gather(i + 1, 1 - cur)

    for k in range(TOP_K):
        pltpu.make_async_copy(ys_ref.at[pl.ds(0, tm * nb), :], buf.at[cur, k], sem.at[cur]).wait()
    ws = [jnp.broadcast_to(w_ref[:, k:k + 1], (tm, LANES)) for k in range(TOP_K)]
    for c in range(nb):
        cols = slice(c * LANES, (c + 1) * LANES)
        acc = ws[0] * buf[cur, 0, pl.ds(c, tm, stride=nb), :]
        for k in range(1, TOP_K):
            acc = acc + ws[k] * buf[cur, k, pl.ds(c, tm, stride=nb), :]
        o_ref[:, cols] = y_ref[:, cols] + gm_ref[:, cols] * acc


def _combine(rows, slots_flat, ys, w_tok, y, mod, layer):
    d = y.shape[1]
    nb = d // LANES
    gm_arr, gm_spec = rows.mod(mod, layer, 5, d)
    grid_spec = pltpu.PrefetchScalarGridSpec(
        num_scalar_prefetch=1,
        grid=(rows.n_tiles,),
        in_specs=[pl.BlockSpec(memory_space=pl.ANY),
                  pl.BlockSpec((rows.tile, TOP_K), lambda i, s: (i, 0)),
                  pl.BlockSpec((rows.tile, d), lambda i, s: (i, 0)), gm_spec],
        out_specs=pl.BlockSpec((rows.tile, d), lambda i, s: (i, 0)),
        scratch_shapes=[pltpu.VMEM((2, TOP_K, rows.tile * nb, LANES), F32), pltpu.SemaphoreType.DMA((2,))],
    )
    return pl.pallas_call(
        functools.partial(_combine_kernel, tm=rows.tile, n_rows=rows.rows, nb=nb),
        out_shape=jax.ShapeDtypeStruct(y.shape, F32),
        grid_spec=grid_spec,
        compiler_params=_params("arbitrary"),
        name="moe_combine",
    )(slots_flat, ys, w_tok, y, gm_arr)


def _rmsnorm_kernel(x_ref, g_ref, o_ref):
    x = x_ref[...]
    o_ref[...] = x * lax.rsqrt(jnp.mean(x * x, axis=-1, keepdims=True) + 1e-6) * g_ref[...]


def _final_norm(rows, x, g):
    d = x.shape[1]
    return pl.pallas_call(
        _rmsnorm_kernel,
        out_shape=jax.ShapeDtypeStruct(x.shape, F32),
        grid=(rows.n_tiles,),
        in_specs=[rows.spec(d), _full(g)],
        out_specs=rows.spec(d),
        compiler_params=_params("arbitrary"),
        name="final_rmsnorm",
    )(x, g)


def _moe_layer(groups, ys, layer, g_ffn, router_w, router_b, wg_all, wu_all, exp_b_up, wd_all, exp_b_down, tmf):
    n_exp = router_w.shape[1]
    d = router_w.shape[0]
    rw_t = router_w.T
    rb_col = router_b.reshape(n_exp, 1)
    cnt = jnp.zeros((n_exp, LANES), F32)
    routed = []
    for (rows, m), y in zip(groups, ys):
        h, idx, w, rank, cnt = _route(rows, y, g_ffn, m, layer, rw_t, rb_col, cnt)
        routed.append((h, idx, w, rank))
    total = sum(rows.rows for rows, _ in groups) * TOP_K
    n_tiles = -(-total // tmf) + n_exp
    counts = cnt[:, 0].astype(I32)
    tiles_e = (counts + tmf - 1) // tmf
    tile_end = jnp.cumsum(tiles_e)
    starts = (tile_end - tiles_e) * tmf
    tile_ids = jnp.arange(n_tiles, dtype=I32)
    tile_expert = jnp.minimum(jnp.sum((tile_end[None, :] <= tile_ids[:, None]).astype(I32), axis=1), n_exp - 1)
    tile_valid = (tile_ids < tile_end[-1]).astype(I32)
    nb = d // LANES
    pad_start = starts + counts
    e_ids = jnp.arange(n_exp, dtype=I32)
    slots = [(jnp.sum(jnp.where(idx[..., None] == e_ids, starts, 0), axis=-1) + rank).reshape(-1)
             for (h, idx, w, rank) in routed]
    xs = _dispatch(groups[0][0], slots[0], slots[1], pad_start, tile_end[-1:], routed[0][0], routed[1][0],
                   n_tiles + 1, tmf, nb)
    bg = exp_b_up[:, None, 0::2]
    bu = exp_b_up[:, None, 1::2]
    y_sorted = _expert_ffn(tile_expert, tile_valid, tile_end[-1:], xs, layer, wg_all, wu_all, bg, bu,
                           wd_all, exp_b_down[:, None, :], tmf, nb)
    outs = []
    for (rows, m), y, sl, (h, idx, w, rank) in zip(groups, ys, slots, routed):
        outs.append(_combine(rows, sl, y_sorted, w.T, y, m, layer))
    return outs


def kernel(x_prompt, x_sample, c_prompt, c_sample, cache_k, cache_v, cache_logf, page_table, state_ret, state_ssm, state_conv, ada_w, ada_b, norm_mix_g, norm_ffn_g, norm_final_g, att_w_in, att_b_f, ret_gn_g, ret_gn_b, att_w_out, ssm_w_in, ssm_conv_w, ssm_conv_b, ssm_dt_bias, ssm_a_log, ssm_d, ssm_norm_g, ssm_w_out, router_w, router_b, exp_w_up, exp_b_up, exp_w_down, exp_b_down):
    bp, sp, d = x_prompt.shape
    bs = x_sample.shape[0]
    depth = ada_w.shape[0]
    n_pages, page = page_table.shape[1], cache_k.shape[2]
    past = n_pages * page
    h_a, dh_a = cache_k.shape[3], cache_k.shape[4]
    h_b, dk_b, dv_b = state_ret.shape[2:]
    h_c, d_state, hd_c = state_ssm.shape[2:]
    d_inner = h_c * hd_c
    conv_w_len, conv_dim = ssm_conv_w.shape[1:]
    gn = (conv_dim - d_inner) // 2
    n_groups = gn // d_state
    assert conv_w_len == 4 and hd_c * 2 == LANES and h_a <= LANES and h_c <= LANES

    rows_p = _Rows(bp, sp, 256)
    rows_m = _Rows(bp, sp, 512)
    rows_s = _Rows(bs, 1, bs)
    yp = x_prompt.reshape(bp * sp, d)
    ys = x_sample.reshape(bs, d)

    mod = _modulation_all(jnp.concatenate([c_prompt, c_sample], axis=0), ada_w, ada_b)
    mod_p, mod_s = mod[:, :bp], mod[:, bp:]

    wa, wr, wvr = h_a * dh_a, h_b * dk_b, h_b * dv_b
    att_segs = []
    off = 0
    for wdt in (wa, wa, wa, wr, wr, wvr, wvr, LANES):
        att_segs.append((off, wdt))
        off += wdt
    ssm_segs = ((0, d_inner), (d_inner, conv_dim), (d_inner + conv_dim, LANES))

    cos_p, sin_p = _rope_tables(jnp.arange(sp), h_b, dk_b)
    cos_s, sin_s = _rope_tables(jnp.full((1,), past), h_b, dk_b)
    log_gammas = [math.log1p(-2.0 ** (-RET_DECAY_BASE - h)) for h in range(h_b)]
    gamma_col = jnp.broadcast_to(jnp.asarray(np.exp(log_gammas), F32).reshape(1, h_b, 1, 1), (bs, h_b, 1, 1))
    kt_pages = jnp.transpose(cache_k, (0, 1, 3, 4, 2))
    vt_pages = jnp.transpose(cache_v, (0, 1, 3, 4, 2))
    lf_pages_t = jnp.swapaxes(cache_logf, 2, 3)
    n_exp = router_w.shape[2]
    tmf = 512 if (bp * sp + bs) * TOP_K >= 512 * n_exp else 64
    e_mat = (jnp.arange(LANES)[:, None] == (jnp.arange(d_inner) // hd_c)[None, :]).astype(BF16)
    wg_all, wu_all = _split_gate_up(exp_w_up)

    k_p, v_p, f_p, k_s, v_s, f_s, ret_p, ret_s = [], [], [], [], [], [], [], []
    ssm_p, ssm_s, conv_p, conv_s = [], [], [], []
    for l in range(depth):
        j = l // 2
        g_mix = norm_mix_g[l][None]
        if l % 2 == 0:
            w = att_w_in[j]
            o0 = 3 * wa
            n_att = att_segs[-1][0] + LANES
            w_perm32 = jnp.concatenate(
                [w[:, :o0], w[:, o0 + h_a:], w[:, o0:o0 + h_a],
                 jnp.zeros((d, -(-n_att // F32_COL_BLOCK) * F32_COL_BLOCK - n_att + LANES - h_a), F32)], axis=1)
            w_perm = w_perm32[:, :n_att].astype(BF16)
            b_f_pad = jnp.pad(att_b_f[j], (0, LANES - h_a))[None]
            w1_32, w2_32 = att_w_out[j][:wa], att_w_out[j][wa:]
            w1, w2 = w1_32.astype(BF16), w2_32.astype(BF16)
            gn_g, gn_b = ret_gn_g[j][None], ret_gn_b[j][None]
            qa, ka, va, qr, kr, vr, gate, fa = _project(rows_p, yp, g_mix, mod_p, l, w_perm, att_segs)
            logf, fcum = _forget_gates(rows_p, fa, b_f_pad)
            hp8 = -(-h_a // 8) * 8
            fk_t = jnp.transpose(fcum.reshape(bp, sp, LANES)[:, :, :hp8], (0, 2, 1))
            o_fox = _fox_prompt(qa, ka, va, fcum, fk_t, bp, sp, dh_a, h_a)
            o_ret, st = _retention_prompt(qr, kr, vr, cos_p, sin_p, bp, sp, h_b, dk_b, dv_b)
            yp = _att_output(rows_p, o_fox, o_ret, gate, yp, mod_p, l, gn_g, gn_b, w1, w2, h_b, dv_b)
            k_p.append(ka.reshape(bp, sp, h_a, dh_a))
            v_p.append(va.reshape(bp, sp, h_a, dh_a))
            f_p.append(logf[:, :h_a].reshape(bp, sp, h_a))
            ret_p.append(st)
            qa, ka, va, qr, kr, vr, gate, fa = _project_f32(rows_s, ys, g_mix, mod_s, l, w_perm32, att_segs)
            logf, _ = _forget_gates(rows_s, fa, b_f_pad)
            o_fox = _fox_decode(page_table, qa[:, None, :], ka[:, None, :], va[:, None, :], logf[:, None, :],
                                kt_pages, vt_pages, lf_pages_t, j).reshape(bs, wa)
            qr2, kr2 = _rope_rows(qr, kr, cos_s, sin_s, dk_b)
            st, y_ret = _state_step(state_ret[j], gamma_col, kr2.reshape(bs, h_b, dk_b, 1),
                                    qr2.reshape(bs, h_b, dk_b, 1), vr.reshape(bs, h_b, 1, dv_b))
            ys = _att_output(rows_s, o_fox, y_ret.reshape(bs, wvr), gate, ys, mod_s, l, gn_g, gn_b, w1_32, w2_32,
                             h_b, dv_b)
            k_s.append(ka.reshape(bs, 1, h_a, dh_a))
            v_s.append(va.reshape(bs, 1, h_a, dh_a))
            f_s.append(logf[:, :h_a].reshape(bs, 1, h_a))
            ret_s.append(st)
        else:
            n_ssm = ssm_segs[-1][0] + LANES
            w_pad32 = jnp.pad(ssm_w_in[j], ((0, 0), (0, -(-n_ssm // F32_COL_BLOCK) * F32_COL_BLOCK - n_ssm + LANES - h_c)))
            w_pad = w_pad32[:, :n_ssm].astype(BF16)
            cw, cb = ssm_conv_w[j], ssm_conv_b[j][None]
            dtb = jnp.pad(ssm_dt_bias[j], (0, LANES - h_c))[None]
            alog = jnp.pad(ssm_a_log[j], (0, LANES - h_c))[None]
            dsk = jnp.repeat(ssm_d[j], hd_c)[None]
            ng = ssm_norm_g[j][None]
            w_out = ssm_w_out[j].astype(BF16)
            z, xr, dtr = _project(rows_p, yp, g_mix, mod_p, l, w_pad, ssm_segs)
            y_n, st2, cv = _ssd_prompt(xr, z, dtr, cw, cb, dtb, alog, dsk, ng, e_mat, bp, sp,
                                       d_inner, n_groups, d_state, hd_c)
            yp = _matmul_residual(rows_p, y_n, yp, mod_p, l, w_out)
            st = st2.reshape(bp, h_c // 2, d_state, 2, hd_c).transpose(0, 1, 3, 2, 4).reshape(bp, h_c, d_state, hd_c)
            ssm_p.append(st)
            conv_p.append(cv[:, 8 - (conv_w_len - 1):, :])
            z, xr, dtr = _project_f32(rows_s, ys, g_mix, mod_s, l, w_pad32, ssm_segs)
            taps = [state_conv[j][:, i, :] for i in range(conv_w_len - 1)]
            x, bm, cm, v, a = _ssd_step_prep(xr, taps, dtr, cw, cb, dtb, alog, e_mat, d_inner, gn)
            st, y_s = _state_step(state_ssm[j], a[:, :h_c].reshape(bs, h_c, 1, 1),
                                  bm.reshape(bs, n_groups, d_state, 1), cm.reshape(bs, n_groups, d_state, 1),
                                  v.reshape(bs, h_c, 1, hd_c))
            y_n = _ssd_step_post(y_s.reshape(bs, d_inner), x, z, dsk, ng, n_groups)
            ys = _matmul_residual(rows_s, y_n, ys, mod_s, l, ssm_w_out[j])
            ssm_s.append(st)
            conv_s.append(jnp.concatenate([state_conv[j][:, 1:, :], xr[:, None, :]], axis=1))
        yp, ys = _moe_layer([(rows_m, mod_p), (rows_s, mod_s)], [yp, ys], l, norm_ffn_g[l][None],
                            router_w[l], router_b[l], wg_all, wu_all, exp_b_up[l], exp_w_down, exp_b_down[l], tmf)
    y_prompt = _final_norm(rows_p, yp, norm_final_g[None]).reshape(bp, sp, d)
    y_sample = _final_norm(rows_s, ys, norm_final_g[None]).reshape(bs, 1, d)
    return (y_prompt, y_sample,
            jnp.stack(k_p), jnp.stack(v_p), jnp.stack(f_p),
            jnp.stack(k_s), jnp.stack(v_s), jnp.stack(f_s),
            jnp.stack(ret_p), jnp.stack(ret_s),
            jnp.stack(ssm_p), jnp.stack(ssm_s),
            jnp.stack(conv_p), jnp.stack(conv_s))
```

```python
import functools
import math

import jax
import jax.numpy as jnp
import numpy as np
from jax import lax
from jax.experimental import pallas as pl
from jax.experimental.pallas import tpu as pltpu

F32 = jnp.float32
BF16 = jnp.bfloat16
I32 = jnp.int32

LANES = 128
VMEM_LIMIT = 56 * 1024 * 1024
CHUNK = 128
F32_COL_BLOCK = 512
TOP_K = 4
RET_DECAY_BASE = 5.0
ROPE_THETA = 10000.0
SWIGLU_LIMIT = 7.0
SWIGLU_ALPHA = 1.702
NEG_INF = float("-inf")


def _params(*sem):
    return pltpu.CompilerParams(dimension_semantics=sem, vmem_limit_bytes=VMEM_LIMIT)


def _dot(a, b):
    return jnp.dot(a, b, preferred_element_type=F32)


def _dot_nt(a, b):
    return lax.dot_general(a, b, (((1,), (1,)), ((), ())), preferred_element_type=F32)


def _dot_tn(a, b):
    return lax.dot_general(a, b, (((0,), (0,)), ((), ())), preferred_element_type=F32)


def _split3(x):
    hi = x.astype(BF16)
    r = x - hi.astype(F32)
    mid = r.astype(BF16)
    lo = (r - mid.astype(F32)).astype(BF16)
    return hi, mid, lo


def _dot_sel_rhs(x, m):
    hi, mid, lo = _split3(x)
    return _dot(hi, m) + _dot(mid, m) + _dot(lo, m)


def _dot_sel_lhs(m, x):
    hi, mid, lo = _split3(x)
    return _dot(m, hi) + _dot(m, mid) + _dot(m, lo)


def _dot3(a, b, dot=_dot):
    a_hi = a.astype(BF16)
    a_lo = (a - a_hi.astype(F32)).astype(BF16)
    b_hi = b.astype(BF16)
    b_lo = (b - b_hi.astype(F32)).astype(BF16)
    return dot(a_hi, b_hi) + dot(a_hi, b_lo) + dot(a_lo, b_hi)


def _dot_w(a, w_ref):
    w = w_ref[...]
    return _dot3(a, w) if w.dtype == F32 else _dot(a.astype(BF16), w)


def _sigmoid(x):
    return 1.0 / (1.0 + jnp.exp(-x))


def _silu(x):
    return x * _sigmoid(x)


def _softplus(x):
    return jnp.maximum(x, 0.0) + jnp.log(1.0 + jnp.exp(-jnp.abs(x)))


def _log_sigmoid(x):
    return -_softplus(-x)


def _modulate(x, g, shift, scale):
    ms = jnp.mean(x * x, axis=-1, keepdims=True)
    return (x * lax.rsqrt(ms + 1e-6)) * g * (1.0 + scale) + shift


def _iota2(shape, dim):
    return lax.broadcasted_iota(I32, shape, dim)


class _Rows:
    def __init__(self, n_batch, rows_per_batch, tile):
        self.n_batch = n_batch
        self.rows_per_batch = rows_per_batch
        self.rows = n_batch * rows_per_batch
        self.per_row_mod = rows_per_batch == 1
        self.tile = self.rows if self.per_row_mod else min(tile, rows_per_batch)
        assert self.rows % self.tile == 0 and (self.per_row_mod or rows_per_batch % self.tile == 0)
        self.n_tiles = self.rows // self.tile
        self.tiles_per_batch = 1 if self.per_row_mod else rows_per_batch // self.tile

    def spec(self, width, col=0):
        return pl.BlockSpec((self.tile, width), lambda i, *_: (i, col))

    def mod(self, mod_arr, layer, chunk, d):
        if self.per_row_mod:
            return mod_arr, pl.BlockSpec((None, self.rows, d), lambda i, *_: (layer, 0, chunk))
        tpb = self.tiles_per_batch
        arr = mod_arr.reshape(mod_arr.shape[0], mod_arr.shape[1], 1, mod_arr.shape[2])
        return arr, pl.BlockSpec((None, None, 1, d), lambda i, *_: (layer, i // tpb, 0, chunk))


def _full(arr):
    nd = arr.ndim
    return pl.BlockSpec(arr.shape, lambda *_: (0,) * nd)


def _mod_kernel(c_ref, w_ref, b_ref, o_ref):
    c = c_ref[...]
    o_ref[...] = _dot3(_silu(c), w_ref[...]) + b_ref[...]


def _modulation_all(c_all, ada_w, ada_b):
    n_layers, d, n6 = ada_w.shape
    rows = c_all.shape[0]
    tn = n6 // 8 if n6 % (8 * LANES) == 0 else n6
    return pl.pallas_call(
        _mod_kernel,
        out_shape=jax.ShapeDtypeStruct((n_layers, rows, n6), F32),
        grid=(n_layers, n6 // tn),
        in_specs=[pl.BlockSpec((rows, d), lambda l, j: (0, 0)),
                  pl.BlockSpec((None, d, tn), lambda l, j: (l, 0, j)),
                  pl.BlockSpec((None, 1, tn), lambda l, j: (l, 0, j))],
        out_specs=pl.BlockSpec((None, rows, tn), lambda l, j: (l, 0, j)),
        compiler_params=_params("arbitrary", "arbitrary"),
        name="adaln_modulation",
    )(c_all, ada_w, ada_b.reshape(n_layers, 1, n6))


def _proj_kernel(x_ref, g_ref, sh_ref, sc_ref, w_ref, *o_refs, segs, also_transposed):
    h = _modulate(x_ref[...], g_ref[...], sh_ref[...], sc_ref[...]).astype(BF16)
    t_refs = list(o_refs[len(segs):])
    for i, (o_ref, (start, width)) in enumerate(zip(o_refs, segs)):
        val = _dot(h, w_ref[:, start:start + width])
        o_ref[...] = val
        if i in also_transposed:
            t_refs.pop(0)[...] = val.T


def _project(rows, x, g, mod, layer, w_bf16, segs, also_transposed=()):
    d = x.shape[1]
    sh_arr, sh_spec = rows.mod(mod, layer, 0, d)
    sc_arr, sc_spec = rows.mod(mod, layer, 1, d)
    tpb = rows.tiles_per_batch
    t_shapes = [jax.ShapeDtypeStruct((rows.n_batch, segs[i][1], rows.rows_per_batch), F32) for i in also_transposed]
    t_specs = [pl.BlockSpec((None, segs[i][1], rows.tile), lambda t: (t // tpb, 0, t % tpb)) for i in also_transposed]
    return pl.pallas_call(
        functools.partial(_proj_kernel, segs=segs, also_transposed=tuple(also_transposed)),
        out_shape=[jax.ShapeDtypeStruct((rows.rows, wd), F32) for _, wd in segs] + t_shapes,
        grid=(rows.n_tiles,),
        in_specs=[rows.spec(d), _full(g), sh_spec, sc_spec, _full(w_bf16)],
        out_specs=[rows.spec(wd) for _, wd in segs] + t_specs,
        compiler_params=_params("arbitrary"),
        name="norm_mod_project",
    )(x, g, sh_arr, sc_arr, w_bf16)


def _proj_f32_kernel(x_ref, g_ref, sh_ref, sc_ref, w_ref, o_ref):
    o_ref[...] = _dot3(_modulate(x_ref[...], g_ref[...], sh_ref[...], sc_ref[...]), w_ref[...])


def _project_f32(rows, x, g, mod, layer, w_f32, segs):
    d, n_pad = w_f32.shape
    cb = F32_COL_BLOCK
    assert n_pad % cb == 0
    w_pad = w_f32
    sh_arr, sh_spec = rows.mod(mod, layer, 0, d)
    sc_arr, sc_spec = rows.mod(mod, layer, 1, d)
    assert rows.n_tiles == 1
    out = pl.pallas_call(
        _proj_f32_kernel,
        out_shape=jax.ShapeDtypeStruct((rows.rows, n_pad), F32),
        grid=(n_pad // cb,),
        in_specs=[pl.BlockSpec((rows.rows, d), lambda j: (0, 0)), _full(g),
                  pl.BlockSpec(sh_spec.block_shape, lambda j: sh_spec.index_map(0)),
                  pl.BlockSpec(sc_spec.block_shape, lambda j: sc_spec.index_map(0)),
                  pl.BlockSpec((d, cb), lambda j: (0, j))],
        out_specs=pl.BlockSpec((rows.rows, cb), lambda j: (0, j)),
        compiler_params=_params("arbitrary"),
        name="norm_mod_project_f32",
    )(x, g, sh_arr, sc_arr, w_pad)
    return [out[:, s:s + wd] for s, wd in segs]


def _logf_kernel(fa_ref, bf_ref, lf_ref, fc_ref, carry_ref, *, tiles_per_batch):
    i = pl.program_id(0)
    lf = _log_sigmoid(fa_ref[...] + bf_ref[...])
    lf_ref[...] = lf

    @pl.when(i % tiles_per_batch == 0)
    def _():
        carry_ref[...] = jnp.zeros_like(carry_ref)

    tm = lf.shape[0]
    tri = (_iota2((tm, tm), 1) <= _iota2((tm, tm), 0)).astype(BF16)
    cs = _dot_sel_lhs(tri, lf) + carry_ref[...]
    fc_ref[...] = cs
    carry_ref[...] = cs[tm - 1:tm, :]


def _forget_gates(rows, fa_raw, b_f_pad):
    return pl.pallas_call(
        functools.partial(_logf_kernel, tiles_per_batch=rows.tiles_per_batch),
        out_shape=[jax.ShapeDtypeStruct((rows.rows, LANES), F32)] * 2,
        grid=(rows.n_tiles,),
        in_specs=[rows.spec(LANES), _full(b_f_pad)],
        out_specs=[rows.spec(LANES)] * 2,
        scratch_shapes=[pltpu.VMEM((1, LANES), F32)],
        compiler_params=_params("arbitrary"),
        name="forget_gates",
    )(fa_raw, b_f_pad)


def _fox_kernel(q_ref, k_ref, v_ref, fc_ref, ft_ref, o_ref, fkb_ref, qm_ref, acc_ref, *, tq, dh, n_blocks, n_heads):
    qi = pl.program_id(1)
    seq = k_ref.shape[0]
    hpb = LANES // dh

    @pl.when(qi == 0)
    def _():
        for h in range(n_heads):
            fkb_ref[h] = jnp.broadcast_to(fc_ref[:, h:h + 1], (seq, LANES))

    q0 = pl.multiple_of(qi * tq, tq)
    lane = _iota2((1, LANES), 1)
    row_head = _iota2((LANES, 1), 0) // dh
    on_or_before = _iota2((tq, tq), 0) <= _iota2((tq, tq), 1)
    for p in range(n_blocks):
        q2 = q_ref[:, p * LANES:(p + 1) * LANES] * (dh ** -0.5)
        for j in range(hpb):
            qm_ref[p * hpb + j] = jnp.where(lane // dh == j, q2, 0.0).astype(BF16)
    acc_ref[...] = jnp.zeros_like(acc_ref)

    def block(k0, carry, diagonal):
        ms, ls = carry
        ss = []
        for p in range(n_blocks):
            k2 = k_ref[pl.ds(k0, tq), p * LANES:(p + 1) * LANES].astype(BF16)
            ss += [_dot_nt(k2, qm_ref[p * hpb + j]) for j in range(hpb)]
        new_ms, new_ls, alphas, pes = [], [], [], []
        for h in range(n_heads):
            fk = fkb_ref[h, pl.ds(k0, tq), :]
            s = ss[h] + (ft_ref[h:h + 1, pl.ds(q0, tq)] - jnp.concatenate([fk] * (tq // LANES), axis=1))
            if diagonal:
                s = jnp.where(on_or_before, s, NEG_INF)
            m_new = jnp.maximum(ms[h], jnp.max(s, axis=0, keepdims=True))
            alphas.append(jnp.exp(ms[h] - m_new))
            pe = jnp.exp(s - m_new)
            new_ms.append(m_new)
            new_ls.append(alphas[h] * ls[h] + jnp.sum(pe, axis=0, keepdims=True))
            pes.append(pe.astype(BF16))
        for p in range(n_blocks):
            v2 = v_ref[pl.ds(k0, tq), p * LANES:(p + 1) * LANES].astype(BF16)
            alpha2 = None
            pv2 = None
            for j in range(hpb):
                h = p * hpb + j
                pv = _dot_tn(v2, pes[h])
                sel = row_head == j
                alpha2 = jnp.where(sel, alphas[h], 0.0) if alpha2 is None else jnp.where(sel, alphas[h], alpha2)
                pv2 = jnp.where(sel, pv, 0.0) if pv2 is None else jnp.where(sel, pv, pv2)
            acc_ref[p] = alpha2 * acc_ref[p] + pv2
        return tuple(new_ms), tuple(new_ls)

    init = (tuple(jnp.full((1, tq), NEG_INF, F32) for _ in range(n_heads)),
            tuple(jnp.zeros((1, tq), F32) for _ in range(n_heads)))
    carry = lax.fori_loop(0, qi, lambda kv, c: block(pl.multiple_of(kv * tq, tq), c, False), init)
    ms, ls = block(q0, carry, True)
    for p in range(n_blocks):
        l2 = None
        for j in range(hpb):
            sel = row_head == j
            l2 = jnp.where(sel, ls[p * hpb + j], 1.0) if l2 is None else jnp.where(sel, ls[p * hpb + j], l2)
        o_ref[:, p * LANES:(p + 1) * LANES] = (acc_ref[p] / l2).T


def _fox_prompt(q, k, v, fcum, fcum_t, n_batch, seq, dh, n_heads):
    width = q.shape[1]
    assert width % LANES == 0 and LANES % dh == 0
    tq = min(256, seq)
    nq = seq // tq
    hp = fcum_t.shape[1]
    return pl.pallas_call(
        functools.partial(_fox_kernel, tq=tq, dh=dh, n_blocks=width // LANES, n_heads=n_heads),
        out_shape=jax.ShapeDtypeStruct(q.shape, F32),
        grid=(n_batch, nq),
        in_specs=[pl.BlockSpec((tq, width), lambda b, i: (b * nq + i, 0)),
                  pl.BlockSpec((seq, width), lambda b, i: (b, 0)),
                  pl.BlockSpec((seq, width), lambda b, i: (b, 0)),
                  pl.BlockSpec((seq, LANES), lambda b, i: (b, 0)),
                  pl.BlockSpec((None, hp, seq), lambda b, i: (b, 0, 0))],
        out_specs=pl.BlockSpec((tq, width), lambda b, i: (b * nq + i, 0)),
        scratch_shapes=[pltpu.VMEM((n_heads, seq, LANES), F32), pltpu.VMEM((n_heads, tq, LANES), BF16),
                        pltpu.VMEM((width // LANES, LANES, tq), F32)],
        compiler_params=_params("arbitrary", "arbitrary"),
        name="fox_prompt_attention",
    )(q, k, v, fcum, fcum_t)


def _rope_tables(pos, n_heads, dk):
    half = dk // 2
    freq = ROPE_THETA ** (-jnp.arange(half, dtype=F32) / half)
    ang = pos.astype(F32)[:, None] * freq[None, :]
    cos = jnp.cos(ang)
    sin = jnp.sin(ang)
    cos_h = jnp.concatenate([cos, cos], axis=-1)
    sin_h = jnp.concatenate([-sin, sin], axis=-1)
    return jnp.tile(cos_h, (1, n_heads)), jnp.tile(sin_h, (1, n_heads))


def _rope(x, cos, sin, dk):
    half = dk // 2
    lane = _iota2((1, LANES), 1)
    up = pltpu.roll(x, LANES - half, 1)
    down = pltpu.roll(x, half, 1)
    partner = jnp.where((lane % dk) < half, up, down)
    return x * cos + partner * sin


def _ret_kernel(q_ref, k_ref, v_ref, cos_ref, sin_ref, o_ref, st_ref, state, *, n_heads, dk, dv, log_gammas):
    c = pl.program_id(1)
    hpb = LANES // dk

    @pl.when(c == 0)
    def _():
        state[...] = jnp.zeros_like(state)

    lane = _iota2((1, LANES), 1)
    t_col = _iota2((CHUNK, 1), 0).astype(F32)
    seg = (_iota2((CHUNK, CHUNK), 0) - _iota2((CHUNK, CHUNK), 1)).astype(F32)
    causal = seg >= 0.0
    row_head = _iota2((LANES, 1), 0) // dk
    scores, carried, updates, row_decays = [], [], [], []
    for p in range(n_heads // hpb):
        cols = slice(p * LANES, (p + 1) * LANES)
        cos = cos_ref[:, cols]
        sin = sin_ref[:, cols]
        q2 = _rope(q_ref[:, cols], cos, sin, dk)
        k2 = _rope(k_ref[:, cols], cos, sin, dk) * (dk ** -0.5)
        k2b = k2.astype(BF16)
        st2b = state[p].astype(BF16)
        upd = None
        row_decay = None
        for j in range(hpb):
            h = p * hpb + j
            lg = log_gammas[h]
            sel = lane // dk == j
            qh = jnp.where(sel, q2, 0.0).astype(BF16)
            scores.append(_dot_nt(qh, k2b))
            carried.append(_dot(qh, st2b))
            tail = jnp.exp((CHUNK - 1.0 - t_col) * lg)
            kh = jnp.where(sel, k2, 0.0).astype(BF16)
            u = _dot_tn(kh, (v_ref[:, h * dv:(h + 1) * dv] * tail).astype(BF16))
            upd = u if upd is None else upd + u
            rd = jnp.where(row_head == j, math.exp(CHUNK * lg), 0.0)
            row_decay = rd if row_decay is None else row_decay + rd
        updates.append(upd)
        row_decays.append(row_decay)
    for h in range(n_heads):
        lg = log_gammas[h]
        decay = jnp.exp(jnp.where(causal, seg * lg, NEG_INF))
        y = _dot((scores[h] * decay).astype(BF16), v_ref[:, h * dv:(h + 1) * dv].astype(BF16))
        o_ref[:, h * dv:(h + 1) * dv] = y + jnp.exp((t_col + 1.0) * lg) * carried[h]
    for p in range(n_heads // hpb):
        state[p] = row_decays[p] * state[p] + updates[p]

    @pl.when(c == pl.num_programs(1) - 1)
    def _():
        for h in range(n_heads):
            p, j = divmod(h, hpb)
            st_ref[h] = state[p, j * dk:(j + 1) * dk, :]


def _retention_prompt(qr, kr, vr, cos, sin, n_batch, seq, n_heads, dk, dv):
    nc = seq // CHUNK
    log_gammas = tuple(math.log1p(-2.0 ** (-RET_DECAY_BASE - h)) for h in range(n_heads))
    wq = n_heads * dk
    wv = n_heads * dv
    assert wq % LANES == 0 and LANES % dk == 0 and dv % LANES == 0
    return pl.pallas_call(
        functools.partial(_ret_kernel, n_heads=n_heads, dk=dk, dv=dv, log_gammas=log_gammas),
        out_shape=[jax.ShapeDtypeStruct((n_batch * seq, wv), F32),
                   jax.ShapeDtypeStruct((n_batch, n_heads, dk, dv), F32)],
        grid=(n_batch, nc),
        in_specs=[pl.BlockSpec((CHUNK, wq), lambda b, c: (b * nc + c, 0)),
                  pl.BlockSpec((CHUNK, wq), lambda b, c: (b * nc + c, 0)),
                  pl.BlockSpec((CHUNK, wv), lambda b, c: (b * nc + c, 0)),
                  pl.BlockSpec((CHUNK, wq), lambda b, c: (c, 0)),
                  pl.BlockSpec((CHUNK, wq), lambda b, c: (c, 0))],
        out_specs=[pl.BlockSpec((CHUNK, wv), lambda b, c: (b * nc + c, 0)),
                   pl.BlockSpec((None, n_heads, dk, dv), lambda b, c: (b, 0, 0, 0))],
        scratch_shapes=[pltpu.VMEM((wq // LANES, LANES, dv), F32)],
        compiler_params=_params("arbitrary", "arbitrary"),
        name="retention_prompt_scan",
    )(qr, kr, vr, cos, sin)


def _att_out_kernel(of_ref, or_ref, gt_ref, y_ref, gm_ref, gg_ref, gb_ref, w1_ref, w2_ref, o_ref, *, n_heads, dv):
    parts = []
    for h in range(n_heads):
        r = or_ref[:, h * dv:(h + 1) * dv]
        mu = jnp.mean(r, axis=-1, keepdims=True)
        d = r - mu
        var = jnp.mean(d * d, axis=-1, keepdims=True)
        parts.append(d * lax.rsqrt(var + 1e-5))
    r = jnp.concatenate(parts, axis=1) * gg_ref[...] + gb_ref[...]
    r = r * _silu(gt_ref[...])
    o = _dot_w(of_ref[...], w1_ref) + _dot_w(r, w2_ref)
    o_ref[...] = y_ref[...] + gm_ref[...] * o


def _att_output(rows, o_fox, o_ret, gate, y, mod, layer, gn_g, gn_b, w1, w2, n_heads, dv):
    d = y.shape[1]
    gm_arr, gm_spec = rows.mod(mod, layer, 2, d)
    return pl.pallas_call(
        functools.partial(_att_out_kernel, n_heads=n_heads, dv=dv),
        out_shape=jax.ShapeDtypeStruct(y.shape, F32),
        grid=(rows.n_tiles,),
        in_specs=[rows.spec(o_fox.shape[1]), rows.spec(o_ret.shape[1]), rows.spec(gate.shape[1]), rows.spec(d),
                  gm_spec, _full(gn_g), _full(gn_b), _full(w1), _full(w2)],
        out_specs=rows.spec(d),
        compiler_params=_params("arbitrary"),
        name="attention_output",
    )(o_fox, o_ret, gate, y, gm_arr, gn_g, gn_b, w1, w2)


def _mm_res_kernel(a_ref, y_ref, gm_ref, w_ref, o_ref):
    o_ref[...] = y_ref[...] + gm_ref[...] * _dot_w(a_ref[...], w_ref)


def _matmul_residual(rows, a, y, mod, layer, w):
    d = y.shape[1]
    gm_arr, gm_spec = rows.mod(mod, layer, 2, d)
    return pl.pallas_call(
        _mm_res_kernel,
        out_shape=jax.ShapeDtypeStruct(y.shape, F32),
        grid=(rows.n_tiles,),
        in_specs=[rows.spec(a.shape[1]), rows.spec(d), gm_spec, _full(w)],
        out_specs=rows.spec(d),
        compiler_params=_params("arbitrary"),
        name="matmul_gated_residual",
    )(a, y, gm_arr, w)


def _ssd_conv_act(taps, cw_ref, cb_ref):
    conv = cb_ref[...] + cw_ref[0:1, :] * taps[0]
    for i in range(1, len(taps)):
        conv = conv + cw_ref[i:i + 1, :] * taps[i]
    return _silu(conv)


def _ssd_gate_norm(y, x, z, dsk, ng, n_groups):
    y = (y + dsk * x) * _silu(z)
    gw = y.shape[1] // n_groups
    parts = []
    for g in range(n_groups):
        blk = y[:, g * gw:(g + 1) * gw]
        ms = jnp.mean(blk * blk, axis=-1, keepdims=True)
        parts.append(blk * lax.rsqrt(ms + 1e-5))
    return jnp.concatenate(parts, axis=1) * ng


def _ssd_kernel(xr_ref, z_ref, dt_ref, cw_ref, cb_ref, dtb_ref, alog_ref, dsk_ref, ng_ref, e_ref,
                y_ref, st_ref, cv_ref, prev, state, *, d_inner, n_groups, d_state, hd):
    c = pl.program_id(1)
    hpg = d_inner // hd // n_groups
    ppg = hpg * hd // LANES

    @pl.when(c == 0)
    def _():
        prev[...] = jnp.zeros_like(prev)
        state[...] = jnp.zeros_like(state)

    cur = xr_ref[...]
    xw = jnp.concatenate([prev[...], cur], axis=0)
    taps = [xw[5 + i:5 + i + CHUNK] for i in range(3)] + [cur]
    tail_rows = cur[CHUNK - 8:CHUNK]
    prev[...] = tail_rows
    cv_ref[...] = tail_rows
    xbc = _ssd_conv_act(taps, cw_ref, cb_ref)
    gn = n_groups * d_state
    x = xbc[:, :d_inner]
    bm = xbc[:, d_inner:d_inner + gn].astype(BF16)
    cm = xbc[:, d_inner + gn:].astype(BF16)

    dt = _softplus(dt_ref[...] + dtb_ref[...])
    la = -dt * jnp.exp(alog_ref[...])
    seg = _iota2((CHUNK, CHUNK), 0) - _iota2((CHUNK, CHUNK), 1)
    causal = seg >= 0
    cum = _dot_sel_lhs(causal.astype(BF16), la)
    cum_t = cum.T
    e = e_ref[...]
    dt_e = _dot_sel_rhs(dt, e)
    cum_e = _dot_sel_rhs(cum, e)
    ecum_e = jnp.exp(cum_e)
    tail_e = jnp.exp(cum_e[CHUNK - 1:CHUNK, :] - cum_e)
    v = x * dt_e
    vb = v.astype(BF16)
    vtb = (v * tail_e).astype(BF16)
    lane = _iota2((1, LANES), 1)
    hpb = LANES // hd
    ys = []
    for g in range(n_groups):
        cmg = cm[:, g * d_state:(g + 1) * d_state]
        bmg = bm[:, g * d_state:(g + 1) * d_state]
        s = _dot_nt(cmg, bmg)
        for pp in range(ppg):
            p = g * ppg + pp
            cols = slice(p * LANES, (p + 1) * LANES)
            v2 = vb[:, cols]
            yp = None
            for j in range(hpb):
                h = p * hpb + j
                d = jnp.exp(jnp.where(causal, cum[:, h:h + 1] - cum_t[h:h + 1, :], NEG_INF))
                yj = _dot((s * d).astype(BF16), v2)
                sel = lane // hd == j
                yp = jnp.where(sel, yj, 0.0) if yp is None else jnp.where(sel, yj, yp)
            st2 = state[p]
            yp = yp + ecum_e[:, cols] * _dot(cmg, st2.astype(BF16))
            ys.append(yp)
            state[p] = ecum_e[CHUNK - 1:CHUNK, cols] * st2 + _dot_tn(bmg, vtb[:, cols])
    y = jnp.concatenate(ys, axis=1)
    y_ref[...] = _ssd_gate_norm(y, x, z_ref[...], dsk_ref[...], ng_ref[...], n_groups)

    @pl.when(c == pl.num_programs(1) - 1)
    def _():
        st_ref[...] = state[...]


def _ssd_prompt(xr, z, dtr, cw, cb, dtb, alog, dsk, ng, e_mat, n_batch, seq, d_inner, n_groups, d_state, hd):
    nc = seq // CHUNK
    cd = xr.shape[1]
    n_blk = d_inner // LANES
    row = lambda b, c: (b * nc + c, 0)
    return pl.pallas_call(
        functools.partial(_ssd_kernel, d_inner=d_inner, n_groups=n_groups, d_state=d_state, hd=hd),
        out_shape=[jax.ShapeDtypeStruct((n_batch * seq, d_inner), F32),
                   jax.ShapeDtypeStruct((n_batch, n_blk, d_state, LANES), F32),
                   jax.ShapeDtypeStruct((n_batch, 8, cd), F32)],
        grid=(n_batch, nc),
        in_specs=[pl.BlockSpec((CHUNK, cd), row), pl.BlockSpec((CHUNK, d_inner), row),
                  pl.BlockSpec((CHUNK, LANES), row),
                  _full(cw), _full(cb), _full(dtb), _full(alog), _full(dsk), _full(ng), _full(e_mat)],
        out_specs=[pl.BlockSpec((CHUNK, d_inner), row),
                   pl.BlockSpec((None, n_blk, d_state, LANES), lambda b, c: (b, 0, 0, 0)),
                   pl.BlockSpec((None, 8, cd), lambda b, c: (b, 0, 0))],
        scratch_shapes=[pltpu.VMEM((8, cd), F32), pltpu.VMEM((n_blk, d_state, LANES), F32)],
        compiler_params=_params("arbitrary", "arbitrary"),
        name="ssd_prompt_scan",
    )(xr, z, dtr, cw, cb, dtb, alog, dsk, ng, e_mat)


def _ssd_step_prep_kernel(xr_ref, c0_ref, c1_ref, c2_ref, dt_ref, cw_ref, cb_ref, dtb_ref, alog_ref, e_ref,
                          x_ref, bm_ref, cm_ref, v_ref, a_ref, *, d_inner, gn):
    xbc = _ssd_conv_act([c0_ref[...], c1_ref[...], c2_ref[...], xr_ref[...]], cw_ref, cb_ref)
    x = xbc[:, :d_inner]
    x_ref[...] = x
    bm_ref[...] = xbc[:, d_inner:d_inner + gn]
    cm_ref[...] = xbc[:, d_inner + gn:]
    dt = _softplus(dt_ref[...] + dtb_ref[...])
    a_ref[...] = jnp.exp(-dt * jnp.exp(alog_ref[...]))
    v_ref[...] = x * _dot_sel_rhs(dt, e_ref[...])


def _ssd_step_prep(xr, taps, dtr, cw, cb, dtb, alog, e_mat, d_inner, gn):
    n = xr.shape[0]
    args = (xr, *taps, dtr, cw, cb, dtb, alog, e_mat)
    return pl.pallas_call(
        functools.partial(_ssd_step_prep_kernel, d_inner=d_inner, gn=gn),
        out_shape=[jax.ShapeDtypeStruct((n, d_inner), F32), jax.ShapeDtypeStruct((n, gn), F32),
                   jax.ShapeDtypeStruct((n, gn), F32), jax.ShapeDtypeStruct((n, d_inner), F32),
                   jax.ShapeDtypeStruct((n, LANES), F32)],
        grid=(1,),
        in_specs=[_full(a) for a in args],
        out_specs=[pl.BlockSpec((n, d_inner), lambda i: (0, 0)), pl.BlockSpec((n, gn), lambda i: (0, 0)),
                   pl.BlockSpec((n, gn), lambda i: (0, 0)), pl.BlockSpec((n, d_inner), lambda i: (0, 0)),
                   pl.BlockSpec((n, LANES), lambda i: (0, 0))],
        compiler_params=_params("arbitrary"),
        name="ssd_step_prep",
    )(*args)


def _ssd_step_post_kernel(y_ref, x_ref, z_ref, dsk_ref, ng_ref, o_ref, *, n_groups):
    o_ref[...] = _ssd_gate_norm(y_ref[...], x_ref[...], z_ref[...], dsk_ref[...], ng_ref[...], n_groups)


def _ssd_step_post(y, x, z, dsk, ng, n_groups):
    args = (y, x, z, dsk, ng)
    return pl.pallas_call(
        functools.partial(_ssd_step_post_kernel, n_groups=n_groups),
        out_shape=jax.ShapeDtypeStruct(y.shape, F32),
        grid=(1,),
        in_specs=[_full(a) for a in args],
        out_specs=pl.BlockSpec(y.shape, lambda i: (0, 0)),
        compiler_params=_params("arbitrary"),
        name="ssd_step_post",
    )(*args)


def _state_step_kernel(s_ref, a_ref, k_ref, q_ref, v_ref, so_ref, y_ref, *, n_heads, heads_per_key):
    for h in range(n_heads):
        g = h // heads_per_key
        new = a_ref[h] * s_ref[h] + k_ref[g] * v_ref[h]
        so_ref[h] = new
        y_ref[h] = jnp.sum(q_ref[g] * new, axis=0, keepdims=True)


def _state_step(state, a, k_col, q_col, v_row):
    n_b, n_h, n_n, n_v = state.shape
    n_g = k_col.shape[1]
    blk = lambda arr: pl.BlockSpec((None,) + arr.shape[1:], lambda b: (b, 0, 0, 0))
    return pl.pallas_call(
        functools.partial(_state_step_kernel, n_heads=n_h, heads_per_key=n_h // n_g),
        out_shape=[jax.ShapeDtypeStruct(state.shape, F32), jax.ShapeDtypeStruct((n_b, n_h, 1, n_v), F32)],
        grid=(n_b,),
        in_specs=[blk(state), blk(a), blk(k_col), blk(q_col), blk(v_row)],
        out_specs=[blk(state), pl.BlockSpec((None, n_h, 1, n_v), lambda b: (b, 0, 0, 0))],
        compiler_params=_params("arbitrary"),
        name="state_step",
    )(state, a, k_col, q_col, v_row)


def _state_step_t_kernel(s_ref, a_ref, k_ref, q_ref, v_ref, so_ref, y_ref, *, n_heads, heads_per_key):
    n_v = s_ref.shape[1]
    eye_v = (_iota2((n_v, n_v), 0) == _iota2((n_v, n_v), 1)).astype(BF16)
    v_t = sum(_dot_nt(eye_v, part) for part in _split3(v_ref[...]))
    a_row = a_ref[...]
    lane = _iota2((1, LANES), 1)
    y_t = jnp.zeros((n_v, LANES), F32)
    for h in range(n_heads):
        g = h // heads_per_key
        new = a_row[:, h:h + 1] * s_ref[h] + v_t[:, h:h + 1] * k_ref[g:g + 1, :]
        so_ref[h] = new
        y_t = jnp.where(lane == h, jnp.sum(new * q_ref[g:g + 1, :], axis=1, keepdims=True), y_t)
    eye_l = (_iota2((LANES, LANES), 0) == _iota2((LANES, LANES), 1)).astype(BF16)
    y = sum(_dot_nt(eye_l, part) for part in _split3(y_t))
    y_ref[...] = y[:n_heads]


def _state_step_t(state_t, a, k_row, q_row, v_row):
    n_b, n_h, n_v, n_n = state_t.shape
    n_g = k_row.shape[1]
    assert n_h <= LANES
    blk = lambda arr: pl.BlockSpec((None,) + arr.shape[1:], lambda b: (b,) + (0,) * (arr.ndim - 1))
    return pl.pallas_call(
        functools.partial(_state_step_t_kernel, n_heads=n_h, heads_per_key=n_h // n_g),
        out_shape=[jax.ShapeDtypeStruct(state_t.shape, F32), jax.ShapeDtypeStruct((n_b, n_h, n_v), F32)],
        grid=(n_b,),
        in_specs=[blk(state_t), blk(a), blk(k_row), blk(q_row), blk(v_row)],
        out_specs=[blk(state_t), pl.BlockSpec((None, n_h, n_v), lambda b: (b, 0, 0))],
        compiler_params=_params("arbitrary"),
        name="state_step_transposed",
    )(state_t, a, k_row, q_row, v_row)


def _rope_rows_kernel(q_ref, k_ref, cos_ref, sin_ref, qo_ref, ko_ref, *, dk):
    for p in range(q_ref.shape[1] // LANES):
        cols = slice(p * LANES, (p + 1) * LANES)
        qo_ref[:, cols] = _rope(q_ref[:, cols], cos_ref[:, cols], sin_ref[:, cols], dk)
        ko_ref[:, cols] = _rope(k_ref[:, cols], cos_ref[:, cols], sin_ref[:, cols], dk) * (dk ** -0.5)


def _rope_rows(q, k, cos, sin, dk):
    args = (q, k, cos, sin)
    return pl.pallas_call(
        functools.partial(_rope_rows_kernel, dk=dk),
        out_shape=[jax.ShapeDtypeStruct(q.shape, F32)] * 2,
        grid=(1,),
        in_specs=[_full(a) for a in args],
        out_specs=[pl.BlockSpec(q.shape, lambda i: (0, 0))] * 2,
        compiler_params=_params("arbitrary"),
        name="rope_rows",
    )(*args)


def _fox_decode_kernel(pt_ref, q_ref, kn_ref, vn_ref, lfn_ref, kt_hbm, vt_hbm, lf_hbm, o_ref,
                       kbuf, vbuf, lbuf, sem, qb_ref, acc_ref, *, layer, n_pages, group, n_slots, n_heads, dh, page):
    b = pl.program_id(0)
    nb = pl.num_programs(0)
    n_groups = n_pages // group
    hd = n_heads * dh

    def copies(bb, gi, slot):
        out = []
        for g in range(group):
            pid = pt_ref[bb, gi * group + g]
            out.append(pltpu.make_async_copy(kt_hbm.at[layer, pid], kbuf.at[slot, g], sem.at[slot, 0]))
            out.append(pltpu.make_async_copy(vt_hbm.at[layer, pid], vbuf.at[slot, g], sem.at[slot, 1]))
            out.append(pltpu.make_async_copy(lf_hbm.at[layer, pid], lbuf.at[slot, g], sem.at[slot, 2]))
        return out

    def start(bb, gi, slot):
        for c in copies(bb, gi, slot):
            c.start()

    def wait(bb, gi, slot):
        for c in copies(bb, gi, slot):
            c.wait()

    @pl.when(b == 0)
    def _():
        for g in range(n_slots - 1):
            start(b, g, g)

    q_row = q_ref[...] * (dh ** -0.5)
    qb_ref[...] = jnp.broadcast_to(q_row, (page, hd)).T.reshape(n_heads, dh, page)
    acc_ref[...] = jnp.zeros_like(acc_ref)
    tri = (_iota2((page, page), 0) <= _iota2((page, page), 1)).astype(BF16)

    def page_update(slot, g, carry):
        m, l, run = carry
        s = jnp.concatenate(
            [jnp.sum(kbuf[slot, g, h] * qb_ref[h], axis=0, keepdims=True) for h in range(n_heads)], axis=0)
        fcum = _dot_sel_rhs(lbuf[slot, g], tri) + run
        s = s - fcum
        m_new = jnp.maximum(m, jnp.max(s, axis=-1, keepdims=True))
        alpha = jnp.exp(m - m_new)
        pe = jnp.exp(s - m_new)
        l_new = alpha * l + jnp.sum(pe, axis=-1, keepdims=True)
        for h in range(n_heads):
            acc_ref[h] = alpha[h:h + 1, :] * acc_ref[h] + pe[h:h + 1, :] * vbuf[slot, g, h]
        return m_new, l_new, fcum[:, page - 1:page]

    def ring(gq, carry):
        for s in range(n_slots):
            gi = gq * n_slots + s
            wait(b, gi, s)
            ahead = gi + n_slots - 1
            refill = (s + n_slots - 1) % n_slots

            @pl.when(ahead < n_groups)
            def _():
                start(b, ahead, refill)

            @pl.when(jnp.logical_and(ahead >= n_groups, b + 1 < nb))
            def _():
                start(b + 1, ahead - n_groups, refill)

            for g in range(group):
                carry = page_update(s, g, carry)
        return carry

    init = (jnp.full((n_heads, 1), NEG_INF, F32), jnp.zeros((n_heads, 1), F32), jnp.zeros((n_heads, 1), F32))
    m, l, f_tot = lax.fori_loop(0, n_groups // n_slots, ring, init)

    own = (_iota2((n_heads, hd), 1) // dh) == _iota2((n_heads, hd), 0)
    spread = lambda col: jnp.sum(jnp.where(own, col, 0.0), axis=0, keepdims=True)
    lfn = jnp.sum(jnp.where(_iota2((n_heads, LANES), 1) == _iota2((n_heads, LANES), 0), lfn_ref[...], 0.0),
                  axis=1, keepdims=True)
    s_new = jnp.sum(jnp.where(own, q_row * kn_ref[...], 0.0), axis=1, keepdims=True) - (f_tot + lfn)
    m_fin = jnp.maximum(m, s_new)
    a2 = jnp.exp(m - m_fin)
    p_new = jnp.exp(s_new - m_fin)
    l_fin = a2 * l + p_new
    o_past = jnp.sum(acc_ref[...].reshape(hd, page).T, axis=0, keepdims=True)
    o_ref[...] = (spread(a2) * o_past + spread(p_new) * vn_ref[...]) / spread(l_fin)


def _fox_decode(page_table, q, k_new, v_new, lf_new, kt_pages, vt_pages, lf_pages_t, layer):
    n_b, _, hd = q.shape
    n_pages = page_table.shape[1]
    _, _, n_h, dh, page = kt_pages.shape
    group = max(1, min(4, n_pages // 2))
    n_slots = 4 if n_pages % (4 * group) == 0 else 2
    assert n_pages % (n_slots * group) == 0
    tok = lambda arr: pl.BlockSpec((None, 1, arr.shape[2]), lambda b, pt: (b, 0, 0))
    grid_spec = pltpu.PrefetchScalarGridSpec(
        num_scalar_prefetch=1,
        grid=(n_b,),
        in_specs=[tok(q), tok(k_new), tok(v_new), tok(lf_new),
                  pl.BlockSpec(memory_space=pl.ANY), pl.BlockSpec(memory_space=pl.ANY),
                  pl.BlockSpec(memory_space=pl.ANY)],
        out_specs=pl.BlockSpec((None, 1, hd), lambda b, pt: (b, 0, 0)),
        scratch_shapes=[pltpu.VMEM((n_slots, group, n_h, dh, page), F32),
                        pltpu.VMEM((n_slots, group, n_h, dh, page), F32),
                        pltpu.VMEM((n_slots, group, n_h, page), F32), pltpu.SemaphoreType.DMA((n_slots, 3)),
                        pltpu.VMEM((n_h, dh, page), F32), pltpu.VMEM((n_h, dh, page), F32)],
    )
    return pl.pallas_call(
        functools.partial(_fox_decode_kernel, layer=layer, n_pages=n_pages, group=group, n_slots=n_slots,
                          n_heads=n_h, dh=dh, page=page),
        out_shape=jax.ShapeDtypeStruct((n_b, 1, hd), F32),
        grid_spec=grid_spec,
        compiler_params=_params("arbitrary"),
        name="fox_paged_decode",
    )(page_table, q, k_new, v_new, lf_new, kt_pages, vt_pages, lf_pages_t)


def _store_row_tiles(ref, val):
    n, d = val.shape
    nb = d // LANES
    for c in range(nb):
        ref[pl.ds(c, n, stride=nb), :] = val[:, c * LANES:(c + 1) * LANES]


def _load_row_tiles(ref, n, nb):
    return jnp.concatenate([ref[pl.ds(c, n, stride=nb), :] for c in range(nb)], axis=1)


def _router_kernel(y_ref, g_ref, sh_ref, sc_ref, rwt_ref, rb_ref, cin_ref,
                   h_ref, idx_ref, w_ref, rk_ref, cnt_ref, carry, *, n_exp):
    i = pl.program_id(0)

    @pl.when(i == 0)
    def _():
        carry[...] = cin_ref[...]

    h = _modulate(y_ref[...], g_ref[...], sh_ref[...], sc_ref[...])
    _store_row_tiles(h_ref, h)
    tm = h.shape[0]
    logits = _dot3(rwt_ref[...], h, dot=_dot_nt) + rb_ref[...]
    eio = _iota2((n_exp, tm), 0)
    vals, idxs = [], []
    rest = logits
    for _ in range(TOP_K):
        m = jnp.max(rest, axis=0, keepdims=True)
        ik = jnp.min(jnp.where(rest == m, eio, n_exp), axis=0, keepdims=True)
        vals.append(m)
        idxs.append(ik)
        rest = jnp.where(eio == ik, NEG_INF, rest)
    ex = [jnp.exp(v - vals[0]) for v in vals]
    den = ex[0]
    for e in ex[1:]:
        den = den + e
    sel = jnp.zeros((n_exp, tm), F32)
    for ik in idxs:
        sel = sel + jnp.where(eio == ik, 1.0, 0.0)
    before = (_iota2((tm, tm), 0) < _iota2((tm, tm), 1)).astype(BF16)
    rank_all = _dot(sel.astype(BF16), before) + carry[:, 0:1]
    ranks = [jnp.sum(jnp.where(eio == ik, rank_all, 0.0), axis=0, keepdims=True) for ik in idxs]
    carry[...] = carry[...] + jnp.sum(sel, axis=1, keepdims=True)
    idx_ref[...] = jnp.concatenate(idxs, axis=0)
    w_ref[...] = jnp.concatenate([e / den for e in ex], axis=0)
    rk_ref[...] = jnp.concatenate(ranks, axis=0).astype(I32)
    cnt_ref[...] = carry[...]


def _route(rows, y, g, mod, layer, rw_t, rb_col, cnt_in):
    d = y.shape[1]
    n_exp = rw_t.shape[0]
    sh_arr, sh_spec = rows.mod(mod, layer, 3, d)
    sc_arr, sc_spec = rows.mod(mod, layer, 4, d)
    kt = pl.BlockSpec((TOP_K, rows.tile), lambda i: (0, i))
    nb = d // LANES
    return pl.pallas_call(
        functools.partial(_router_kernel, n_exp=n_exp),
        out_shape=[jax.ShapeDtypeStruct((rows.rows * nb, LANES), F32),
                   jax.ShapeDtypeStruct((TOP_K, rows.rows), I32),
                   jax.ShapeDtypeStruct((TOP_K, rows.rows), F32),
                   jax.ShapeDtypeStruct((TOP_K, rows.rows), I32),
                   jax.ShapeDtypeStruct((n_exp, LANES), F32)],
        grid=(rows.n_tiles,),
        in_specs=[rows.spec(d), _full(g), sh_spec, sc_spec, _full(rw_t), _full(rb_col), _full(cnt_in)],
        out_specs=[pl.BlockSpec((rows.tile * nb, LANES), lambda i: (i, 0)), kt, kt, kt,
                   pl.BlockSpec((n_exp, LANES), lambda i: (0, 0))],
        scratch_shapes=[pltpu.VMEM((n_exp, LANES), F32)],
        compiler_params=_params("arbitrary"),
        name="moe_router",
    )(y, g, sh_arr, sc_arr, rw_t, rb_col, cnt_in)


def _dispatch_kernel(slot_ref, slot2_ref, pad_ref, nv_ref, h_ref, h2_ref, xs_ref, sem, zeros,
                     *, tm, n_rows, n_rows2, nb, n_exp, tmf, n_tiles):
    i = pl.program_id(0)

    @pl.when(i == 0)
    def _():
        zeros[...] = jnp.zeros_like(zeros)
        fills = [pltpu.make_async_copy(zeros, xs_ref.at[pl.ds(pl.multiple_of(pad_ref[e] * nb, nb), tmf * nb), :], sem)
                 for e in range(n_exp)]
        for c in fills:
            c.start()
        for c in fills:
            c.wait()

        def tail(j):
            return pltpu.make_async_copy(zeros, xs_ref.at[pl.ds(pl.multiple_of(j * (tmf * nb), tmf * nb), tmf * nb), :], sem)

        def start_tail(j, carry):
            tail(j).start()
            return carry

        def wait_tail(j, carry):
            tail(j).wait()
            return carry

        lax.fori_loop(nv_ref[0], n_tiles, start_tail, 0)
        lax.fori_loop(nv_ref[0], n_tiles, wait_tail, 0)

    def scatter(src_ref, slots, base, count, stride):
        def issue(t, carry):
            for k in range(TOP_K):
                s = slots[k * stride + base + t]
                pltpu.make_async_copy(src_ref.at[pl.ds(pl.multiple_of(t * nb, nb), nb), :],
                                      xs_ref.at[pl.ds(pl.multiple_of(s * nb, nb), nb), :], sem).start(priority=k % 2)
            return carry

        lax.fori_loop(0, count, issue, 0, unroll=2)
        for k in range(TOP_K):
            pltpu.make_async_copy(src_ref, xs_ref.at[pl.ds(0, count * nb), :], sem).wait()

    scatter(h_ref, slot_ref, i * tm, tm, n_rows)

    @pl.when(i == pl.num_programs(0) - 1)
    def _():
        scatter(h2_ref, slot2_ref, 0, n_rows2, n_rows2)


def _dispatch(rows, slots, slots2, pad_start, n_valid, h, h2, n_tiles, tmf, nb):
    n_exp = pad_start.shape[0]
    n_rows2 = h2.shape[0] // nb
    grid_spec = pltpu.PrefetchScalarGridSpec(
        num_scalar_prefetch=4,
        grid=(rows.n_tiles,),
        in_specs=[pl.BlockSpec((rows.tile * nb, LANES), lambda i, *_: (i, 0)),
                  pl.BlockSpec(h2.shape, lambda i, *_: (0, 0))],
        out_specs=pl.BlockSpec(memory_space=pl.ANY),
        scratch_shapes=[pltpu.SemaphoreType.DMA(()), pltpu.VMEM((tmf * nb, LANES), F32)],
    )
    return pl.pallas_call(
        functools.partial(_dispatch_kernel, tm=rows.tile, n_rows=rows.rows, n_rows2=n_rows2, nb=nb, n_exp=n_exp,
                          tmf=tmf, n_tiles=n_tiles),
        out_shape=jax.ShapeDtypeStruct((n_tiles * tmf * nb, LANES), F32),
        grid_spec=grid_spec,
        compiler_params=_params("arbitrary"),
        name="moe_dispatch",
    )(slots, slots2, pad_start, n_valid, h, h2)


def _split_gate_up_kernel(w_ref, wg_ref, wu_ref, wt_ref, *, chunk):
    half = chunk // 2
    n_r = w_ref.shape[0] // LANES
    for c in range(w_ref.shape[1] // chunk):
        for r in range(n_r):
            rows = slice(r * LANES, (r + 1) * LANES)
            wt = wt_ref.at[(c * n_r + r) % wt_ref.shape[0]]
            wt[...] = w_ref[rows, c * chunk:(c + 1) * chunk].T
            wg_ref[rows, c * half:(c + 1) * half] = wt[pl.ds(0, half, stride=2), :].T.astype(BF16)
            wu_ref[rows, c * half:(c + 1) * half] = wt[pl.ds(1, half, stride=2), :].T.astype(BF16)


def _split_gate_up(w_up):
    n_l, n_e, d, f2 = w_up.shape
    chunk = min(512, f2)
    blk = lambda width: pl.BlockSpec((None, None, d, width), lambda l, e: (l, e, 0, 0))
    return pl.pallas_call(
        functools.partial(_split_gate_up_kernel, chunk=chunk),
        out_shape=[jax.ShapeDtypeStruct((n_l, n_e, d, f2 // 2), BF16)] * 2,
        grid=(n_l, n_e),
        in_specs=[blk(f2)],
        out_specs=[blk(f2 // 2)] * 2,
        scratch_shapes=[pltpu.VMEM((4, chunk, LANES), F32)],
        compiler_params=_params("arbitrary", "arbitrary"),
        name="moe_split_gate_up",
    )(w_up)


def _ffn_kernel(te_ref, tv_ref, nv_ref, xs_ref, wg_ref, wu_ref, bg_ref, bu_ref, wd_ref, bd_ref, ys_ref, *, tmf, nb):
    del te_ref, nv_ref
    i = pl.program_id(0)

    @pl.when(tv_ref[i] == 1)
    def _():
        x = _load_row_tiles(xs_ref, tmf, nb).astype(BF16)
        gate = jnp.minimum(_dot(x, wg_ref[...]) + bg_ref[...], SWIGLU_LIMIT)
        up = jnp.clip(_dot(x, wu_ref[...]) + bu_ref[...], -SWIGLU_LIMIT, SWIGLU_LIMIT)
        act = (up + 1.0) * gate * _sigmoid(SWIGLU_ALPHA * gate)
        _store_row_tiles(ys_ref, _dot(act.astype(BF16), wd_ref[...].astype(BF16)) + bd_ref[...])

    @pl.when(tv_ref[i] == 0)
    def _():
        ys_ref[...] = jnp.zeros_like(ys_ref)


def _expert_ffn(tile_expert, tile_valid, n_valid, xs, layer, wg, wu, bg, bu, wd, bd, tmf, nb):
    n_tiles = tile_expert.shape[0]
    d, f = wg.shape[2], wg.shape[3]
    ex = lambda i, te, tv, nv: (te[i], 0, 0)
    lex = lambda i, te, tv, nv: (layer, te[i], 0, 0)
    grid_spec = pltpu.PrefetchScalarGridSpec(
        num_scalar_prefetch=3,
        grid=(n_tiles,),
        in_specs=[pl.BlockSpec((tmf * nb, LANES), lambda i, te, tv, nv: (jnp.minimum(i, nv[0] - 1), 0)),
                  pl.BlockSpec((None, None, d, f), lex), pl.BlockSpec((None, None, d, f), lex),
                  pl.BlockSpec((None, 1, f), ex), pl.BlockSpec((None, 1, f), ex),
                  pl.BlockSpec((None, None, f, d), lex), pl.BlockSpec((None, 1, d), ex)],
        out_specs=pl.BlockSpec((tmf * nb, LANES), lambda i, te, tv, nv: (i, 0)),
    )
    return pl.pallas_call(
        functools.partial(_ffn_kernel, tmf=tmf, nb=nb),
        out_shape=jax.ShapeDtypeStruct((n_tiles * tmf * nb, LANES), F32),
        grid_spec=grid_spec,
        compiler_params=_params("arbitrary"),
        name="moe_expert_ffn",
    )(tile_expert, tile_valid, n_valid, xs, wg, wu, bg, bu, wd, bd)


def _combine_kernel(slot_ref, ys_ref, w_ref, y_ref, gm_ref, fg_ref, o_ref, buf, sem, *, tm, n_rows, nb, final_norm):
    i = pl.program_id(0)
    n = pl.num_programs(0)

    def gather(tile, slot):
        def issue(t, carry):
            for k in range(TOP_K):
                s = slot_ref[k * n_rows + tile * tm + t]
                pltpu.make_async_copy(ys_ref.at[pl.ds(pl.multiple_of(s * nb, nb), nb), :],
                                      buf.at[slot, k, pl.ds(pl.multiple_of(t * nb, nb), nb), :],
                                      sem.at[slot]).start(priority=k % 2)
            return carry

        lax.fori_loop(0, tm, issue, 0, unroll=2)

    @pl.when(i == 0)
    def _():
        gather(0, 0)

    cur = i % 2

    @pl.when(i + 1 < n)
    def _():
        gather(i + 1, 1 - cur)

    for k in range(TOP_K):
        pltpu.make_async_copy(ys_ref.at[pl.ds(0, tm * nb), :], buf.at[cur, k], sem.at[cur]).wait()
    ws = [jnp.broadcast_to(w_ref[:, k:k + 1], (tm, LANES)) for k in range(TOP_K)]
    for c in range(nb):
        cols = slice(c * LANES, (c + 1) * LANES)
        acc = ws[0] * buf[cur, 0, pl.ds(c, tm, stride=nb), :]
        for k in range(1, TOP_K):
            acc = acc + ws[k] * buf[cur, k, pl.ds(c, tm, stride=nb), :]
        o_ref[:, cols] = y_ref[:, cols] + gm_ref[:, cols] * acc
    if final_norm:
        x = o_ref[...]
        o_ref[...] = x * lax.rsqrt(jnp.mean(x * x, axis=-1, keepdims=True) + 1e-6) * fg_ref[...]


def _combine(rows, slots_flat, ys, w_tok, y, mod, layer, final_g, final_norm):
    d = y.shape[1]
    nb = d // LANES
    gm_arr, gm_spec = rows.mod(mod, layer, 5, d)
    grid_spec = pltpu.PrefetchScalarGridSpec(
        num_scalar_prefetch=1,
        grid=(rows.n_tiles,),
        in_specs=[pl.BlockSpec(memory_space=pl.ANY),
                  pl.BlockSpec((rows.tile, TOP_K), lambda i, s: (i, 0)),
                  pl.BlockSpec((rows.tile, d), lambda i, s: (i, 0)), gm_spec,
                  pl.BlockSpec(final_g.shape, lambda i, s: (0, 0))],
        out_specs=pl.BlockSpec((rows.tile, d), lambda i, s: (i, 0)),
        scratch_shapes=[pltpu.VMEM((2, TOP_K, rows.tile * nb, LANES), F32), pltpu.SemaphoreType.DMA((2,))],
    )
    return pl.pallas_call(
        functools.partial(_combine_kernel, tm=rows.tile, n_rows=rows.rows, nb=nb, final_norm=final_norm),
        out_shape=jax.ShapeDtypeStruct(y.shape, F32),
        grid_spec=grid_spec,
        compiler_params=_params("arbitrary"),
        name="moe_combine",
    )(slots_flat, ys, w_tok, y, gm_arr, final_g)


def _moe_layer(groups, ys, layer, g_ffn, router_w, router_b, wg_all, wu_all, exp_b_up, wd_all, exp_b_down, tmf,
               final_g, final_norm):
    n_exp = router_w.shape[1]
    d = router_w.shape[0]
    rw_t = router_w.T
    rb_col = router_b.reshape(n_exp, 1)
    cnt = jnp.zeros((n_exp, LANES), F32)
    routed = []
    for (rows, m), y in zip(groups, ys):
        h, idx, w, rank, cnt = _route(rows, y, g_ffn, m, layer, rw_t, rb_col, cnt)
        routed.append((h, idx, w, rank))
    total = sum(rows.rows for rows, _ in groups) * TOP_K
    n_tiles = -(-total // tmf) + n_exp
    counts = cnt[:, 0].astype(I32)
    tiles_e = (counts + tmf - 1) // tmf
    tile_end = jnp.cumsum(tiles_e)
    starts = (tile_end - tiles_e) * tmf
    tile_ids = jnp.arange(n_tiles, dtype=I32)
    tile_expert = jnp.minimum(jnp.sum((tile_end[None, :] <= tile_ids[:, None]).astype(I32), axis=1), n_exp - 1)
    tile_valid = (tile_ids < tile_end[-1]).astype(I32)
    nb = d // LANES
    pad_start = starts + counts
    e_ids = jnp.arange(n_exp, dtype=I32)
    slots = [(jnp.sum(jnp.where(idx[..., None] == e_ids, starts, 0), axis=-1) + rank).reshape(-1)
             for (h, idx, w, rank) in routed]
    xs = _dispatch(groups[0][0], slots[0], slots[1], pad_start, tile_end[-1:], routed[0][0], routed[1][0],
                   n_tiles + 1, tmf, nb)
    bg = exp_b_up[:, None, 0::2]
    bu = exp_b_up[:, None, 1::2]
    y_sorted = _expert_ffn(tile_expert, tile_valid, tile_end[-1:], xs, layer, wg_all, wu_all, bg, bu,
                           wd_all, exp_b_down[:, None, :], tmf, nb)
    outs = []
    for (rows, m), y, sl, (h, idx, w, rank) in zip(groups, ys, slots, routed):
        outs.append(_combine(rows, sl, y_sorted, w.T, y, m, layer, final_g, final_norm))
    return outs


def kernel(x_prompt, x_sample, c_prompt, c_sample, cache_k, cache_v, cache_logf, page_table, state_ret, state_ssm, state_conv, ada_w, ada_b, norm_mix_g, norm_ffn_g, norm_final_g, att_w_in, att_b_f, ret_gn_g, ret_gn_b, att_w_out, ssm_w_in, ssm_conv_w, ssm_conv_b, ssm_dt_bias, ssm_a_log, ssm_d, ssm_norm_g, ssm_w_out, router_w, router_b, exp_w_up, exp_b_up, exp_w_down, exp_b_down):
    bp, sp, d = x_prompt.shape
    bs = x_sample.shape[0]
    depth = ada_w.shape[0]
    n_pages, page = page_table.shape[1], cache_k.shape[2]
    past = n_pages * page
    h_a, dh_a = cache_k.shape[3], cache_k.shape[4]
    h_b, dk_b, dv_b = state_ret.shape[2:]
    h_c, d_state, hd_c = state_ssm.shape[2:]
    d_inner = h_c * hd_c
    conv_w_len, conv_dim = ssm_conv_w.shape[1:]
    gn = (conv_dim - d_inner) // 2
    n_groups = gn // d_state
    assert conv_w_len == 4 and hd_c * 2 == LANES and h_a <= LANES and h_c <= LANES

    rows_p = _Rows(bp, sp, 256)
    rows_m = _Rows(bp, sp, 512)
    rows_s = _Rows(bs, 1, bs)
    yp = x_prompt.reshape(bp * sp, d)
    ys = x_sample.reshape(bs, d)

    mod = _modulation_all(jnp.concatenate([c_prompt, c_sample], axis=0), ada_w, ada_b)
    mod_p, mod_s = mod[:, :bp], mod[:, bp:]

    wa, wr, wvr = h_a * dh_a, h_b * dk_b, h_b * dv_b
    att_segs = []
    off = 0
    for wdt in (wa, wa, wa, wr, wr, wvr, wvr, LANES):
        att_segs.append((off, wdt))
        off += wdt
    ssm_segs = ((0, d_inner), (d_inner, conv_dim), (d_inner + conv_dim, LANES))

    cos_p, sin_p = _rope_tables(jnp.arange(sp), h_b, dk_b)
    cos_s, sin_s = _rope_tables(jnp.full((1,), past), h_b, dk_b)
    log_gammas = [math.log1p(-2.0 ** (-RET_DECAY_BASE - h)) for h in range(h_b)]
    gamma_col = jnp.broadcast_to(jnp.asarray(np.exp(log_gammas), F32).reshape(1, h_b, 1, 1), (bs, h_b, 1, 1))
    kt_pages = jnp.transpose(cache_k, (0, 1, 3, 4, 2))
    vt_pages = jnp.transpose(cache_v, (0, 1, 3, 4, 2))
    lf_pages_t = jnp.swapaxes(cache_logf, 2, 3)
    n_exp = router_w.shape[2]
    tmf = 512 if (bp * sp + bs) * TOP_K >= 512 * n_exp else 64
    e_mat = (jnp.arange(LANES)[:, None] == (jnp.arange(d_inner) // hd_c)[None, :]).astype(BF16)
    wg_all, wu_all = _split_gate_up(exp_w_up)

    k_p, v_p, f_p, k_s, v_s, f_s, ret_p, ret_s = [], [], [], [], [], [], [], []
    ssm_p, ssm_s, conv_p, conv_s = [], [], [], []
    for l in range(depth):
        j = l // 2
        g_mix = norm_mix_g[l][None]
        if l % 2 == 0:
            w = att_w_in[j]
            o0 = 3 * wa
            n_att = att_segs[-1][0] + LANES
            w_perm32 = jnp.concatenate(
                [w[:, :o0], w[:, o0 + h_a:], w[:, o0:o0 + h_a],
                 jnp.zeros((d, -(-n_att // F32_COL_BLOCK) * F32_COL_BLOCK - n_att + LANES - h_a), F32)], axis=1)
            w_perm = w_perm32[:, :n_att].astype(BF16)
            b_f_pad = jnp.pad(att_b_f[j], (0, LANES - h_a))[None]
            w1_32, w2_32 = att_w_out[j][:wa], att_w_out[j][wa:]
            w1, w2 = w1_32.astype(BF16), w2_32.astype(BF16)
            gn_g, gn_b = ret_gn_g[j][None], ret_gn_b[j][None]
            qa, ka, va, qr, kr, vr, gate, fa, ka_t, va_t = _project(rows_p, yp, g_mix, mod_p, l, w_perm, att_segs,
                                                                    also_transposed=(1, 2))
            logf, fcum = _forget_gates(rows_p, fa, b_f_pad)
            hp8 = -(-h_a // 8) * 8
            fk_t = jnp.transpose(fcum.reshape(bp, sp, LANES)[:, :, :hp8], (0, 2, 1))
            o_fox = _fox_prompt(qa, ka, va, fcum, fk_t, bp, sp, dh_a, h_a)
            o_ret, st = _retention_prompt(qr, kr, vr, cos_p, sin_p, bp, sp, h_b, dk_b, dv_b)
            yp = _att_output(rows_p, o_fox, o_ret, gate, yp, mod_p, l, gn_g, gn_b, w1, w2, h_b, dv_b)
            k_p.append(ka_t)
            v_p.append(va_t)
            f_p.append(logf[:, :h_a].reshape(bp, sp, h_a))
            ret_p.append(st)
            qa, ka, va, qr, kr, vr, gate, fa = _project_f32(rows_s, ys, g_mix, mod_s, l, w_perm32, att_segs)
            logf, _ = _forget_gates(rows_s, fa, b_f_pad)
            o_fox = _fox_decode(page_table, qa[:, None, :], ka[:, None, :], va[:, None, :], logf[:, None, :],
                                kt_pages, vt_pages, lf_pages_t, j).reshape(bs, wa)
            qr2, kr2 = _rope_rows(qr, kr, cos_s, sin_s, dk_b)
            st, y_ret = _state_step(state_ret[j], gamma_col, kr2.reshape(bs, h_b, dk_b, 1),
                                    qr2.reshape(bs, h_b, dk_b, 1), vr.reshape(bs, h_b, 1, dv_b))
            ys = _att_output(rows_s, o_fox, y_ret.reshape(bs, wvr), gate, ys, mod_s, l, gn_g, gn_b, w1_32, w2_32,
                             h_b, dv_b)
            k_s.append(ka.reshape(bs, 1, h_a, dh_a))
            v_s.append(va.reshape(bs, 1, h_a, dh_a))
            f_s.append(logf[:, :h_a].reshape(bs, 1, h_a))
            ret_s.append(st)
        else:
            n_ssm = ssm_segs[-1][0] + LANES
            w_pad32 = jnp.pad(ssm_w_in[j], ((0, 0), (0, -(-n_ssm // F32_COL_BLOCK) * F32_COL_BLOCK - n_ssm + LANES - h_c)))
            w_pad = w_pad32[:, :n_ssm].astype(BF16)
            cw, cb = ssm_conv_w[j], ssm_conv_b[j][None]
            dtb = jnp.pad(ssm_dt_bias[j], (0, LANES - h_c))[None]
            alog = jnp.pad(ssm_a_log[j], (0, LANES - h_c))[None]
            dsk = jnp.repeat(ssm_d[j], hd_c)[None]
            ng = ssm_norm_g[j][None]
            w_out = ssm_w_out[j].astype(BF16)
            z, xr, dtr = _project(rows_p, yp, g_mix, mod_p, l, w_pad, ssm_segs)
            y_n, st2, cv = _ssd_prompt(xr, z, dtr, cw, cb, dtb, alog, dsk, ng, e_mat, bp, sp,
                                       d_inner, n_groups, d_state, hd_c)
            yp = _matmul_residual(rows_p, y_n, yp, mod_p, l, w_out)
            st = st2.reshape(bp, h_c // 2, d_state, 2, hd_c).transpose(0, 1, 3, 2, 4).reshape(bp, h_c, d_state, hd_c)
            ssm_p.append(st)
            conv_p.append(cv[:, 8 - (conv_w_len - 1):, :])
            z, xr, dtr = _project_f32(rows_s, ys, g_mix, mod_s, l, w_pad32, ssm_segs)
            taps = [state_conv[j][:, i, :] for i in range(conv_w_len - 1)]
            x, bm, cm, v, a = _ssd_step_prep(xr, taps, dtr, cw, cb, dtb, alog, e_mat, d_inner, gn)
            st_t, y_s = _state_step_t(jnp.swapaxes(state_ssm[j], -1, -2), a[:, None, :],
                                      bm.reshape(bs, n_groups, d_state), cm.reshape(bs, n_groups, d_state),
                                      v.reshape(bs, h_c, hd_c))
            y_n = _ssd_step_post(y_s.reshape(bs, d_inner), x, z, dsk, ng, n_groups)
            ys = _matmul_residual(rows_s, y_n, ys, mod_s, l, ssm_w_out[j])
            ssm_s.append(jnp.swapaxes(st_t, -1, -2))
            conv_s.append(jnp.concatenate([state_conv[j][:, 1:, :], xr[:, None, :]], axis=1))
        yp, ys = _moe_layer([(rows_m, mod_p), (rows_s, mod_s)], [yp, ys], l, norm_ffn_g[l][None],
                            router_w[l], router_b[l], wg_all, wu_all, exp_b_up[l], exp_w_down, exp_b_down[l], tmf,
                            norm_final_g[None], l == depth - 1)
    y_prompt = yp.reshape(bp, sp, d)
    y_sample = ys.reshape(bs, 1, d)

    def token_major(parts):
        t = jnp.stack(parts).reshape(len(parts), bp, h_a, dh_a, sp)
        return jnp.transpose(t, (0, 1, 4, 2, 3))

    return (y_prompt, y_sample,
            token_major(k_p), token_major(v_p), jnp.stack(f_p),
            jnp.stack(k_s), jnp.stack(v_s), jnp.stack(f_s),
            jnp.stack(ret_p), jnp.stack(ret_s),
            jnp.stack(ssm_p), jnp.stack(ssm_s),
            jnp.stack(conv_p), jnp.stack(conv_s))
```

```python
import functools
import math

import jax
import jax.numpy as jnp
import numpy as np
from jax import lax
from jax.experimental import pallas as pl
from jax.experimental.pallas import tpu as pltpu

F32 = jnp.float32
BF16 = jnp.bfloat16
I32 = jnp.int32

LANES = 128
VMEM_LIMIT = 56 * 1024 * 1024
CHUNK = 128
F32_COL_BLOCK = 512
TOP_K = 4
RET_DECAY_BASE = 5.0
ROPE_THETA = 10000.0
SWIGLU_LIMIT = 7.0
SWIGLU_ALPHA = 1.702
NEG_INF = float("-inf")


def _params(*sem):
    return pltpu.CompilerParams(dimension_semantics=sem, vmem_limit_bytes=VMEM_LIMIT)


def _dot(a, b):
    return jnp.dot(a, b, preferred_element_type=F32)


def _dot_nt(a, b):
    return lax.dot_general(a, b, (((1,), (1,)), ((), ())), preferred_element_type=F32)


def _dot_tn(a, b):
    return lax.dot_general(a, b, (((0,), (0,)), ((), ())), preferred_element_type=F32)


def _split3(x):
    hi = x.astype(BF16)
    r = x - hi.astype(F32)
    mid = r.astype(BF16)
    lo = (r - mid.astype(F32)).astype(BF16)
    return hi, mid, lo


def _dot_sel_rhs(x, m):
    hi, mid, lo = _split3(x)
    return _dot(hi, m) + _dot(mid, m) + _dot(lo, m)


def _dot_sel_lhs(m, x):
    hi, mid, lo = _split3(x)
    return _dot(m, hi) + _dot(m, mid) + _dot(m, lo)


def _dot3(a, b, dot=_dot):
    a_hi = a.astype(BF16)
    a_lo = (a - a_hi.astype(F32)).astype(BF16)
    b_hi = b.astype(BF16)
    b_lo = (b - b_hi.astype(F32)).astype(BF16)
    return dot(a_hi, b_hi) + dot(a_hi, b_lo) + dot(a_lo, b_hi)


def _dot_w(a, w_ref):
    w = w_ref[...]
    return _dot3(a, w) if w.dtype == F32 else _dot(a.astype(BF16), w)


def _sigmoid(x):
    return 1.0 / (1.0 + jnp.exp(-x))


def _silu(x):
    return x * _sigmoid(x)


def _softplus(x):
    return jnp.maximum(x, 0.0) + jnp.log(1.0 + jnp.exp(-jnp.abs(x)))


def _log_sigmoid(x):
    return -_softplus(-x)


def _modulate(x, g, shift, scale):
    ms = jnp.mean(x * x, axis=-1, keepdims=True)
    return (x * lax.rsqrt(ms + 1e-6)) * g * (1.0 + scale) + shift


def _iota2(shape, dim):
    return lax.broadcasted_iota(I32, shape, dim)


class _Rows:
    def __init__(self, n_batch, rows_per_batch, tile):
        self.n_batch = n_batch
        self.rows_per_batch = rows_per_batch
        self.rows = n_batch * rows_per_batch
        self.per_row_mod = rows_per_batch == 1
        self.tile = self.rows if self.per_row_mod else min(tile, rows_per_batch)
        assert self.rows % self.tile == 0 and (self.per_row_mod or rows_per_batch % self.tile == 0)
        self.n_tiles = self.rows // self.tile
        self.tiles_per_batch = 1 if self.per_row_mod else rows_per_batch // self.tile

    def spec(self, width, col=0):
        return pl.BlockSpec((self.tile, width), lambda i, *_: (i, col))

    def mod(self, mod_arr, layer, chunk, d):
        if self.per_row_mod:
            return mod_arr, pl.BlockSpec((None, self.rows, d), lambda i, *_: (layer, 0, chunk))
        tpb = self.tiles_per_batch
        arr = mod_arr.reshape(mod_arr.shape[0], mod_arr.shape[1], 1, mod_arr.shape[2])
        return arr, pl.BlockSpec((None, None, 1, d), lambda i, *_: (layer, i // tpb, 0, chunk))


def _full(arr):
    nd = arr.ndim
    return pl.BlockSpec(arr.shape, lambda *_: (0,) * nd)


def _mod_kernel(c_ref, w_ref, b_ref, o_ref):
    c = c_ref[...]
    o_ref[...] = _dot3(_silu(c), w_ref[...]) + b_ref[...]


def _modulation_all(c_all, ada_w, ada_b):
    n_layers, d, n6 = ada_w.shape
    rows = c_all.shape[0]
    tn = n6 // 8 if n6 % (8 * LANES) == 0 else n6
    return pl.pallas_call(
        _mod_kernel,
        out_shape=jax.ShapeDtypeStruct((n_layers, rows, n6), F32),
        grid=(n_layers, n6 // tn),
        in_specs=[pl.BlockSpec((rows, d), lambda l, j: (0, 0)),
                  pl.BlockSpec((None, d, tn), lambda l, j: (l, 0, j)),
                  pl.BlockSpec((None, 1, tn), lambda l, j: (l, 0, j))],
        out_specs=pl.BlockSpec((None, rows, tn), lambda l, j: (l, 0, j)),
        compiler_params=_params("arbitrary", "arbitrary"),
        name="adaln_modulation",
    )(c_all, ada_w, ada_b.reshape(n_layers, 1, n6))


def _proj_kernel(x_ref, g_ref, sh_ref, sc_ref, w_ref, *o_refs, segs, also_transposed):
    h = _modulate(x_ref[...], g_ref[...], sh_ref[...], sc_ref[...]).astype(BF16)
    t_refs = list(o_refs[len(segs):])
    for i, (o_ref, (start, width)) in enumerate(zip(o_refs, segs)):
        val = _dot(h, w_ref[:, start:start + width])
        o_ref[...] = val
        if i in also_transposed:
            t_refs.pop(0)[...] = val.T


def _project(rows, x, g, mod, layer, w_bf16, segs, also_transposed=()):
    d = x.shape[1]
    sh_arr, sh_spec = rows.mod(mod, layer, 0, d)
    sc_arr, sc_spec = rows.mod(mod, layer, 1, d)
    tpb = rows.tiles_per_batch
    t_shapes = [jax.ShapeDtypeStruct((rows.n_batch, segs[i][1], rows.rows_per_batch), F32) for i in also_transposed]
    t_specs = [pl.BlockSpec((None, segs[i][1], rows.tile), lambda t: (t // tpb, 0, t % tpb)) for i in also_transposed]
    return pl.pallas_call(
        functools.partial(_proj_kernel, segs=segs, also_transposed=tuple(also_transposed)),
        out_shape=[jax.ShapeDtypeStruct((rows.rows, wd), F32) for _, wd in segs] + t_shapes,
        grid=(rows.n_tiles,),
        in_specs=[rows.spec(d), _full(g), sh_spec, sc_spec, _full(w_bf16)],
        out_specs=[rows.spec(wd) for _, wd in segs] + t_specs,
        compiler_params=_params("arbitrary"),
        name="norm_mod_project",
    )(x, g, sh_arr, sc_arr, w_bf16)


def _proj_f32_kernel(x_ref, g_ref, sh_ref, sc_ref, w_ref, o_ref):
    o_ref[...] = _dot3(_modulate(x_ref[...], g_ref[...], sh_ref[...], sc_ref[...]), w_ref[...])


def _project_f32(rows, x, g, mod, layer, w_f32, segs):
    d, n_pad = w_f32.shape
    cb = F32_COL_BLOCK
    assert n_pad % cb == 0
    w_pad = w_f32
    sh_arr, sh_spec = rows.mod(mod, layer, 0, d)
    sc_arr, sc_spec = rows.mod(mod, layer, 1, d)
    assert rows.n_tiles == 1
    out = pl.pallas_call(
        _proj_f32_kernel,
        out_shape=jax.ShapeDtypeStruct((rows.rows, n_pad), F32),
        grid=(n_pad // cb,),
        in_specs=[pl.BlockSpec((rows.rows, d), lambda j: (0, 0)), _full(g),
                  pl.BlockSpec(sh_spec.block_shape, lambda j: sh_spec.index_map(0)),
                  pl.BlockSpec(sc_spec.block_shape, lambda j: sc_spec.index_map(0)),
                  pl.BlockSpec((d, cb), lambda j: (0, j))],
        out_specs=pl.BlockSpec((rows.rows, cb), lambda j: (0, j)),
        compiler_params=_params("arbitrary"),
        name="norm_mod_project_f32",
    )(x, g, sh_arr, sc_arr, w_pad)
    return [out[:, s:s + wd] for s, wd in segs]


def _logf_kernel(fa_ref, bf_ref, lf_ref, fc_ref, carry_ref, *, tiles_per_batch):
    i = pl.program_id(0)
    lf = _log_sigmoid(fa_ref[...] + bf_ref[...])
    lf_ref[...] = lf

    @pl.when(i % tiles_per_batch == 0)
    def _():
        carry_ref[...] = jnp.zeros_like(carry_ref)

    tm = lf.shape[0]
    tri = (_iota2((tm, tm), 1) <= _iota2((tm, tm), 0)).astype(BF16)
    cs = _dot_sel_lhs(tri, lf) + carry_ref[...]
    fc_ref[...] = cs
    carry_ref[...] = cs[tm - 1:tm, :]


def _forget_gates(rows, fa_raw, b_f_pad):
    return pl.pallas_call(
        functools.partial(_logf_kernel, tiles_per_batch=rows.tiles_per_batch),
        out_shape=[jax.ShapeDtypeStruct((rows.rows, LANES), F32)] * 2,
        grid=(rows.n_tiles,),
        in_specs=[rows.spec(LANES), _full(b_f_pad)],
        out_specs=[rows.spec(LANES)] * 2,
        scratch_shapes=[pltpu.VMEM((1, LANES), F32)],
        compiler_params=_params("arbitrary"),
        name="forget_gates",
    )(fa_raw, b_f_pad)


def _fox_kernel(q_ref, k_ref, v_ref, fc_ref, ft_ref, o_ref, fkb_ref, qm_ref, acc_ref, *, tq, dh, n_blocks, n_heads):
    qi = pl.program_id(1)
    seq = k_ref.shape[0]
    hpb = LANES // dh

    @pl.when(qi == 0)
    def _():
        for h in range(n_heads):
            fkb_ref[h] = jnp.broadcast_to(fc_ref[:, h:h + 1], (seq, LANES))

    q0 = pl.multiple_of(qi * tq, tq)
    lane = _iota2((1, LANES), 1)
    row_head = _iota2((LANES, 1), 0) // dh
    on_or_before = _iota2((tq, tq), 0) <= _iota2((tq, tq), 1)
    for p in range(n_blocks):
        q2 = q_ref[:, p * LANES:(p + 1) * LANES] * (dh ** -0.5)
        for j in range(hpb):
            qm_ref[p * hpb + j] = jnp.where(lane // dh == j, q2, 0.0).astype(BF16)
    acc_ref[...] = jnp.zeros_like(acc_ref)

    def block(k0, carry, diagonal):
        ms, ls = carry
        ss = []
        for p in range(n_blocks):
            k2 = k_ref[pl.ds(k0, tq), p * LANES:(p + 1) * LANES].astype(BF16)
            ss += [_dot_nt(k2, qm_ref[p * hpb + j]) for j in range(hpb)]
        new_ms, new_ls, alphas, pes = [], [], [], []
        for h in range(n_heads):
            fk = fkb_ref[h, pl.ds(k0, tq), :]
            s = ss[h] + (ft_ref[h:h + 1, pl.ds(q0, tq)] - jnp.concatenate([fk] * (tq // LANES), axis=1))
            if diagonal:
                s = jnp.where(on_or_before, s, NEG_INF)
            m_new = jnp.maximum(ms[h], jnp.max(s, axis=0, keepdims=True))
            alphas.append(jnp.exp(ms[h] - m_new))
            pe = jnp.exp(s - m_new)
            new_ms.append(m_new)
            new_ls.append(alphas[h] * ls[h] + jnp.sum(pe, axis=0, keepdims=True))
            pes.append(pe.astype(BF16))
        for p in range(n_blocks):
            v2 = v_ref[pl.ds(k0, tq), p * LANES:(p + 1) * LANES].astype(BF16)
            alpha2 = None
            pv2 = None
            for j in range(hpb):
                h = p * hpb + j
                pv = _dot_tn(v2, pes[h])
                sel = row_head == j
                alpha2 = jnp.where(sel, alphas[h], 0.0) if alpha2 is None else jnp.where(sel, alphas[h], alpha2)
                pv2 = jnp.where(sel, pv, 0.0) if pv2 is None else jnp.where(sel, pv, pv2)
            acc_ref[p] = alpha2 * acc_ref[p] + pv2
        return tuple(new_ms), tuple(new_ls)

    init = (tuple(jnp.full((1, tq), NEG_INF, F32) for _ in range(n_heads)),
            tuple(jnp.zeros((1, tq), F32) for _ in range(n_heads)))
    carry = lax.fori_loop(0, qi, lambda kv, c: block(pl.multiple_of(kv * tq, tq), c, False), init)
    ms, ls = block(q0, carry, True)
    for p in range(n_blocks):
        l2 = None
        for j in range(hpb):
            sel = row_head == j
            l2 = jnp.where(sel, ls[p * hpb + j], 1.0) if l2 is None else jnp.where(sel, ls[p * hpb + j], l2)
        o_ref[:, p * LANES:(p + 1) * LANES] = (acc_ref[p] / l2).T


def _fox_prompt(q, k, v, fcum, fcum_t, n_batch, seq, dh, n_heads):
    width = q.shape[1]
    assert width % LANES == 0 and LANES % dh == 0
    tq = min(256, seq)
    nq = seq // tq
    hp = fcum_t.shape[1]
    return pl.pallas_call(
        functools.partial(_fox_kernel, tq=tq, dh=dh, n_blocks=width // LANES, n_heads=n_heads),
        out_shape=jax.ShapeDtypeStruct(q.shape, F32),
        grid=(n_batch, nq),
        in_specs=[pl.BlockSpec((tq, width), lambda b, i: (b * nq + i, 0)),
                  pl.BlockSpec((seq, width), lambda b, i: (b, 0)),
                  pl.BlockSpec((seq, width), lambda b, i: (b, 0)),
                  pl.BlockSpec((seq, LANES), lambda b, i: (b, 0)),
                  pl.BlockSpec((None, hp, seq), lambda b, i: (b, 0, 0))],
        out_specs=pl.BlockSpec((tq, width), lambda b, i: (b * nq + i, 0)),
        scratch_shapes=[pltpu.VMEM((n_heads, seq, LANES), F32), pltpu.VMEM((n_heads, tq, LANES), BF16),
                        pltpu.VMEM((width // LANES, LANES, tq), F32)],
        compiler_params=_params("arbitrary", "arbitrary"),
        name="fox_prompt_attention",
    )(q, k, v, fcum, fcum_t)


def _rope_tables(pos, n_heads, dk):
    half = dk // 2
    freq = ROPE_THETA ** (-jnp.arange(half, dtype=F32) / half)
    ang = pos.astype(F32)[:, None] * freq[None, :]
    cos = jnp.cos(ang)
    sin = jnp.sin(ang)
    cos_h = jnp.concatenate([cos, cos], axis=-1)
    sin_h = jnp.concatenate([-sin, sin], axis=-1)
    return jnp.tile(cos_h, (1, n_heads)), jnp.tile(sin_h, (1, n_heads))


def _rope(x, cos, sin, dk):
    half = dk // 2
    lane = _iota2((1, LANES), 1)
    up = pltpu.roll(x, LANES - half, 1)
    down = pltpu.roll(x, half, 1)
    partner = jnp.where((lane % dk) < half, up, down)
    return x * cos + partner * sin


def _ret_kernel(q_ref, k_ref, v_ref, cos_ref, sin_ref, o_ref, st_ref, state, *, n_heads, dk, dv, log_gammas):
    c = pl.program_id(1)
    hpb = LANES // dk

    @pl.when(c == 0)
    def _():
        state[...] = jnp.zeros_like(state)

    lane = _iota2((1, LANES), 1)
    t_col = _iota2((CHUNK, 1), 0).astype(F32)
    seg = (_iota2((CHUNK, CHUNK), 0) - _iota2((CHUNK, CHUNK), 1)).astype(F32)
    causal = seg >= 0.0
    row_head = _iota2((LANES, 1), 0) // dk
    scores, carried, updates, row_decays = [], [], [], []
    for p in range(n_heads // hpb):
        cols = slice(p * LANES, (p + 1) * LANES)
        cos = cos_ref[:, cols]
        sin = sin_ref[:, cols]
        q2 = _rope(q_ref[:, cols], cos, sin, dk)
        k2 = _rope(k_ref[:, cols], cos, sin, dk) * (dk ** -0.5)
        k2b = k2.astype(BF16)
        st2b = state[p].astype(BF16)
        upd = None
        row_decay = None
        for j in range(hpb):
            h = p * hpb + j
            lg = log_gammas[h]
            sel = lane // dk == j
            qh = jnp.where(sel, q2, 0.0).astype(BF16)
            scores.append(_dot_nt(qh, k2b))
            carried.append(_dot(qh, st2b))
            tail = jnp.exp((CHUNK - 1.0 - t_col) * lg)
            kh = jnp.where(sel, k2, 0.0).astype(BF16)
            u = _dot_tn(kh, (v_ref[:, h * dv:(h + 1) * dv] * tail).astype(BF16))
            upd = u if upd is None else upd + u
            rd = jnp.where(row_head == j, math.exp(CHUNK * lg), 0.0)
            row_decay = rd if row_decay is None else row_decay + rd
        updates.append(upd)
        row_decays.append(row_decay)
    for h in range(n_heads):
        lg = log_gammas[h]
        decay = jnp.exp(jnp.where(causal, seg * lg, NEG_INF))
        y = _dot((scores[h] * decay).astype(BF16), v_ref[:, h * dv:(h + 1) * dv].astype(BF16))
        o_ref[:, h * dv:(h + 1) * dv] = y + jnp.exp((t_col + 1.0) * lg) * carried[h]
    for p in range(n_heads // hpb):
        state[p] = row_decays[p] * state[p] + updates[p]

    @pl.when(c == pl.num_programs(1) - 1)
    def _():
        for h in range(n_heads):
            p, j = divmod(h, hpb)
            st_ref[h] = state[p, j * dk:(j + 1) * dk, :]


def _retention_prompt(qr, kr, vr, cos, sin, n_batch, seq, n_heads, dk, dv):
    nc = seq // CHUNK
    log_gammas = tuple(math.log1p(-2.0 ** (-RET_DECAY_BASE - h)) for h in range(n_heads))
    wq = n_heads * dk
    wv = n_heads * dv
    assert wq % LANES == 0 and LANES % dk == 0 and dv % LANES == 0
    return pl.pallas_call(
        functools.partial(_ret_kernel, n_heads=n_heads, dk=dk, dv=dv, log_gammas=log_gammas),
        out_shape=[jax.ShapeDtypeStruct((n_batch * seq, wv), F32),
                   jax.ShapeDtypeStruct((n_batch, n_heads, dk, dv), F32)],
        grid=(n_batch, nc),
        in_specs=[pl.BlockSpec((CHUNK, wq), lambda b, c: (b * nc + c, 0)),
                  pl.BlockSpec((CHUNK, wq), lambda b, c: (b * nc + c, 0)),
                  pl.BlockSpec((CHUNK, wv), lambda b, c: (b * nc + c, 0)),
                  pl.BlockSpec((CHUNK, wq), lambda b, c: (c, 0)),
                  pl.BlockSpec((CHUNK, wq), lambda b, c: (c, 0))],
        out_specs=[pl.BlockSpec((CHUNK, wv), lambda b, c: (b * nc + c, 0)),
                   pl.BlockSpec((None, n_heads, dk, dv), lambda b, c: (b, 0, 0, 0))],
        scratch_shapes=[pltpu.VMEM((wq // LANES, LANES, dv), F32)],
        compiler_params=_params("arbitrary", "arbitrary"),
        name="retention_prompt_scan",
    )(qr, kr, vr, cos, sin)


def _att_out_kernel(of_ref, or_ref, gt_ref, y_ref, gm_ref, gg_ref, gb_ref, w1_ref, w2_ref, o_ref, *, n_heads, dv):
    parts = []
    for h in range(n_heads):
        r = or_ref[:, h * dv:(h + 1) * dv]
        mu = jnp.mean(r, axis=-1, keepdims=True)
        d = r - mu
        var = jnp.mean(d * d, axis=-1, keepdims=True)
        parts.append(d * lax.rsqrt(var + 1e-5))
    r = jnp.concatenate(parts, axis=1) * gg_ref[...] + gb_ref[...]
    r = r * _silu(gt_ref[...])
    o = _dot_w(of_ref[...], w1_ref) + _dot_w(r, w2_ref)
    o_ref[...] = y_ref[...] + gm_ref[...] * o


def _att_output(rows, o_fox, o_ret, gate, y, mod, layer, gn_g, gn_b, w1, w2, n_heads, dv):
    d = y.shape[1]
    gm_arr, gm_spec = rows.mod(mod, layer, 2, d)
    return pl.pallas_call(
        functools.partial(_att_out_kernel, n_heads=n_heads, dv=dv),
        out_shape=jax.ShapeDtypeStruct(y.shape, F32),
        grid=(rows.n_tiles,),
        in_specs=[rows.spec(o_fox.shape[1]), rows.spec(o_ret.shape[1]), rows.spec(gate.shape[1]), rows.spec(d),
                  gm_spec, _full(gn_g), _full(gn_b), _full(w1), _full(w2)],
        out_specs=rows.spec(d),
        compiler_params=_params("arbitrary"),
        name="attention_output",
    )(o_fox, o_ret, gate, y, gm_arr, gn_g, gn_b, w1, w2)


def _mm_res_kernel(a_ref, y_ref, gm_ref, w_ref, o_ref):
    o_ref[...] = y_ref[...] + gm_ref[...] * _dot_w(a_ref[...], w_ref)


def _matmul_residual(rows, a, y, mod, layer, w):
    d = y.shape[1]
    gm_arr, gm_spec = rows.mod(mod, layer, 2, d)
    return pl.pallas_call(
        _mm_res_kernel,
        out_shape=jax.ShapeDtypeStruct(y.shape, F32),
        grid=(rows.n_tiles,),
        in_specs=[rows.spec(a.shape[1]), rows.spec(d), gm_spec, _full(w)],
        out_specs=rows.spec(d),
        compiler_params=_params("arbitrary"),
        name="matmul_gated_residual",
    )(a, y, gm_arr, w)


def _ssd_conv_act(taps, cw_ref, cb_ref):
    conv = cb_ref[...] + cw_ref[0:1, :] * taps[0]
    for i in range(1, len(taps)):
        conv = conv + cw_ref[i:i + 1, :] * taps[i]
    return _silu(conv)


def _ssd_gate_norm(y, x, z, dsk, ng, n_groups):
    y = (y + dsk * x) * _silu(z)
    gw = y.shape[1] // n_groups
    parts = []
    for g in range(n_groups):
        blk = y[:, g * gw:(g + 1) * gw]
        ms = jnp.mean(blk * blk, axis=-1, keepdims=True)
        parts.append(blk * lax.rsqrt(ms + 1e-5))
    return jnp.concatenate(parts, axis=1) * ng


def _ssd_kernel(xr_ref, z_ref, dt_ref, cw_ref, cb_ref, dtb_ref, alog_ref, dsk_ref, ng_ref, e_ref,
                y_ref, st_ref, cv_ref, prev, state, *, d_inner, n_groups, d_state, hd):
    c = pl.program_id(1)
    hpg = d_inner // hd // n_groups
    ppg = hpg * hd // LANES

    @pl.when(c == 0)
    def _():
        prev[...] = jnp.zeros_like(prev)
        state[...] = jnp.zeros_like(state)

    cur = xr_ref[...]
    xw = jnp.concatenate([prev[...], cur], axis=0)
    taps = [xw[5 + i:5 + i + CHUNK] for i in range(3)] + [cur]
    tail_rows = cur[CHUNK - 8:CHUNK]
    prev[...] = tail_rows
    cv_ref[...] = tail_rows
    xbc = _ssd_conv_act(taps, cw_ref, cb_ref)
    gn = n_groups * d_state
    x = xbc[:, :d_inner]
    bm = xbc[:, d_inner:d_inner + gn].astype(BF16)
    cm = xbc[:, d_inner + gn:].astype(BF16)

    dt = _softplus(dt_ref[...] + dtb_ref[...])
    la = -dt * jnp.exp(alog_ref[...])
    seg = _iota2((CHUNK, CHUNK), 0) - _iota2((CHUNK, CHUNK), 1)
    causal = seg >= 0
    cum = _dot_sel_lhs(causal.astype(BF16), la)
    cum_t = cum.T
    e = e_ref[...]
    dt_e = _dot_sel_rhs(dt, e)
    cum_e = _dot_sel_rhs(cum, e)
    ecum_e = jnp.exp(cum_e)
    tail_e = jnp.exp(cum_e[CHUNK - 1:CHUNK, :] - cum_e)
    v = x * dt_e
    vb = v.astype(BF16)
    vtb = (v * tail_e).astype(BF16)
    lane = _iota2((1, LANES), 1)
    hpb = LANES // hd
    ys = []
    for g in range(n_groups):
        cmg = cm[:, g * d_state:(g + 1) * d_state]
        bmg = bm[:, g * d_state:(g + 1) * d_state]
        s = _dot_nt(cmg, bmg)
        for pp in range(ppg):
            p = g * ppg + pp
            cols = slice(p * LANES, (p + 1) * LANES)
            v2 = vb[:, cols]
            yp = None
            for j in range(hpb):
                h = p * hpb + j
                d = jnp.exp(jnp.where(causal, cum[:, h:h + 1] - cum_t[h:h + 1, :], NEG_INF))
                yj = _dot((s * d).astype(BF16), v2)
                sel = lane // hd == j
                yp = jnp.where(sel, yj, 0.0) if yp is None else jnp.where(sel, yj, yp)
            st2 = state[p]
            yp = yp + ecum_e[:, cols] * _dot(cmg, st2.astype(BF16))
            ys.append(yp)
            state[p] = ecum_e[CHUNK - 1:CHUNK, cols] * st2 + _dot_tn(bmg, vtb[:, cols])
    y = jnp.concatenate(ys, axis=1)
    y_ref[...] = _ssd_gate_norm(y, x, z_ref[...], dsk_ref[...], ng_ref[...], n_groups)

    @pl.when(c == pl.num_programs(1) - 1)
    def _():
        st_ref[...] = state[...]


def _ssd_prompt(xr, z, dtr, cw, cb, dtb, alog, dsk, ng, e_mat, n_batch, seq, d_inner, n_groups, d_state, hd):
    nc = seq // CHUNK
    cd = xr.shape[1]
    n_blk = d_inner // LANES
    row = lambda b, c: (b * nc + c, 0)
    return pl.pallas_call(
        functools.partial(_ssd_kernel, d_inner=d_inner, n_groups=n_groups, d_state=d_state, hd=hd),
        out_shape=[jax.ShapeDtypeStruct((n_batch * seq, d_inner), F32),
                   jax.ShapeDtypeStruct((n_batch, n_blk, d_state, LANES), F32),
                   jax.ShapeDtypeStruct((n_batch, 8, cd), F32)],
        grid=(n_batch, nc),
        in_specs=[pl.BlockSpec((CHUNK, cd), row), pl.BlockSpec((CHUNK, d_inner), row),
                  pl.BlockSpec((CHUNK, LANES), row),
                  _full(cw), _full(cb), _full(dtb), _full(alog), _full(dsk), _full(ng), _full(e_mat)],
        out_specs=[pl.BlockSpec((CHUNK, d_inner), row),
                   pl.BlockSpec((None, n_blk, d_state, LANES), lambda b, c: (b, 0, 0, 0)),
                   pl.BlockSpec((None, 8, cd), lambda b, c: (b, 0, 0))],
        scratch_shapes=[pltpu.VMEM((8, cd), F32), pltpu.VMEM((n_blk, d_state, LANES), F32)],
        compiler_params=_params("arbitrary", "arbitrary"),
        name="ssd_prompt_scan",
    )(xr, z, dtr, cw, cb, dtb, alog, dsk, ng, e_mat)


def _ssd_step_prep_kernel(xr_ref, c0_ref, c1_ref, c2_ref, dt_ref, cw_ref, cb_ref, dtb_ref, alog_ref, e_ref,
                          x_ref, bm_ref, cm_ref, v_ref, a_ref, *, d_inner, gn):
    xbc = _ssd_conv_act([c0_ref[...], c1_ref[...], c2_ref[...], xr_ref[...]], cw_ref, cb_ref)
    x = xbc[:, :d_inner]
    x_ref[...] = x
    bm_ref[...] = xbc[:, d_inner:d_inner + gn]
    cm_ref[...] = xbc[:, d_inner + gn:]
    dt = _softplus(dt_ref[...] + dtb_ref[...])
    a_ref[...] = jnp.exp(-dt * jnp.exp(alog_ref[...]))
    v_ref[...] = x * _dot_sel_rhs(dt, e_ref[...])


def _ssd_step_prep(xr, taps, dtr, cw, cb, dtb, alog, e_mat, d_inner, gn):
    n = xr.shape[0]
    args = (xr, *taps, dtr, cw, cb, dtb, alog, e_mat)
    return pl.pallas_call(
        functools.partial(_ssd_step_prep_kernel, d_inner=d_inner, gn=gn),
        out_shape=[jax.ShapeDtypeStruct((n, d_inner), F32), jax.ShapeDtypeStruct((n, gn), F32),
                   jax.ShapeDtypeStruct((n, gn), F32), jax.ShapeDtypeStruct((n, d_inner), F32),
                   jax.ShapeDtypeStruct((n, LANES), F32)],
        grid=(1,),
        in_specs=[_full(a) for a in args],
        out_specs=[pl.BlockSpec((n, d_inner), lambda i: (0, 0)), pl.BlockSpec((n, gn), lambda i: (0, 0)),
                   pl.BlockSpec((n, gn), lambda i: (0, 0)), pl.BlockSpec((n, d_inner), lambda i: (0, 0)),
                   pl.BlockSpec((n, LANES), lambda i: (0, 0))],
        compiler_params=_params("arbitrary"),
        name="ssd_step_prep",
    )(*args)


def _ssd_step_post_kernel(y_ref, x_ref, z_ref, dsk_ref, ng_ref, o_ref, *, n_groups):
    o_ref[...] = _ssd_gate_norm(y_ref[...], x_ref[...], z_ref[...], dsk_ref[...], ng_ref[...], n_groups)


def _ssd_step_post(y, x, z, dsk, ng, n_groups):
    args = (y, x, z, dsk, ng)
    return pl.pallas_call(
        functools.partial(_ssd_step_post_kernel, n_groups=n_groups),
        out_shape=jax.ShapeDtypeStruct(y.shape, F32),
        grid=(1,),
        in_specs=[_full(a) for a in args],
        out_specs=pl.BlockSpec(y.shape, lambda i: (0, 0)),
        compiler_params=_params("arbitrary"),
        name="ssd_step_post",
    )(*args)


def _state_step_kernel(s_ref, a_ref, k_ref, q_ref, v_ref, so_ref, y_ref, *, n_heads, heads_per_key):
    for h in range(n_heads):
        g = h // heads_per_key
        new = a_ref[h] * s_ref[h] + k_ref[g] * v_ref[h]
        so_ref[h] = new
        y_ref[h] = jnp.sum(q_ref[g] * new, axis=0, keepdims=True)


def _state_step(state, a, k_col, q_col, v_row):
    n_b, n_h, n_n, n_v = state.shape
    n_g = k_col.shape[1]
    blk = lambda arr: pl.BlockSpec((None,) + arr.shape[1:], lambda b: (b, 0, 0, 0))
    return pl.pallas_call(
        functools.partial(_state_step_kernel, n_heads=n_h, heads_per_key=n_h // n_g),
        out_shape=[jax.ShapeDtypeStruct(state.shape, F32), jax.ShapeDtypeStruct((n_b, n_h, 1, n_v), F32)],
        grid=(n_b,),
        in_specs=[blk(state), blk(a), blk(k_col), blk(q_col), blk(v_row)],
        out_specs=[blk(state), pl.BlockSpec((None, n_h, 1, n_v), lambda b: (b, 0, 0, 0))],
        compiler_params=_params("arbitrary"),
        name="state_step",
    )(state, a, k_col, q_col, v_row)


def _state_step_t_kernel(s_ref, a_ref, k_ref, q_ref, v_ref, so_ref, y_ref, *, n_heads, heads_per_key):
    n_v = s_ref.shape[1]
    eye_v = (_iota2((n_v, n_v), 0) == _iota2((n_v, n_v), 1)).astype(BF16)
    v_t = sum(_dot_nt(eye_v, part) for part in _split3(v_ref[...]))
    a_row = a_ref[...]
    lane = _iota2((1, LANES), 1)
    y_t = jnp.zeros((n_v, LANES), F32)
    for h in range(n_heads):
        g = h // heads_per_key
        new = a_row[:, h:h + 1] * s_ref[h] + v_t[:, h:h + 1] * k_ref[g:g + 1, :]
        so_ref[h] = new
        y_t = jnp.where(lane == h, jnp.sum(new * q_ref[g:g + 1, :], axis=1, keepdims=True), y_t)
    eye_l = (_iota2((LANES, LANES), 0) == _iota2((LANES, LANES), 1)).astype(BF16)
    y = sum(_dot_nt(eye_l, part) for part in _split3(y_t))
    y_ref[...] = y[:n_heads]


def _state_step_t(state_t, a, k_row, q_row, v_row):
    n_b, n_h, n_v, n_n = state_t.shape
    n_g = k_row.shape[1]
    assert n_h <= LANES
    blk = lambda arr: pl.BlockSpec((None,) + arr.shape[1:], lambda b: (b,) + (0,) * (arr.ndim - 1))
    return pl.pallas_call(
        functools.partial(_state_step_t_kernel, n_heads=n_h, heads_per_key=n_h // n_g),
        out_shape=[jax.ShapeDtypeStruct(state_t.shape, F32), jax.ShapeDtypeStruct((n_b, n_h, n_v), F32)],
        grid=(n_b,),
        in_specs=[blk(state_t), blk(a), blk(k_row), blk(q_row), blk(v_row)],
        out_specs=[blk(state_t), pl.BlockSpec((None, n_h, n_v), lambda b: (b, 0, 0))],
        compiler_params=_params("arbitrary"),
        name="state_step_transposed",
    )(state_t, a, k_row, q_row, v_row)


def _rope_rows_kernel(q_ref, k_ref, cos_ref, sin_ref, qo_ref, ko_ref, *, dk):
    for p in range(q_ref.shape[1] // LANES):
        cols = slice(p * LANES, (p + 1) * LANES)
        qo_ref[:, cols] = _rope(q_ref[:, cols], cos_ref[:, cols], sin_ref[:, cols], dk)
        ko_ref[:, cols] = _rope(k_ref[:, cols], cos_ref[:, cols], sin_ref[:, cols], dk) * (dk ** -0.5)


def _rope_rows(q, k, cos, sin, dk):
    args = (q, k, cos, sin)
    return pl.pallas_call(
        functools.partial(_rope_rows_kernel, dk=dk),
        out_shape=[jax.ShapeDtypeStruct(q.shape, F32)] * 2,
        grid=(1,),
        in_specs=[_full(a) for a in args],
        out_specs=[pl.BlockSpec(q.shape, lambda i: (0, 0))] * 2,
        compiler_params=_params("arbitrary"),
        name="rope_rows",
    )(*args)


def _fox_decode_kernel(pt_ref, q_ref, kn_ref, vn_ref, lfn_ref, kt_hbm, vt_hbm, lf_hbm, o_ref,
                       kbuf, vbuf, lbuf, sem, qb_ref, acc_ref, *, layer, n_pages, group, n_slots, n_heads, dh, page):
    b = pl.program_id(0)
    nb = pl.num_programs(0)
    n_groups = n_pages // group
    hd = n_heads * dh

    def copies(bb, gi, slot):
        out = []
        for g in range(group):
            pid = pt_ref[bb, gi * group + g]
            out.append(pltpu.make_async_copy(kt_hbm.at[layer, pid], kbuf.at[slot, g], sem.at[slot, 0]))
            out.append(pltpu.make_async_copy(vt_hbm.at[layer, pid], vbuf.at[slot, g], sem.at[slot, 1]))
            out.append(pltpu.make_async_copy(lf_hbm.at[layer, pid], lbuf.at[slot, g], sem.at[slot, 2]))
        return out

    def start(bb, gi, slot):
        for c in copies(bb, gi, slot):
            c.start()

    def wait(bb, gi, slot):
        for c in copies(bb, gi, slot):
            c.wait()

    @pl.when(b == 0)
    def _():
        for g in range(n_slots - 1):
            start(b, g, g)

    q_row = q_ref[...] * (dh ** -0.5)
    qb_ref[...] = jnp.broadcast_to(q_row, (page, hd)).T.reshape(n_heads, dh, page)
    acc_ref[...] = jnp.zeros_like(acc_ref)
    tri = (_iota2((page, page), 0) <= _iota2((page, page), 1)).astype(BF16)

    def page_update(slot, g, carry):
        m, l, run = carry
        s = jnp.concatenate(
            [jnp.sum(kbuf[slot, g, h] * qb_ref[h], axis=0, keepdims=True) for h in range(n_heads)], axis=0)
        fcum = _dot_sel_rhs(lbuf[slot, g], tri) + run
        s = s - fcum
        m_new = jnp.maximum(m, jnp.max(s, axis=-1, keepdims=True))
        alpha = jnp.exp(m - m_new)
        pe = jnp.exp(s - m_new)
        l_new = alpha * l + jnp.sum(pe, axis=-1, keepdims=True)
        for h in range(n_heads):
            acc_ref[h] = alpha[h:h + 1, :] * acc_ref[h] + pe[h:h + 1, :] * vbuf[slot, g, h]
        return m_new, l_new, fcum[:, page - 1:page]

    def ring(gq, carry):
        for s in range(n_slots):
            gi = gq * n_slots + s
            wait(b, gi, s)
            ahead = gi + n_slots - 1
            refill = (s + n_slots - 1) % n_slots

            @pl.when(ahead < n_groups)
            def _():
                start(b, ahead, refill)

            @pl.when(jnp.logical_and(ahead >= n_groups, b + 1 < nb))
            def _():
                start(b + 1, ahead - n_groups, refill)

            for g in range(group):
                carry = page_update(s, g, carry)
        return carry

    init = (jnp.full((n_heads, 1), NEG_INF, F32), jnp.zeros((n_heads, 1), F32), jnp.zeros((n_heads, 1), F32))
    m, l, f_tot = lax.fori_loop(0, n_groups // n_slots, ring, init)

    own = (_iota2((n_heads, hd), 1) // dh) == _iota2((n_heads, hd), 0)
    spread = lambda col: jnp.sum(jnp.where(own, col, 0.0), axis=0, keepdims=True)
    lfn = jnp.sum(jnp.where(_iota2((n_heads, LANES), 1) == _iota2((n_heads, LANES), 0), lfn_ref[...], 0.0),
                  axis=1, keepdims=True)
    s_new = jnp.sum(jnp.where(own, q_row * kn_ref[...], 0.0), axis=1, keepdims=True) - (f_tot + lfn)
    m_fin = jnp.maximum(m, s_new)
    a2 = jnp.exp(m - m_fin)
    p_new = jnp.exp(s_new - m_fin)
    l_fin = a2 * l + p_new
    o_past = jnp.sum(acc_ref[...].reshape(hd, page).T, axis=0, keepdims=True)
    o_ref[...] = (spread(a2) * o_past + spread(p_new) * vn_ref[...]) / spread(l_fin)


def _fox_decode(page_table, q, k_new, v_new, lf_new, kt_pages, vt_pages, lf_pages_t, layer):
    n_b, _, hd = q.shape
    n_pages = page_table.shape[1]
    _, _, n_h, dh, page = kt_pages.shape
    group = max(1, min(4, n_pages // 2))
    n_slots = 4 if n_pages % (4 * group) == 0 else 2
    assert n_pages % (n_slots * group) == 0
    tok = lambda arr: pl.BlockSpec((None, 1, arr.shape[2]), lambda b, pt: (b, 0, 0))
    grid_spec = pltpu.PrefetchScalarGridSpec(
        num_scalar_prefetch=1,
        grid=(n_b,),
        in_specs=[tok(q), tok(k_new), tok(v_new), tok(lf_new),
                  pl.BlockSpec(memory_space=pl.ANY), pl.BlockSpec(memory_space=pl.ANY),
                  pl.BlockSpec(memory_space=pl.ANY)],
        out_specs=pl.BlockSpec((None, 1, hd), lambda b, pt: (b, 0, 0)),
        scratch_shapes=[pltpu.VMEM((n_slots, group, n_h, dh, page), F32),
                        pltpu.VMEM((n_slots, group, n_h, dh, page), F32),
                        pltpu.VMEM((n_slots, group, n_h, page), F32), pltpu.SemaphoreType.DMA((n_slots, 3)),
                        pltpu.VMEM((n_h, dh, page), F32), pltpu.VMEM((n_h, dh, page), F32)],
    )
    return pl.pallas_call(
        functools.partial(_fox_decode_kernel, layer=layer, n_pages=n_pages, group=group, n_slots=n_slots,
                          n_heads=n_h, dh=dh, page=page),
        out_shape=jax.ShapeDtypeStruct((n_b, 1, hd), F32),
        grid_spec=grid_spec,
        compiler_params=_params("arbitrary"),
        name="fox_paged_decode",
    )(page_table, q, k_new, v_new, lf_new, kt_pages, vt_pages, lf_pages_t)


def _store_row_tiles(ref, val):
    n, d = val.shape
    nb = d // LANES
    for c in range(nb):
        ref[pl.ds(c, n, stride=nb), :] = val[:, c * LANES:(c + 1) * LANES]


def _load_row_tiles(ref, n, nb):
    return jnp.concatenate([ref[pl.ds(c, n, stride=nb), :] for c in range(nb)], axis=1)


def _router_kernel(y_ref, g_ref, sh_ref, sc_ref, rwt_ref, rb_ref, cin_ref,
                   h_ref, idx_ref, w_ref, rk_ref, cnt_ref, carry, *, n_exp):
    i = pl.program_id(0)

    @pl.when(i == 0)
    def _():
        carry[...] = cin_ref[...]

    h = _modulate(y_ref[...], g_ref[...], sh_ref[...], sc_ref[...])
    _store_row_tiles(h_ref, h)
    tm = h.shape[0]
    logits = _dot3(rwt_ref[...], h, dot=_dot_nt) + rb_ref[...]
    eio = _iota2((n_exp, tm), 0)
    vals, idxs = [], []
    rest = logits
    for _ in range(TOP_K):
        m = jnp.max(rest, axis=0, keepdims=True)
        ik = jnp.min(jnp.where(rest == m, eio, n_exp), axis=0, keepdims=True)
        vals.append(m)
        idxs.append(ik)
        rest = jnp.where(eio == ik, NEG_INF, rest)
    ex = [jnp.exp(v - vals[0]) for v in vals]
    den = ex[0]
    for e in ex[1:]:
        den = den + e
    sel = jnp.zeros((n_exp, tm), F32)
    for ik in idxs:
        sel = sel + jnp.where(eio == ik, 1.0, 0.0)
    before = (_iota2((tm, tm), 0) < _iota2((tm, tm), 1)).astype(BF16)
    rank_all = _dot(sel.astype(BF16), before) + carry[:, 0:1]
    ranks = [jnp.sum(jnp.where(eio == ik, rank_all, 0.0), axis=0, keepdims=True) for ik in idxs]
    carry[...] = carry[...] + jnp.sum(sel, axis=1, keepdims=True)
    idx_ref[...] = jnp.concatenate(idxs, axis=0)
    w_ref[...] = jnp.concatenate([e / den for e in ex], axis=0)
    rk_ref[...] = jnp.concatenate(ranks, axis=0).astype(I32)
    cnt_ref[...] = carry[...]


def _route(rows, y, g, mod, layer, rw_t, rb_col, cnt_in):
    d = y.shape[1]
    n_exp = rw_t.shape[0]
    sh_arr, sh_spec = rows.mod(mod, layer, 3, d)
    sc_arr, sc_spec = rows.mod(mod, layer, 4, d)
    kt = pl.BlockSpec((TOP_K, rows.tile), lambda i: (0, i))
    nb = d // LANES
    return pl.pallas_call(
        functools.partial(_router_kernel, n_exp=n_exp),
        out_shape=[jax.ShapeDtypeStruct((rows.rows * nb, LANES), F32),
                   jax.ShapeDtypeStruct((TOP_K, rows.rows), I32),
                   jax.ShapeDtypeStruct((TOP_K, rows.rows), F32),
                   jax.ShapeDtypeStruct((TOP_K, rows.rows), I32),
                   jax.ShapeDtypeStruct((n_exp, LANES), F32)],
        grid=(rows.n_tiles,),
        in_specs=[rows.spec(d), _full(g), sh_spec, sc_spec, _full(rw_t), _full(rb_col), _full(cnt_in)],
        out_specs=[pl.BlockSpec((rows.tile * nb, LANES), lambda i: (i, 0)), kt, kt, kt,
                   pl.BlockSpec((n_exp, LANES), lambda i: (0, 0))],
        scratch_shapes=[pltpu.VMEM((n_exp, LANES), F32)],
        compiler_params=_params("arbitrary"),
        name="moe_router",
    )(y, g, sh_arr, sc_arr, rw_t, rb_col, cnt_in)


def _dispatch_kernel(slot_ref, slot2_ref, pad_ref, nv_ref, h_ref, h2_ref, xs_ref, sem, zeros,
                     *, tm, n_rows, n_rows2, nb, n_exp, tmf, n_tiles):
    i = pl.program_id(0)

    @pl.when(i == 0)
    def _():
        zeros[...] = jnp.zeros_like(zeros)
        fills = [pltpu.make_async_copy(zeros, xs_ref.at[pl.ds(pl.multiple_of(pad_ref[e] * nb, nb), tmf * nb), :], sem)
                 for e in range(n_exp)]
        for c in fills:
            c.start()
        for c in fills:
            c.wait()

        def tail(j):
            return pltpu.make_async_copy(zeros, xs_ref.at[pl.ds(pl.multiple_of(j * (tmf * nb), tmf * nb), tmf * nb), :], sem)

        def start_tail(j, carry):
            tail(j).start()
            return carry

        def wait_tail(j, carry):
            tail(j).wait()
            return carry

        lax.fori_loop(nv_ref[0], n_tiles, start_tail, 0)
        lax.fori_loop(nv_ref[0], n_tiles, wait_tail, 0)

    def scatter(src_ref, slots, base, count, stride):
        def issue(t, carry):
            for k in range(TOP_K):
                s = slots[k * stride + base + t]
                pltpu.make_async_copy(src_ref.at[pl.ds(pl.multiple_of(t * nb, nb), nb), :],
                                      xs_ref.at[pl.ds(pl.multiple_of(s * nb, nb), nb), :], sem).start(priority=k % 2)
            return carry

        lax.fori_loop(0, count, issue, 0, unroll=2)
        for k in range(TOP_K):
            pltpu.make_async_copy(src_ref, xs_ref.at[pl.ds(0, count * nb), :], sem).wait()

    scatter(h_ref, slot_ref, i * tm, tm, n_rows)

    @pl.when(i == pl.num_programs(0) - 1)
    def _():
        scatter(h2_ref, slot2_ref, 0, n_rows2, n_rows2)


def _dispatch(rows, slots, slots2, pad_start, n_valid, h, h2, n_tiles, tmf, nb):
    n_exp = pad_start.shape[0]
    n_rows2 = h2.shape[0] // nb
    grid_spec = pltpu.PrefetchScalarGridSpec(
        num_scalar_prefetch=4,
        grid=(rows.n_tiles,),
        in_specs=[pl.BlockSpec((rows.tile * nb, LANES), lambda i, *_: (i, 0)),
                  pl.BlockSpec(h2.shape, lambda i, *_: (0, 0))],
        out_specs=pl.BlockSpec(memory_space=pl.ANY),
        scratch_shapes=[pltpu.SemaphoreType.DMA(()), pltpu.VMEM((tmf * nb, LANES), F32)],
    )
    return pl.pallas_call(
        functools.partial(_dispatch_kernel, tm=rows.tile, n_rows=rows.rows, n_rows2=n_rows2, nb=nb, n_exp=n_exp,
                          tmf=tmf, n_tiles=n_tiles),
        out_shape=jax.ShapeDtypeStruct((n_tiles * tmf * nb, LANES), F32),
        grid_spec=grid_spec,
        compiler_params=_params("arbitrary"),
        name="moe_dispatch",
    )(slots, slots2, pad_start, n_valid, h, h2)


def _split_gate_up_kernel(w_ref, wg_ref, wu_ref, wt_ref, *, chunk):
    half = chunk // 2
    for c in range(w_ref.shape[1] // chunk):
        for r in range(w_ref.shape[0] // LANES):
            cols = slice(r * LANES, (r + 1) * LANES)
            wt_ref[...] = w_ref[cols, c * chunk:(c + 1) * chunk].T
            wg_ref[c * half:(c + 1) * half, cols] = wt_ref[pl.ds(0, half, stride=2), :].astype(BF16)
            wu_ref[c * half:(c + 1) * half, cols] = wt_ref[pl.ds(1, half, stride=2), :].astype(BF16)


def _split_gate_up(w_up):
    n_l, n_e, d, f2 = w_up.shape
    chunk = min(512, f2)
    blk = lambda rows, width: pl.BlockSpec((None, None, rows, width), lambda l, e: (l, e, 0, 0))
    return pl.pallas_call(
        functools.partial(_split_gate_up_kernel, chunk=chunk),
        out_shape=[jax.ShapeDtypeStruct((n_l, n_e, f2 // 2, d), BF16)] * 2,
        grid=(n_l, n_e),
        in_specs=[blk(d, f2)],
        out_specs=[blk(f2 // 2, d)] * 2,
        scratch_shapes=[pltpu.VMEM((chunk, LANES), F32)],
        compiler_params=_params("arbitrary", "arbitrary"),
        name="moe_split_gate_up",
    )(w_up)


def _ffn_kernel(te_ref, tv_ref, nv_ref, xs_ref, wg_ref, wu_ref, bg_ref, bu_ref, wd_ref, bd_ref, ys_ref, *, tmf, nb):
    del te_ref, nv_ref
    i = pl.program_id(0)

    @pl.when(tv_ref[i] == 1)
    def _():
        x = _load_row_tiles(xs_ref, tmf, nb).astype(BF16)
        gate = jnp.minimum(_dot_nt(x, wg_ref[...]) + bg_ref[...], SWIGLU_LIMIT)
        up = jnp.clip(_dot_nt(x, wu_ref[...]) + bu_ref[...], -SWIGLU_LIMIT, SWIGLU_LIMIT)
        act = (up + 1.0) * gate * _sigmoid(SWIGLU_ALPHA * gate)
        _store_row_tiles(ys_ref, _dot(act.astype(BF16), wd_ref[...].astype(BF16)) + bd_ref[...])

    @pl.when(tv_ref[i] == 0)
    def _():
        ys_ref[...] = jnp.zeros_like(ys_ref)


def _expert_ffn(tile_expert, tile_valid, n_valid, xs, layer, wg, wu, bg, bu, wd, bd, tmf, nb):
    n_tiles = tile_expert.shape[0]
    f, d = wg.shape[2], wg.shape[3]
    ex = lambda i, te, tv, nv: (te[i], 0, 0)
    lex = lambda i, te, tv, nv: (layer, te[i], 0, 0)
    grid_spec = pltpu.PrefetchScalarGridSpec(
        num_scalar_prefetch=3,
        grid=(n_tiles,),
        in_specs=[pl.BlockSpec((tmf * nb, LANES), lambda i, te, tv, nv: (jnp.minimum(i, nv[0] - 1), 0)),
                  pl.BlockSpec((None, None, f, d), lex), pl.BlockSpec((None, None, f, d), lex),
                  pl.BlockSpec((None, 1, f), ex), pl.BlockSpec((None, 1, f), ex),
                  pl.BlockSpec((None, None, f, d), lex), pl.BlockSpec((None, 1, d), ex)],
        out_specs=pl.BlockSpec((tmf * nb, LANES), lambda i, te, tv, nv: (i, 0)),
    )
    return pl.pallas_call(
        functools.partial(_ffn_kernel, tmf=tmf, nb=nb),
        out_shape=jax.ShapeDtypeStruct((n_tiles * tmf * nb, LANES), F32),
        grid_spec=grid_spec,
        compiler_params=_params("arbitrary"),
        name="moe_expert_ffn",
    )(tile_expert, tile_valid, n_valid, xs, wg, wu, bg, bu, wd, bd)


def _combine_kernel(slot_ref, ys_ref, w_ref, y_ref, gm_ref, fg_ref, o_ref, buf, sem, *, tm, n_rows, nb, final_norm):
    i = pl.program_id(0)
    n = pl.num_programs(0)

    def gather(tile, slot):
        def issue(t, carry):
            for k in range(TOP_K):
                s = slot_ref[k * n_rows + tile * tm + t]
                pltpu.make_async_copy(ys_ref.at[pl.ds(pl.multiple_of(s * nb, nb), nb), :],
                                      buf.at[slot, k, pl.ds(pl.multiple_of(t * nb, nb), nb), :],
                                      sem.at[slot]).start(priority=k % 2)
            return carry

        lax.fori_loop(0, tm, issue, 0, unroll=2)

    @pl.when(i == 0)
    def _():
        gather(0, 0)

    cur = i % 2

    @pl.when(i + 1 < n)
    def _():
        gather(i + 1, 1 - cur)

    for k in range(TOP_K):
        pltpu.make_async_copy(ys_ref.at[pl.ds(0, tm * nb), :], buf.at[cur, k], sem.at[cur]).wait()
    ws = [jnp.broadcast_to(w_ref[:, k:k + 1], (tm, LANES)) for k in range(TOP_K)]
    for c in range(nb):
        cols = slice(c * LANES, (c + 1) * LANES)
        acc = ws[0] * buf[cur, 0, pl.ds(c, tm, stride=nb), :]
        for k in range(1, TOP_K):
            acc = acc + ws[k] * buf[cur, k, pl.ds(c, tm, stride=nb), :]
        o_ref[:, cols] = y_ref[:, cols] + gm_ref[:, cols] * acc
    if final_norm:
        x = o_ref[...]
        o_ref[...] = x * lax.rsqrt(jnp.mean(x * x, axis=-1, keepdims=True) + 1e-6) * fg_ref[...]


def _combine(rows, slots_flat, ys, w_tok, y, mod, layer, final_g, final_norm):
    d = y.shape[1]
    nb = d // LANES
    gm_arr, gm_spec = rows.mod(mod, layer, 5, d)
    grid_spec = pltpu.PrefetchScalarGridSpec(
        num_scalar_prefetch=1,
        grid=(rows.n_tiles,),
        in_specs=[pl.BlockSpec(memory_space=pl.ANY),
                  pl.BlockSpec((rows.tile, TOP_K), lambda i, s: (i, 0)),
                  pl.BlockSpec((rows.tile, d), lambda i, s: (i, 0)), gm_spec,
                  pl.BlockSpec(final_g.shape, lambda i, s: (0, 0))],
        out_specs=pl.BlockSpec((rows.tile, d), lambda i, s: (i, 0)),
        scratch_shapes=[pltpu.VMEM((2, TOP_K, rows.tile * nb, LANES), F32), pltpu.SemaphoreType.DMA((2,))],
    )
    return pl.pallas_call(
        functools.partial(_combine_kernel, tm=rows.tile, n_rows=rows.rows, nb=nb, final_norm=final_norm),
        out_shape=jax.ShapeDtypeStruct(y.shape, F32),
        grid_spec=grid_spec,
        compiler_params=_params("arbitrary"),
        name="moe_combine",
    )(slots_flat, ys, w_tok, y, gm_arr, final_g)


def _moe_layer(groups, ys, layer, g_ffn, router_w, router_b, wg_all, wu_all, exp_b_up, wd_all, exp_b_down, tmf,
               final_g, final_norm):
    n_exp = router_w.shape[1]
    d = router_w.shape[0]
    rw_t = router_w.T
    rb_col = router_b.reshape(n_exp, 1)
    cnt = jnp.zeros((n_exp, LANES), F32)
    routed = []
    for (rows, m), y in zip(groups, ys):
        h, idx, w, rank, cnt = _route(rows, y, g_ffn, m, layer, rw_t, rb_col, cnt)
        routed.append((h, idx, w, rank))
    total = sum(rows.rows for rows, _ in groups) * TOP_K
    n_tiles = -(-total // tmf) + n_exp
    counts = cnt[:, 0].astype(I32)
    tiles_e = (counts + tmf - 1) // tmf
    tile_end = jnp.cumsum(tiles_e)
    starts = (tile_end - tiles_e) * tmf
    tile_ids = jnp.arange(n_tiles, dtype=I32)
    tile_expert = jnp.minimum(jnp.sum((tile_end[None, :] <= tile_ids[:, None]).astype(I32), axis=1), n_exp - 1)
    tile_valid = (tile_ids < tile_end[-1]).astype(I32)
    nb = d // LANES
    pad_start = starts + counts
    e_ids = jnp.arange(n_exp, dtype=I32)
    slots = [(jnp.sum(jnp.where(idx[..., None] == e_ids, starts, 0), axis=-1) + rank).reshape(-1)
             for (h, idx, w, rank) in routed]
    xs = _dispatch(groups[0][0], slots[0], slots[1], pad_start, tile_end[-1:], routed[0][0], routed[1][0],
                   n_tiles + 1, tmf, nb)
    bg = exp_b_up[:, None, 0::2]
    bu = exp_b_up[:, None, 1::2]
    y_sorted = _expert_ffn(tile_expert, tile_valid, tile_end[-1:], xs, layer, wg_all, wu_all, bg, bu,
                           wd_all, exp_b_down[:, None, :], tmf, nb)
    outs = []
    for (rows, m), y, sl, (h, idx, w, rank) in zip(groups, ys, slots, routed):
        outs.append(_combine(rows, sl, y_sorted, w.T, y, m, layer, final_g, final_norm))
    return outs


def kernel(x_prompt, x_sample, c_prompt, c_sample, cache_k, cache_v, cache_logf, page_table, state_ret, state_ssm, state_conv, ada_w, ada_b, norm_mix_g, norm_ffn_g, norm_final_g, att_w_in, att_b_f, ret_gn_g, ret_gn_b, att_w_out, ssm_w_in, ssm_conv_w, ssm_conv_b, ssm_dt_bias, ssm_a_log, ssm_d, ssm_norm_g, ssm_w_out, router_w, router_b, exp_w_up, exp_b_up, exp_w_down, exp_b_down):
    bp, sp, d = x_prompt.shape
    bs = x_sample.shape[0]
    depth = ada_w.shape[0]
    n_pages, page = page_table.shape[1], cache_k.shape[2]
    past = n_pages * page
    h_a, dh_a = cache_k.shape[3], cache_k.shape[4]
    h_b, dk_b, dv_b = state_ret.shape[2:]
    h_c, d_state, hd_c = state_ssm.shape[2:]
    d_inner = h_c * hd_c
    conv_w_len, conv_dim = ssm_conv_w.shape[1:]
    gn = (conv_dim - d_inner) // 2
    n_groups = gn // d_state
    assert conv_w_len == 4 and hd_c * 2 == LANES and h_a <= LANES and h_c <= LANES

    rows_p = _Rows(bp, sp, 256)
    rows_m = _Rows(bp, sp, 512)
    rows_s = _Rows(bs, 1, bs)
    yp = x_prompt.reshape(bp * sp, d)
    ys = x_sample.reshape(bs, d)

    mod = _modulation_all(jnp.concatenate([c_prompt, c_sample], axis=0), ada_w, ada_b)
    mod_p, mod_s = mod[:, :bp], mod[:, bp:]

    wa, wr, wvr = h_a * dh_a, h_b * dk_b, h_b * dv_b
    att_segs = []
    off = 0
    for wdt in (wa, wa, wa, wr, wr, wvr, wvr, LANES):
        att_segs.append((off, wdt))
        off += wdt
    ssm_segs = ((0, d_inner), (d_inner, conv_dim), (d_inner + conv_dim, LANES))

    cos_p, sin_p = _rope_tables(jnp.arange(sp), h_b, dk_b)
    cos_s, sin_s = _rope_tables(jnp.full((1,), past), h_b, dk_b)
    log_gammas = [math.log1p(-2.0 ** (-RET_DECAY_BASE - h)) for h in range(h_b)]
    gamma_col = jnp.broadcast_to(jnp.asarray(np.exp(log_gammas), F32).reshape(1, h_b, 1, 1), (bs, h_b, 1, 1))
    kt_pages = jnp.transpose(cache_k, (0, 1, 3, 4, 2))
    vt_pages = jnp.transpose(cache_v, (0, 1, 3, 4, 2))
    lf_pages_t = jnp.swapaxes(cache_logf, 2, 3)
    n_exp = router_w.shape[2]
    tmf = 512 if (bp * sp + bs) * TOP_K >= 512 * n_exp else 64
    e_mat = (jnp.arange(LANES)[:, None] == (jnp.arange(d_inner) // hd_c)[None, :]).astype(BF16)
    wg_all, wu_all = _split_gate_up(exp_w_up)

    k_p, v_p, f_p, k_s, v_s, f_s, ret_p, ret_s = [], [], [], [], [], [], [], []
    ssm_p, ssm_s, conv_p, conv_s = [], [], [], []
    for l in range(depth):
        j = l // 2
        g_mix = norm_mix_g[l][None]
        if l % 2 == 0:
            w = att_w_in[j]
            o0 = 3 * wa
            n_att = att_segs[-1][0] + LANES
            w_perm32 = jnp.concatenate(
                [w[:, :o0], w[:, o0 + h_a:], w[:, o0:o0 + h_a],
                 jnp.zeros((d, -(-n_att // F32_COL_BLOCK) * F32_COL_BLOCK - n_att + LANES - h_a), F32)], axis=1)
            w_perm = w_perm32[:, :n_att].astype(BF16)
            b_f_pad = jnp.pad(att_b_f[j], (0, LANES - h_a))[None]
            w1_32, w2_32 = att_w_out[j][:wa], att_w_out[j][wa:]
            w1, w2 = w1_32.astype(BF16), w2_32.astype(BF16)
            gn_g, gn_b = ret_gn_g[j][None], ret_gn_b[j][None]
            qa, ka, va, qr, kr, vr, gate, fa, ka_t, va_t = _project(rows_p, yp, g_mix, mod_p, l, w_perm, att_segs,
                                                                    also_transposed=(1, 2))
            logf, fcum = _forget_gates(rows_p, fa, b_f_pad)
            hp8 = -(-h_a // 8) * 8
            fk_t = jnp.transpose(fcum.reshape(bp, sp, LANES)[:, :, :hp8], (0, 2, 1))
            o_fox = _fox_prompt(qa, ka, va, fcum, fk_t, bp, sp, dh_a, h_a)
            o_ret, st = _retention_prompt(qr, kr, vr, cos_p, sin_p, bp, sp, h_b, dk_b, dv_b)
            yp = _att_output(rows_p, o_fox, o_ret, gate, yp, mod_p, l, gn_g, gn_b, w1, w2, h_b, dv_b)
            k_p.append(ka_t)
            v_p.append(va_t)
            f_p.append(logf[:, :h_a].reshape(bp, sp, h_a))
            ret_p.append(st)
            qa, ka, va, qr, kr, vr, gate, fa = _project_f32(rows_s, ys, g_mix, mod_s, l, w_perm32, att_segs)
            logf, _ = _forget_gates(rows_s, fa, b_f_pad)
            o_fox = _fox_decode(page_table, qa[:, None, :], ka[:, None, :], va[:, None, :], logf[:, None, :],
                                kt_pages, vt_pages, lf_pages_t, j).reshape(bs, wa)
            qr2, kr2 = _rope_rows(qr, kr, cos_s, sin_s, dk_b)
            st, y_ret = _state_step(state_ret[j], gamma_col, kr2.reshape(bs, h_b, dk_b, 1),
                                    qr2.reshape(bs, h_b, dk_b, 1), vr.reshape(bs, h_b, 1, dv_b))
            ys = _att_output(rows_s, o_fox, y_ret.reshape(bs, wvr), gate, ys, mod_s, l, gn_g, gn_b, w1_32, w2_32,
                             h_b, dv_b)
            k_s.append(ka.reshape(bs, 1, h_a, dh_a))
            v_s.append(va.reshape(bs, 1, h_a, dh_a))
            f_s.append(logf[:, :h_a].reshape(bs, 1, h_a))
            ret_s.append(st)
        else:
            n_ssm = ssm_segs[-1][0] + LANES
            w_pad32 = jnp.pad(ssm_w_in[j], ((0, 0), (0, -(-n_ssm // F32_COL_BLOCK) * F32_COL_BLOCK - n_ssm + LANES - h_c)))
            w_pad = w_pad32[:, :n_ssm].astype(BF16)
            cw, cb = ssm_conv_w[j], ssm_conv_b[j][None]
            dtb = jnp.pad(ssm_dt_bias[j], (0, LANES - h_c))[None]
            alog = jnp.pad(ssm_a_log[j], (0, LANES - h_c))[None]
            dsk = jnp.repeat(ssm_d[j], hd_c)[None]
            ng = ssm_norm_g[j][None]
            w_out = ssm_w_out[j].astype(BF16)
            z, xr, dtr = _project(rows_p, yp, g_mix, mod_p, l, w_pad, ssm_segs)
            y_n, st2, cv = _ssd_prompt(xr, z, dtr, cw, cb, dtb, alog, dsk, ng, e_mat, bp, sp,
                                       d_inner, n_groups, d_state, hd_c)
            yp = _matmul_residual(rows_p, y_n, yp, mod_p, l, w_out)
            st = st2.reshape(bp, h_c // 2, d_state, 2, hd_c).transpose(0, 1, 3, 2, 4).reshape(bp, h_c, d_state, hd_c)
            ssm_p.append(st)
            conv_p.append(cv[:, 8 - (conv_w_len - 1):, :])
            z, xr, dtr = _project_f32(rows_s, ys, g_mix, mod_s, l, w_pad32, ssm_segs)
            taps = [state_conv[j][:, i, :] for i in range(conv_w_len - 1)]
            x, bm, cm, v, a = _ssd_step_prep(xr, taps, dtr, cw, cb, dtb, alog, e_mat, d_inner, gn)
            st_t, y_s = _state_step_t(jnp.swapaxes(state_ssm[j], -1, -2), a[:, None, :],
                                      bm.reshape(bs, n_groups, d_state), cm.reshape(bs, n_groups, d_state),
                                      v.reshape(bs, h_c, hd_c))
            y_n = _ssd_step_post(y_s.reshape(bs, d_inner), x, z, dsk, ng, n_groups)
            ys = _matmul_residual(rows_s, y_n, ys, mod_s, l, ssm_w_out[j])
            ssm_s.append(jnp.swapaxes(st_t, -1, -2))
            conv_s.append(jnp.concatenate([state_conv[j][:, 1:, :], xr[:, None, :]], axis=1))
        yp, ys = _moe_layer([(rows_m, mod_p), (rows_s, mod_s)], [yp, ys], l, norm_ffn_g[l][None],
                            router_w[l], router_b[l], wg_all, wu_all, exp_b_up[l], exp_w_down, exp_b_down[l], tmf,
                            norm_final_g[None], l == depth - 1)
    y_prompt = yp.reshape(bp, sp, d)
    y_sample = ys.reshape(bs, 1, d)

    def token_major(parts):
        t = jnp.stack(parts).reshape(len(parts), bp, h_a, dh_a, sp)
        return jnp.transpose(t, (0, 1, 4, 2, 3))

    return (y_prompt, y_sample,
            token_major(k_p), token_major(v_p), jnp.stack(f_p),
            jnp.stack(k_s), jnp.stack(v_s), jnp.stack(f_s),
            jnp.stack(ret_p), jnp.stack(ret_s),
            jnp.stack(ssm_p), jnp.stack(ssm_s),
            jnp.stack(conv_p), jnp.stack(conv_s))
```
